```python
import math
import jax, jax.numpy as jnp
from jax import lax
import numpy as np

D_MODEL = 1024
BATCH = 16
SEQ = 256
DEPTH = 4
DEC_BATCH = 4
DEC_SEQ = 4096
PAST_LEN = 256

GRID_W = 64
H_A = 4
DK_A = 64
DV_A = 128
H_B = 4
DK_B = 64
DV_B = 128
H_C = 4
DK_C = 128
DV_C = 128
W_A = H_A * DV_A
W_B = H_B * DV_B
W_C = H_C * DV_C
N_DIR = 2
CONV_K = 5
CHUNK = 64
Q_BLOCK = 128
ROPE_BASE = 10000.0
EPS = 1e-6
IN_WIDTHS = (H_A * 2 * DK_A, H_A * 2 * DK_A, W_A, W_A,
             H_B * DK_B, H_B * DK_B, W_B, W_B,
             H_C * DK_C, H_C * DK_C, W_C, W_C, N_DIR * H_C, N_DIR * H_C,
             3 * D_MODEL)
D_IN = sum(IN_WIDTHS)

kernel_name = 'hybrid_diff_ret_gdn_dit_step'


def rms_norm(x, w):
    xf = x.astype(jnp.float32)
    y = xf * lax.rsqrt(jnp.mean(xf * xf, axis=-1, keepdims=True) + EPS)
    return (y * w.astype(jnp.float32)).astype(x.dtype)


def l2_norm(x):
    xf = x.astype(jnp.float32)
    return (xf * lax.rsqrt(jnp.sum(xf * xf, axis=-1, keepdims=True) + EPS)).astype(x.dtype)


def axial_rope_tables(n_tokens, dim, dtype):
    n_rows = n_tokens // GRID_W
    row = jnp.repeat(jnp.arange(n_rows, dtype=jnp.float32), GRID_W)
    col = jnp.tile(jnp.arange(GRID_W, dtype=jnp.float32), n_rows)
    n_freq = dim // 4
    inv = 1.0 / (ROPE_BASE ** (jnp.arange(n_freq, dtype=jnp.float32) / n_freq))
    ar = row[:, None] * inv
    ac = col[:, None] * inv
    ang = jnp.concatenate([ar, ar, ac, ac], axis=-1)
    return jnp.cos(ang).astype(dtype), jnp.sin(ang).astype(dtype)


def apply_axial_rope(x, cos, sin):
    extra = x.ndim - 3
    c = cos.reshape(cos.shape[:1] + (1,) * extra + cos.shape[1:])
    s = sin.reshape(sin.shape[:1] + (1,) * extra + sin.shape[1:])
    d = x.shape[-1]
    xs = x.reshape(x.shape[:-1] + (2, 2, d // 4))
    rot = jnp.stack([-xs[..., 1, :], xs[..., 0, :]], axis=-2).reshape(x.shape)
    return x * c + rot * s


def depthwise_conv(x, w):
    k, ch = w.shape
    return lax.conv_general_dilated(x, w[:, None, :].astype(x.dtype), (1,), ((k // 2, k // 2),),
                                    dimension_numbers=('NWC', 'WIO', 'NWC'), feature_group_count=ch)


def diff_attention(q, k, v, lam):
    b, t, h, _, d = q.shape
    dv = v.shape[-1]
    nb = t // Q_BLOCK
    qb = jnp.moveaxis(q.reshape(b, nb, Q_BLOCK, h, 2, d), 1, 0)
    scale = d ** -0.5

    def one_block(qi):
        s = jnp.einsum('bqhmd,bshmd->bmhqs', qi, k).astype(jnp.float32) * scale
        p = jax.nn.softmax(s, axis=-1)
        wts = (p[:, 0] - lam * p[:, 1]).astype(v.dtype)
        return jnp.einsum('bhqs,bshe->bqhe', wts, v)

    o = lax.map(one_block, qb)
    return jnp.moveaxis(o, 0, 1).reshape(b, t, h, dv)


def _to_chunks(a):
    b, t = a.shape[:2]
    a = a.reshape((b, t // CHUNK, CHUNK) + a.shape[2:]).astype(jnp.float32)
    return jnp.moveaxis(a, (1, 3), (0, 2))


def _from_chunks(o):
    o = jnp.moveaxis(o, (0, 2), (1, 3))
    return o.reshape((o.shape[0], o.shape[1] * o.shape[2]) + o.shape[3:])


def retention_scan(q, k, v, log_gamma, s0):
    dk = q.shape[-1]
    qc = _to_chunks(q)
    kc = _to_chunks(k) * dk ** -0.5
    vc = _to_chunks(v)
    i = jnp.arange(CHUNK, dtype=jnp.float32)
    rel = i[:, None] - i[None, :]
    causal = rel >= 0
    dmat = jnp.where(causal, jnp.exp(jnp.where(causal, rel, 0.0)[None] * log_gamma[:, None, None]), 0.0)
    q_dec = jnp.exp((i + 1.0)[None] * log_gamma[:, None])
    k_dec = jnp.exp((CHUNK - 1.0 - i)[None] * log_gamma[:, None])
    c_dec = jnp.exp(CHUNK * log_gamma)

    def step(s, inp):
        q_i, k_i, v_i = inp
        intra = jnp.einsum('bhid,bhjd->bhij', q_i, k_i) * dmat
        o = (jnp.einsum('bhij,bhje->bhie', intra, v_i)
             + jnp.einsum('bhid,bhde->bhie', q_i, s) * q_dec[:, :, None])
        s = s * c_dec[:, None, None] + jnp.einsum('bhjd,bhje->bhde', k_i * k_dec[:, :, None], v_i)
        return s, o

    s, o = lax.scan(step, s0.astype(jnp.float32), (qc, kc, vc))
    return _from_chunks(o), s


def gated_delta_scan(q, k, v, g, beta, s0):
    dk = q.shape[-1]
    qc = _to_chunks(q) * dk ** -0.5
    kc = _to_chunks(k)
    vc = _to_chunks(v)
    gc = jnp.cumsum(_to_chunks(g), axis=-1)
    bc = _to_chunks(beta)[..., None]
    kb = kc * bc
    lower = jnp.tril(jnp.ones((CHUNK, CHUNK), dtype=bool))
    strict = jnp.tril(jnp.ones((CHUNK, CHUNK), dtype=jnp.float32), -1)
    decay = jnp.exp(jnp.where(lower, gc[..., :, None] - gc[..., None, :], -jnp.inf))
    a = jnp.einsum('nbhid,nbhjd->nbhij', kb, kc) * decay * strict
    eye = jnp.eye(CHUNK, dtype=jnp.float32)
    t_inv = lax.linalg.triangular_solve(a + eye, jnp.broadcast_to(eye, a.shape), left_side=True,
                                        lower=True, unit_diagonal=True)
    u = jnp.einsum('nbhij,nbhjd->nbhid', t_inv, vc * bc)
    w = jnp.einsum('nbhij,nbhjd->nbhid', t_inv, kb * jnp.exp(gc)[..., None])
    qk = jnp.einsum('nbhid,nbhjd->nbhij', qc, kc) * decay

    def step(s, inp):
        q_i, k_i, u_i, w_i, g_i, qk_i = inp
        v_new = u_i - jnp.einsum('bhcd,bhde->bhce', w_i, s)
        o = (jnp.einsum('bhcd,bhde->bhce', q_i * jnp.exp(g_i)[..., None], s)
             + jnp.einsum('bhij,bhje->bhie', qk_i, v_new))
        g_last = g_i[..., -1:]
        s = s * jnp.exp(g_last)[..., None] + jnp.einsum(
            'bhcd,bhce->bhde', k_i * jnp.exp(g_last - g_i)[..., None], v_new)
        return s, o

    s, o = lax.scan(step, s0.astype(jnp.float32), (qc, kc, u, w, gc, qk))
    return _from_chunks(o), s


def trunk_layer(x, mod, layer_idx, p, rope, ctx):
    b, t, _ = x.shape
    f32 = jnp.float32
    shift, scale, gate = jnp.split(mod, 3, axis=-1)
    h = rms_norm(x, p['norm_w']) * (1.0 + scale) + shift
    proj = h @ p['w_in']
    split_pts = np.cumsum(IN_WIDTHS)[:-1].tolist()
    (aq, ak, av, az, bq, bk, bv, bz, cq, ck, cv, cz, c_beta, c_alpha, merge_logits) = jnp.split(proj, split_pts, axis=-1)

    lam_init = 0.8 - 0.6 * math.exp(-0.3 * layer_idx)
    qa = rms_norm(aq.reshape(b, t, H_A, 2, DK_A), p['qk_norm_w'][0])
    ka = rms_norm(ak.reshape(b, t, H_A, 2, DK_A), p['qk_norm_w'][1])
    va = av.reshape(b, t, H_A, DV_A)
    if rope is not None:
        qa = apply_axial_rope(qa, rope[0], rope[1])
        ka_att = apply_axial_rope(ka, rope[0], rope[1])
    else:
        ka_att = ka
    lv = p['diff_lambda'].astype(f32)
    lam = jnp.exp(jnp.sum(lv[0] * lv[1])) - jnp.exp(jnp.sum(lv[2] * lv[3])) + lam_init
    if ctx is None:
        k_all, v_all = ka_att, va
    else:
        k_all = jnp.concatenate([ka_att, ctx[0].astype(ka_att.dtype)], axis=1)
        v_all = jnp.concatenate([va, ctx[1].astype(va.dtype)], axis=1)
    oa = rms_norm(diff_attention(qa, k_all, v_all, lam), p['subln_w']) * (1.0 - lam_init)
    ya = (oa.reshape(b, t, W_A) * jax.nn.silu(az)) @ p['w_branch'][0]

    qr = bq.reshape(b, t, H_B, DK_B)
    kr = bk.reshape(b, t, H_B, DK_B)
    vr = bv.reshape(b, t, H_B, DV_B)
    if rope is not None:
        qr = apply_axial_rope(qr, rope[0], rope[1])
        kr = apply_axial_rope(kr, rope[0], rope[1])
    log_gamma = jax.nn.log_sigmoid(p['ret_decay'].astype(f32))
    s_ret = jnp.zeros((b, N_DIR, H_B, DK_B, DV_B), f32) if ctx is None else ctx[2]
    or_f, sr_f = retention_scan(qr, kr, vr, log_gamma[0], s_ret[:, 0])
    or_b, sr_b = retention_scan(jnp.flip(qr, 1), jnp.flip(kr, 1), jnp.flip(vr, 1), log_gamma[1], s_ret[:, 1])
    orr = rms_norm((or_f + jnp.flip(or_b, 1)).astype(x.dtype), p['ret_norm_w'])
    yb = (orr.reshape(b, t, W_B) * jax.nn.silu(bz)) @ p['w_branch'][1]

    qkv = jax.nn.silu(depthwise_conv(jnp.concatenate([cq, ck, cv], axis=-1), p['conv_w']))
    cq, ck, cv = jnp.split(qkv, [H_C * DK_C, 2 * H_C * DK_C], axis=-1)
    qd = l2_norm(cq.reshape(b, t, H_C, DK_C))
    kd = l2_norm(ck.reshape(b, t, H_C, DK_C))
    vd = cv.reshape(b, t, H_C, DV_C)
    beta = jax.nn.sigmoid(c_beta.reshape(b, t, N_DIR, H_C).astype(f32))
    g = -jnp.exp(p['gdn_a_log'].astype(f32)) * jax.nn.softplus(
        c_alpha.reshape(b, t, N_DIR, H_C).astype(f32) + p['gdn_dt_bias'].astype(f32))
    s_gdn = jnp.zeros((b, N_DIR, H_C, DK_C, DV_C), f32) if ctx is None else ctx[3]
    od_f, sd_f = gated_delta_scan(qd, kd, vd, g[:, :, 0], beta[:, :, 0], s_gdn[:, 0])
    od_b, sd_b = gated_delta_scan(jnp.flip(qd, 1), jnp.flip(kd, 1), jnp.flip(vd, 1),
                                  jnp.flip(g[:, :, 1], 1), jnp.flip(beta[:, :, 1], 1), s_gdn[:, 1])
    od = rms_norm((od_f + jnp.flip(od_b, 1)).astype(x.dtype), p['gdn_norm_w'])
    yc = (od.reshape(b, t, W_C) * jax.nn.silu(cz)) @ p['w_branch'][2]

    g_a, g_b, g_c = jnp.split(jax.nn.sigmoid(merge_logits), 3, axis=-1)
    y = (g_a * ya + g_b * yb + g_c * yc) @ p['w_out']
    x = x + gate * y
    if ctx is None:
        return x, (ka, va, jnp.stack([sr_f, sr_b], axis=1).astype(x.dtype),
                   jnp.stack([sd_f, sd_b], axis=1).astype(x.dtype))
    return x, None


def setup_inputs(seed: int = 0) -> dict:
    key = jax.random.key(seed)
    ks = jax.random.split(key, 24)
    f32 = jnp.float32

    def nrm(k, shape, s):
        return s * jax.random.normal(k, shape, f32)

    ret_base = 1.0 - 2.0 ** (-5.0 - jnp.arange(H_B, dtype=f32))
    ret_logit = jnp.log(ret_base) - jnp.log1p(-ret_base)
    dt = jnp.exp(jax.random.uniform(ks[20], (DEPTH, N_DIR, H_C), f32, math.log(1e-3), math.log(1e-1)))
    return {
        'x_prompt': nrm(ks[0], (BATCH, SEQ, D_MODEL), 1.0),
        'x_sample': nrm(ks[1], (DEC_BATCH, DEC_SEQ, D_MODEL), 1.0),
        'cache_attn_k': nrm(ks[2], (DEC_BATCH, DEPTH, PAST_LEN, H_A, 2, DK_A), 1.0),
        'cache_attn_v': nrm(ks[3], (DEC_BATCH, DEPTH, PAST_LEN, H_A, DV_A), 1.0),
        'state_ret': nrm(ks[4], (DEC_BATCH, DEPTH, N_DIR, H_B, DK_B, DV_B), 1.0),
        'state_gdn': nrm(ks[5], (DEC_BATCH, DEPTH, N_DIR, H_C, DK_C, DV_C), 0.1),
        'c': nrm(ks[6], (DEC_BATCH, D_MODEL), 1.0),
        'c_ctx': nrm(ks[7], (D_MODEL,), 1.0),
        'norm_w': 1.0 + nrm(ks[8], (DEPTH, D_MODEL), 0.02),
        'w_ada': nrm(ks[9], (DEPTH, D_MODEL, 3 * D_MODEL), 0.5 * D_MODEL ** -0.5),
        'b_ada': nrm(ks[10], (DEPTH, 3 * D_MODEL), 0.01),
        'w_in': nrm(ks[11], (DEPTH, D_MODEL, D_IN), D_MODEL ** -0.5),
        'qk_norm_w': 1.0 + nrm(ks[12], (DEPTH, 2, DK_A), 0.02),
        'diff_lambda': nrm(ks[13], (DEPTH, 4, DK_A), 0.1),
        'subln_w': 1.0 + nrm(ks[14], (DEPTH, DV_A), 0.02),
        'ret_decay': ret_logit + nrm(ks[15], (DEPTH, N_DIR, H_B), 0.1),
        'ret_norm_w': 1.0 + nrm(ks[16], (DEPTH, DV_B), 0.02),
        'conv_w': nrm(ks[17], (DEPTH, CONV_K, 3 * W_C), CONV_K ** -0.5),
        'gdn_a_log': jnp.log(jax.random.uniform(ks[18], (DEPTH, N_DIR, H_C), f32, 1.0, 16.0)),
        'gdn_dt_bias': dt + jnp.log(-jnp.expm1(-dt)),
        'gdn_norm_w': 1.0 + nrm(ks[19], (DEPTH, DV_C), 0.02),
        'w_branch': nrm(ks[21], (DEPTH, 3, W_A, D_MODEL), W_A ** -0.5),
        'w_out': nrm(ks[22], (DEPTH, D_MODEL, D_MODEL), D_MODEL ** -0.5),
    }


def reference(x_prompt, x_sample, cache_attn_k, cache_attn_v, state_ret, state_gdn, c, c_ctx,
              norm_w, w_ada, b_ada, w_in, qk_norm_w, diff_lambda, subln_w, ret_decay, ret_norm_w,
              conv_w, gdn_a_log, gdn_dt_bias, gdn_norm_w, w_branch, w_out):
    ctx_cond = jax.nn.silu(c_ctx)
    lat_cond = jax.nn.silu(c)
    cos, sin = axial_rope_tables(x_sample.shape[1], DK_A, x_sample.dtype)
    y_p, y_s = x_prompt, x_sample
    ks_out, vs_out, rs_out, gs_out = [], [], [], []
    for l in range(DEPTH):
        p = {'norm_w': norm_w[l], 'w_in': w_in[l], 'qk_norm_w': qk_norm_w[l],
             'diff_lambda': diff_lambda[l], 'subln_w': subln_w[l], 'ret_decay': ret_decay[l],
             'ret_norm_w': ret_norm_w[l], 'conv_w': conv_w[l], 'gdn_a_log': gdn_a_log[l],
             'gdn_dt_bias': gdn_dt_bias[l], 'gdn_norm_w': gdn_norm_w[l],
             'w_branch': w_branch[l], 'w_out': w_out[l]}
        m_ctx = (ctx_cond @ w_ada[l] + b_ada[l])[None, None, :]
        m_lat = (lat_cond @ w_ada[l] + b_ada[l])[:, None, :]
        y_p, (k_l, v_l, r_l, g_l) = trunk_layer(y_p, m_ctx, l, p, None, None)
        y_s, _ = trunk_layer(y_s, m_lat, l, p, (cos, sin),
                             (cache_attn_k[:, l], cache_attn_v[:, l], state_ret[:, l], state_gdn[:, l]))
        ks_out.append(k_l)
        vs_out.append(v_l)
        rs_out.append(r_l)
        gs_out.append(g_l)
    return (y_p, y_s, jnp.stack(ks_out, axis=1), jnp.stack(vs_out, axis=1),
            jnp.stack(rs_out, axis=1), jnp.stack(gs_out, axis=1))
```

```python
import functools
import math

import jax
import jax.numpy as jnp
from jax import lax
from jax.experimental import pallas as pl
from jax.experimental.pallas import tpu as pltpu

F32 = jnp.float32
BF16 = jnp.bfloat16
HIGHEST = lax.Precision.HIGHEST

N_HEADS = 4
DK_A = 64
DK_B = 64
HEAD_W = 128
CONV_K = 5
GRID_W = 64
ROPE_BASE = 10000.0
EPS = 1e-6
LANES = 128
CHUNK = 128
VMEM_LIMIT = 56 * 1024 * 1024

OFF_AQ, OFF_AK, OFF_AV, OFF_AZ = 0, 512, 1024, 1536
OFF_BQ, OFF_BK, OFF_BV, OFF_BZ = 2048, 2304, 2560, 3072
OFF_CQ, OFF_CK, OFF_CV, OFF_CZ = 3584, 4096, 4608, 5120
OFF_BA = 5632
N_MIX = 5632


def _cparams(sem):
    return pltpu.CompilerParams(dimension_semantics=sem, vmem_limit_bytes=VMEM_LIMIT)


def _sigmoid(x):
    return 1.0 / (1.0 + jnp.exp(-x))


def _silu(x):
    return x * _sigmoid(x)


def _softplus(x):
    return jnp.maximum(x, 0.0) + jnp.log1p(jnp.exp(-jnp.abs(x)))


def _dot(a, b):
    return jnp.dot(a, b, preferred_element_type=F32)


def _dot_nt(a, b):
    return lax.dot_general(a, b, (((1,), (1,)), ((), ())), preferred_element_type=F32)


def _lane_iota(shape):
    return lax.broadcasted_iota(jnp.int32, shape, len(shape) - 1)


def _row_iota(shape):
    return lax.broadcasted_iota(jnp.int32, shape, len(shape) - 2)


def _norm_halves(x):
    lo = _lane_iota(x.shape) < 64
    x2 = x * x
    s0 = jnp.sum(jnp.where(lo, x2, 0.0), axis=-1, keepdims=True)
    s1 = jnp.sum(jnp.where(lo, 0.0, x2), axis=-1, keepdims=True)
    return x * lax.rsqrt(jnp.where(lo, s0, s1) * (1.0 / 64.0) + EPS)


def _rope(x, cos, sin_a, sin_b):
    return (x * cos + pltpu.roll(x, LANES - 16, 1) * sin_a + pltpu.roll(x, 16, 1) * sin_b)


def _ada_kernel(cond_ref, w_ref, b_ref, o_ref):
    c = cond_ref[...]
    o_ref[0] = jnp.dot(_silu(c), w_ref[0], precision=HIGHEST,
                       preferred_element_type=F32) + b_ref[0]


def _ada_call(cond, w_ada, b_ada):
    depth, d, d3 = w_ada.shape
    tn = 1024
    return pl.pallas_call(
        _ada_kernel,
        grid=(depth, d3 // tn),
        in_specs=[pl.BlockSpec((8, d), lambda l, j: (0, 0)),
                  pl.BlockSpec((1, d, tn), lambda l, j: (l, 0, j)),
                  pl.BlockSpec((1, 1, tn), lambda l, j: (l, 0, j))],
        out_specs=pl.BlockSpec((1, 8, tn), lambda l, j: (l, 0, j)),
        out_shape=jax.ShapeDtypeStruct((depth, 8, d3), F32),
        compiler_params=_cparams(("parallel", "parallel")),
    )(cond, w_ada, b_ada.reshape(depth, 1, d3))


def _proj_kernel(x_ref, mod_ref, nw_ref, w_ref, o_ref, h_ref, *, d):
    @pl.when(pl.program_id(1) == 0)
    def _():
        x = x_ref[...]
        y = x * lax.rsqrt(jnp.mean(x * x, axis=-1, keepdims=True) + EPS) * nw_ref[...]
        mod = mod_ref[0]
        h_ref[...] = (y * (1.0 + mod[:, d:2 * d]) + mod[:, 0:d]).astype(BF16)

    o_ref[...] = _dot(h_ref[...], w_ref[...])


def _proj_call(x2, mod, mod_row, norm_w, w_pad, tm):
    n, d = x2.shape
    n_cols = w_pad.shape[1]
    tn = 1536
    return pl.pallas_call(
        functools.partial(_proj_kernel, d=d),
        grid=(n // tm, n_cols // tn),
        in_specs=[pl.BlockSpec((tm, d), lambda i, j: (i, 0)),
                  pl.BlockSpec((1, 1, 3 * d), lambda i, j: (mod_row(i), 0, 0)),
                  pl.BlockSpec((1, d), lambda i, j: (0, 0)),
                  pl.BlockSpec((d, tn), lambda i, j: (0, j))],
        out_specs=pl.BlockSpec((tm, tn), lambda i, j: (i, j)),
        out_shape=jax.ShapeDtypeStruct((n, n_cols), F32),
        scratch_shapes=[pltpu.VMEM((tm, d), BF16)],
        compiler_params=_cparams(("parallel", "arbitrary")),
    )(x2, mod, norm_w, w_pad)


def _prep_ctx_kernel(ak_ref, av_ref, bq_ref, bk_ref, wk_ref,
                     kall_ref, vall_ref, rq_ref, rk_ref, ka_ref, va_ref):
    wk = wk_ref[...]
    for h in range(N_HEADS):
        sl = slice(h * HEAD_W, (h + 1) * HEAD_W)
        kn = _norm_halves(ak_ref[:, sl]) * wk
        ka_ref[0, :, sl] = kn
        kall_ref[0, h] = kn.astype(BF16)
        v = av_ref[:, sl]
        va_ref[0, :, sl] = v
        vall_ref[0, h] = v.astype(BF16)
    rq_ref[...] = bq_ref[...]
    rk_ref[...] = bk_ref[...] * (DK_B ** -0.5)


def _prep_lat_kernel(ak_ref, av_ref, bq_ref, bk_ref, wk_ref, cos_ref, sa_ref, sb_ref,
                     ck_ref, cv_ref, kall_ref, vall_ref, rq_ref, rk_ref, *, nt):
    t = pl.program_id(1)

    @pl.when(t < nt)
    def _():
        wk = wk_ref[...]
        cos, sa, sb = cos_ref[...], sa_ref[...], sb_ref[...]
        for h in range(N_HEADS):
            sl = slice(h * HEAD_W, (h + 1) * HEAD_W)
            kn = _norm_halves(ak_ref[:, sl]) * wk
            kall_ref[0, h] = _rope(kn, cos, sa, sb).astype(BF16)
            vall_ref[0, h] = av_ref[:, sl].astype(BF16)
        for e in range(2):
            sl = slice(e * LANES, (e + 1) * LANES)
            rq_ref[:, sl] = _rope(bq_ref[:, sl], cos, sa, sb)
            rk_ref[:, sl] = _rope(bk_ref[:, sl], cos, sa, sb) * (DK_B ** -0.5)

    @pl.when(t == nt)
    def _():
        for h in range(N_HEADS):
            sl = slice(h * HEAD_W, (h + 1) * HEAD_W)
            kall_ref[0, h] = ck_ref[:, sl].astype(BF16)
            vall_ref[0, h] = cv_ref[:, sl].astype(BF16)


def _prep_call(p, b, t, wk_row, rope, cache, layer):
    tk = 256
    nt = t // tk
    n = b * t
    if rope is None:
        return pl.pallas_call(
            _prep_ctx_kernel,
            grid=(b, nt),
            in_specs=[pl.BlockSpec((tk, 512), lambda bi, ti: (bi * nt + ti, OFF_AK // 512)),
                      pl.BlockSpec((tk, 512), lambda bi, ti: (bi * nt + ti, OFF_AV // 512)),
                      pl.BlockSpec((tk, 256), lambda bi, ti: (bi * nt + ti, OFF_BQ // 256)),
                      pl.BlockSpec((tk, 256), lambda bi, ti: (bi * nt + ti, OFF_BK // 256)),
                      pl.BlockSpec((1, LANES), lambda bi, ti: (0, 0))],
            out_specs=[pl.BlockSpec((1, N_HEADS, tk, HEAD_W), lambda bi, ti: (bi, 0, ti, 0)),
                       pl.BlockSpec((1, N_HEADS, tk, HEAD_W), lambda bi, ti: (bi, 0, ti, 0)),
                       pl.BlockSpec((tk, 256), lambda bi, ti: (bi * nt + ti, 0)),
                       pl.BlockSpec((tk, 256), lambda bi, ti: (bi * nt + ti, 0)),
                       pl.BlockSpec((1, tk, 512), lambda bi, ti: (bi, ti, 0)),
                       pl.BlockSpec((1, tk, 512), lambda bi, ti: (bi, ti, 0))],
            out_shape=[jax.ShapeDtypeStruct((b, N_HEADS, t, HEAD_W), BF16),
                       jax.ShapeDtypeStruct((b, N_HEADS, t, HEAD_W), BF16),
                       jax.ShapeDtypeStruct((n, 256), F32),
                       jax.ShapeDtypeStruct((n, 256), F32),
                       jax.ShapeDtypeStruct((b, t, 512), F32),
                       jax.ShapeDtypeStruct((b, t, 512), F32)],
            compiler_params=_cparams(("parallel", "parallel")),
        )(p, p, p, p, wk_row)
    cos, sa, sb = rope
    cache_k, cache_v = cache
    past = cache_k.shape[2]
    assert past == tk
    s = t + past

    def tok(bi, ti):
        return bi * nt + jnp.minimum(ti, nt - 1)

    return pl.pallas_call(
        functools.partial(_prep_lat_kernel, nt=nt),
        grid=(b, nt + 1),
        in_specs=[pl.BlockSpec((tk, 512), lambda bi, ti: (tok(bi, ti), OFF_AK // 512)),
                  pl.BlockSpec((tk, 512), lambda bi, ti: (tok(bi, ti), OFF_AV // 512)),
                  pl.BlockSpec((tk, 256), lambda bi, ti: (tok(bi, ti), OFF_BQ // 256)),
                  pl.BlockSpec((tk, 256), lambda bi, ti: (tok(bi, ti), OFF_BK // 256)),
                  pl.BlockSpec((1, LANES), lambda bi, ti: (0, 0)),
                  pl.BlockSpec((tk, LANES), lambda bi, ti: (jnp.minimum(ti, nt - 1), 0)),
                  pl.BlockSpec((tk, LANES), lambda bi, ti: (jnp.minimum(ti, nt - 1), 0)),
                  pl.BlockSpec((tk, LANES), lambda bi, ti: (jnp.minimum(ti, nt - 1), 0)),
                  pl.BlockSpec((None, None, past, 512), lambda bi, ti: (bi, layer, 0, 0)),
                  pl.BlockSpec((None, None, past, 512), lambda bi, ti: (bi, layer, 0, 0))],
        out_specs=[pl.BlockSpec((1, N_HEADS, tk, HEAD_W), lambda bi, ti: (bi, 0, ti, 0)),
                   pl.BlockSpec((1, N_HEADS, tk, HEAD_W), lambda bi, ti: (bi, 0, ti, 0)),
                   pl.BlockSpec((tk, 256), lambda bi, ti: (tok(bi, ti), 0)),
                   pl.BlockSpec((tk, 256), lambda bi, ti: (tok(bi, ti), 0))],
        out_shape=[jax.ShapeDtypeStruct((b, N_HEADS, s, HEAD_W), BF16),
                   jax.ShapeDtypeStruct((b, N_HEADS, s, HEAD_W), BF16),
                   jax.ShapeDtypeStruct((n, 256), F32),
                   jax.ShapeDtypeStruct((n, 256), F32)],
        compiler_params=_cparams(("parallel", "arbitrary")),
    )(p, p, p, p, wk_row, cos, sa, sb, cache_k, cache_v)


def _attn_kernel(*refs, nc, ck, lam_init, rope):
    if rope:
        (q_ref, z_ref, k_ref, v_ref, wq_ref, dl_ref, sw_ref, cos_ref, sa_ref, sb_ref,
         o_ref, s_ref) = refs
    else:
        q_ref, z_ref, k_ref, v_ref, wq_ref, dl_ref, sw_ref, o_ref, s_ref = refs
    tq = q_ref.shape[0]
    qn = _norm_halves(q_ref[...]) * wq_ref[...]
    if rope:
        qn = _rope(qn, cos_ref[...], sa_ref[...], sb_ref[...])
    qn = qn * (DK_A ** -0.5)
    lo = _lane_iota(qn.shape) < 64
    q0 = jnp.where(lo, qn, 0.0).astype(BF16)
    q1 = jnp.where(lo, 0.0, qn).astype(BF16)
    nh = ck // LANES

    def fold_max(m, s):
        for i in range(nh):
            m = jnp.maximum(m, s[:, i * LANES:(i + 1) * LANES])
        return m

    def fold_sum(l, p):
        for i in range(nh):
            l = l + p[:, i * LANES:(i + 1) * LANES]
        return l

    def pass1(c, carry):
        m0, m1 = carry
        kc = k_ref[0, 0, pl.ds(pl.multiple_of(c * ck, ck), ck), :]
        s0 = _dot_nt(q0, kc)
        s1 = _dot_nt(q1, kc)
        s_ref[0, c] = s0
        s_ref[1, c] = s1
        return fold_max(m0, s0), fold_max(m1, s1)

    neg = jnp.full((tq, LANES), -jnp.inf, F32)
    m0, m1 = lax.fori_loop(0, nc, pass1, (neg, neg))
    m0 = jnp.max(m0, axis=-1, keepdims=True)
    m1 = jnp.max(m1, axis=-1, keepdims=True)

    def pass2(c, carry):
        l0, l1, a0, a1 = carry
        vc = v_ref[0, 0, pl.ds(pl.multiple_of(c * ck, ck), ck), :]
        p0 = jnp.exp(s_ref[0, c] - m0)
        p1 = jnp.exp(s_ref[1, c] - m1)
        a0 = a0 + _dot(p0.astype(BF16), vc)
        a1 = a1 + _dot(p1.astype(BF16), vc)
        return fold_sum(l0, p0), fold_sum(l1, p1), a0, a1

    zero = jnp.zeros((tq, LANES), F32)
    l0, l1, a0, a1 = lax.fori_loop(0, nc, pass2, (zero, zero, zero, zero))
    l0 = jnp.sum(l0, axis=-1, keepdims=True)
    l1 = jnp.sum(l1, axis=-1, keepdims=True)

    lv = dl_ref[...]
    lam = (jnp.exp(jnp.sum(lv[0:1] * lv[1:2], axis=-1, keepdims=True))
           - jnp.exp(jnp.sum(lv[2:3] * lv[3:4], axis=-1, keepdims=True)) + lam_init)
    o = a0 * (1.0 / l0) - lam * (a1 * (1.0 / l1))
    o = o * lax.rsqrt(jnp.mean(o * o, axis=-1, keepdims=True) + EPS) * sw_ref[...]
    o_ref[...] = o * (1.0 - lam_init) * _silu(z_ref[...])


def _attn_call(p, kall, vall, wq_row, dlam, subln_row, rope, b, t, lam_init):
    tq = 256
    ck = 256
    s = kall.shape[2]
    nc = s // ck
    nq = t // tq
    in_specs = [pl.BlockSpec((tq, HEAD_W), lambda bi, h, i: (bi * nq + i, OFF_AQ // HEAD_W + h)),
                pl.BlockSpec((tq, HEAD_W), lambda bi, h, i: (bi * nq + i, OFF_AZ // HEAD_W + h)),
                pl.BlockSpec((1, 1, s, HEAD_W), lambda bi, h, i: (bi, h, 0, 0)),
                pl.BlockSpec((1, 1, s, HEAD_W), lambda bi, h, i: (bi, h, 0, 0)),
                pl.BlockSpec((1, LANES), lambda bi, h, i: (0, 0)),
                pl.BlockSpec((4, DK_A), lambda bi, h, i: (0, 0)),
                pl.BlockSpec((1, LANES), lambda bi, h, i: (0, 0))]
    args = [p, p, kall, vall, wq_row, dlam, subln_row]
    if rope is not None:
        in_specs += [pl.BlockSpec((tq, LANES), lambda bi, h, i: (i, 0))] * 3
        args += list(rope)
    return pl.pallas_call(
        functools.partial(_attn_kernel, nc=nc, ck=ck, lam_init=lam_init, rope=rope is not None),
        grid=(b, N_HEADS, nq),
        in_specs=in_specs,
        out_specs=pl.BlockSpec((tq, HEAD_W), lambda bi, h, i: (bi * nq + i, h)),
        out_shape=jax.ShapeDtypeStruct((b * t, N_HEADS * HEAD_W), F32),
        scratch_shapes=[pltpu.VMEM((2, nc, tq, ck), F32)],
        compiler_params=_cparams(("parallel", "parallel", "arbitrary")),
    )(*args)


def _ret_kernel(*refs, nchunk, has_state):
    if has_state:
        (q_ref, k_ref, v_ref, z_ref, d256_ref, d128_ref, nw_ref, s0_ref,
         o_ref, s_ref, dm_ref, qd_ref, kd_ref) = refs
        st_ref = None
    else:
        (q_ref, k_ref, v_ref, z_ref, d256_ref, d128_ref, nw_ref,
         o_ref, st_ref, s_ref, dm_ref, qd_ref, kd_ref) = refs
    c_len = CHUNK
    lg256 = -_softplus(-d256_ref[...])
    lg128 = -_softplus(-d128_ref[...])
    ri = _row_iota((c_len, c_len)).astype(F32)
    ci = _lane_iota((c_len, c_len)).astype(F32)
    rcol256 = _row_iota((c_len, 2 * HEAD_W)).astype(F32)
    rcol128 = _row_iota((c_len, LANES)).astype(F32)
    blockmask = ((_row_iota((2 * DK_B, 2 * HEAD_W)) < DK_B)
                 == (_lane_iota((2 * DK_B, 2 * HEAD_W)) < HEAD_W))
    for d in range(2):
        rel = (ri - ci) if d == 0 else (ci - ri)
        keep = rel >= 0
        for e in range(2):
            lg = lg256[d:d + 1, e * HEAD_W:e * HEAD_W + 1]
            dm_ref[d, e] = jnp.where(keep, jnp.exp(jnp.where(keep, rel, 0.0) * lg), 0.0)
        qpow = (rcol256 + 1.0) if d == 0 else (c_len - rcol256)
        qd_ref[d] = jnp.exp(qpow * lg256[d:d + 1])
        kpow = (c_len - 1.0 - rcol128) if d == 0 else rcol128
        kd_ref[d] = jnp.exp(kpow * lg128[d:d + 1])
        if has_state:
            s_ref[d] = jnp.zeros((2 * DK_B, 2 * HEAD_W), F32)
            for e in range(2):
                s_ref[d, e * DK_B:(e + 1) * DK_B, e * HEAD_W:(e + 1) * HEAD_W] = s0_ref[d, e]
        else:
            s_ref[d] = jnp.zeros((2 * DK_B, 2 * HEAD_W), F32)
    cdec = [jnp.exp(float(c_len) * lg256[d:d + 1]) for d in range(2)]
    lane_lo = _lane_iota((c_len, LANES)) < DK_B
    nw = nw_ref[...]

    def step(c, d, final):
        t0 = pl.multiple_of(c * c_len, c_len)
        q = q_ref[pl.ds(t0, c_len), :]
        k = k_ref[pl.ds(t0, c_len), :]
        v = v_ref[pl.ds(t0, c_len), :]
        st = s_ref[d]
        inter = _dot(q.astype(BF16), st.astype(BF16)) * qd_ref[d]
        kb = k.astype(BF16)
        vb = v.astype(BF16)
        for e in range(2):
            sl = slice(e * HEAD_W, (e + 1) * HEAD_W)
            qe = jnp.where(lane_lo if e == 0 else jnp.logical_not(lane_lo), q, 0.0).astype(BF16)
            sc = _dot_nt(qe, kb) * dm_ref[d, e]
            o_e = _dot(sc.astype(BF16), vb[:, sl]) + inter[:, sl]
            if final:
                o_e = o_e + o_ref[pl.ds(t0, c_len), sl]
                o_e = o_e * lax.rsqrt(jnp.mean(o_e * o_e, axis=-1, keepdims=True) + EPS) * nw
                o_e = o_e * _silu(z_ref[pl.ds(t0, c_len), sl])
            o_ref[pl.ds(t0, c_len), sl] = o_e
        kdk = (k * kd_ref[d]).astype(BF16)
        kv = lax.dot_general(kdk, vb, (((0,), (0,)), ((), ())), preferred_element_type=F32)
        s_ref[d] = jnp.where(blockmask, st * cdec[d] + kv, 0.0)

    def fwd(c, carry):
        step(c, 0, False)
        return carry

    def bwd(i, carry):
        step(nchunk - 1 - i, 1, True)
        return carry

    lax.fori_loop(0, nchunk, fwd, 0)
    lax.fori_loop(0, nchunk, bwd, 0)
    if not has_state:
        for d in range(2):
            for e in range(2):
                st_ref[d, e] = s_ref[d, e * DK_B:(e + 1) * DK_B, e * HEAD_W:(e + 1) * HEAD_W]


def _ret_call(p, rq, rk, d256, d128, nw_row, state, b, t, layer):
    nchunk = t // CHUNK
    hp = N_HEADS // 2
    in_specs = [pl.BlockSpec((t, LANES), lambda bi, h: (bi, h)),
                pl.BlockSpec((t, LANES), lambda bi, h: (bi, h)),
                pl.BlockSpec((t, 256), lambda bi, h: (bi, OFF_BV // 256 + h)),
                pl.BlockSpec((t, 256), lambda bi, h: (bi, OFF_BZ // 256 + h)),
                pl.BlockSpec((None, 2, 256), lambda bi, h: (h, 0, 0)),
                pl.BlockSpec((None, 2, LANES), lambda bi, h: (h, 0, 0)),
                pl.BlockSpec((1, LANES), lambda bi, h: (0, 0))]
    args = [rq, rk, p, p, d256, d128, nw_row]
    scratch = [pltpu.VMEM((2, 2 * DK_B, 2 * HEAD_W), F32),
               pltpu.VMEM((2, 2, CHUNK, CHUNK), F32),
               pltpu.VMEM((2, CHUNK, 2 * HEAD_W), F32),
               pltpu.VMEM((2, CHUNK, LANES), F32)]
    o_spec = pl.BlockSpec((t, 256), lambda bi, h: (bi, h))
    o_shape = jax.ShapeDtypeStruct((b * t, N_HEADS * HEAD_W), F32)
    if state is not None:
        in_specs.append(pl.BlockSpec((None, None, 2, 2, DK_B, HEAD_W),
                                     lambda bi, h: (bi, layer, 0, h, 0, 0)))
        args.append(state)
        out_specs, out_shape = o_spec, o_shape
    else:
        out_specs = [o_spec, pl.BlockSpec((None, 2, 2, DK_B, HEAD_W), lambda bi, h: (bi, 0, h, 0, 0))]
        out_shape = [o_shape, jax.ShapeDtypeStruct((b, 2, N_HEADS, DK_B, HEAD_W), F32)]
    return pl.pallas_call(
        functools.partial(_ret_kernel, nchunk=nchunk, has_state=state is not None),
        grid=(b, hp),
        in_specs=in_specs,
        out_specs=out_specs,
        out_shape=out_shape,
        scratch_shapes=scratch,
        compiler_params=_cparams(("parallel", "parallel")),
    )(*args)


def _gdn_kernel(*refs, t, has_state):
    if has_state:
        (cq_ref, ck_ref, cv_ref, cz_ref, ba_ref, wq_ref, wk_ref, wv_ref, al_ref, dt_ref, nw_ref,
         s0_ref, o_ref, xp_ref, qs_ref, ks_ref, vs_ref, s_ref) = refs
        st_ref = None
    else:
        (cq_ref, ck_ref, cv_ref, cz_ref, ba_ref, wq_ref, wk_ref, wv_ref, al_ref, dt_ref, nw_ref,
         o_ref, st_ref, xp_ref, qs_ref, ks_ref, vs_ref, s_ref) = refs
    c_len = CHUNK
    nchunk = t // c_len
    h = pl.program_id(1)
    pad = 8

    zrow = jnp.zeros((pad, LANES), F32)
    for a, src in enumerate((cq_ref, ck_ref, cv_ref)):
        xp_ref[a, 0:pad, :] = zrow
        xp_ref[a, pad + t:2 * pad + t, :] = zrow
    win = 256
    nwin = t // win

    def copy_body(i, carry):
        r0 = pl.multiple_of(i * win, win)
        for a, src in enumerate((cq_ref, ck_ref, cv_ref)):
            xp_ref[a, pl.ds(pad + r0, win), :] = src[pl.ds(r0, win), :]
        return carry

    lax.fori_loop(0, nwin, copy_body, 0)

    def conv_body(i, carry):
        r0 = pl.multiple_of(i * win, win)
        for a, (w_ref, dst) in enumerate(((wq_ref, qs_ref), (wk_ref, ks_ref), (wv_ref, vs_ref))):
            xw = xp_ref[a, pl.ds(r0, win + 2 * pad), :]
            w = w_ref[...]
            acc = jnp.zeros((win, LANES), F32)
            for j in range(CONV_K):
                off = pad - CONV_K // 2 + j
                acc = acc + xw[off:off + win, :] * w[j:j + 1, :]
            y = _silu(acc)
            if a < 2:
                y = y * lax.rsqrt(jnp.sum(y * y, axis=-1, keepdims=True) + EPS)
            dst[pl.ds(r0, win), :] = y
        return carry

    lax.fori_loop(0, nwin, conv_body, 0)

    ri = _row_iota((c_len, c_len))
    ci = _lane_iota((c_len, c_len))
    lane = _lane_iota((c_len, LANES))
    alog = al_ref[...]
    dtb = dt_ref[...]
    for d in range(2):
        s_ref[d] = s0_ref[d] if has_state else jnp.zeros((HEAD_W, HEAD_W), F32)

    def step(c, d, first):
        t0 = pl.multiple_of(c * c_len, c_len)
        incl = (ri >= ci) if d == 0 else (ci >= ri)
        strict = (ri > ci) if d == 0 else (ci > ri)
        tri = jnp.where(incl, 1.0, 0.0)
        ba = ba_ref[pl.ds(t0, c_len), :]
        beta_all = _sigmoid(ba)
        g_all = -jnp.exp(alog) * _softplus(ba + dtb)
        beta = jnp.sum(jnp.where(lane == d * N_HEADS + h, beta_all, 0.0), axis=-1, keepdims=True)
        g = jnp.sum(jnp.where(lane == 2 * N_HEADS + d * N_HEADS + h, g_all, 0.0),
                    axis=-1, keepdims=True)
        gcb = jnp.dot(tri, jnp.broadcast_to(g, (c_len, LANES)), precision=HIGHEST,
                      preferred_element_type=F32)
        gc = gcb[:, 0:1]
        gr = gcb.T[0:1, :]
        diff = gc - gr
        ex = jnp.exp(jnp.where(incl, diff, 0.0))
        dec = jnp.where(incl, ex, 0.0)
        dec_s = jnp.where(strict, ex, 0.0)
        q = qs_ref[pl.ds(t0, c_len), :] * (HEAD_W ** -0.5)
        k = ks_ref[pl.ds(t0, c_len), :]
        v = vs_ref[pl.ds(t0, c_len), :]
        kb16 = k.astype(BF16)
        kbeta = k * beta
        a_mat = _dot_nt(kbeta.astype(BF16), kb16) * dec_s
        n_mat = -jnp.where((ri >> 1) == (ci >> 1), a_mat, 0.0)
        sh = 1
        while (1 << sh) < c_len:
            off = jnp.logical_and((ri >> (sh + 1)) == (ci >> (sh + 1)), (ri >> sh) != (ci >> sh))
            l_mat = jnp.where(off, a_mat, 0.0)
            x_mat = l_mat + _dot(l_mat.astype(BF16), n_mat.astype(BF16))
            n_mat = n_mat - x_mat - _dot(n_mat.astype(BF16), x_mat.astype(BF16))
            sh += 1
        egc = jnp.exp(gc)
        rhs = jnp.concatenate([v * beta, kbeta * egc], axis=1)
        uw = rhs + _dot(n_mat.astype(BF16), rhs.astype(BF16))
        u = uw[:, 0:HEAD_W]
        w = uw[:, HEAD_W:2 * HEAD_W]
        qk = _dot_nt(q.astype(BF16), kb16) * dec
        st = s_ref[d]
        st16 = st.astype(BF16)
        v_new = u - _dot(w.astype(BF16), st16)
        vn16 = v_new.astype(BF16)
        o = _dot((q * egc).astype(BF16), st16) + _dot(qk.astype(BF16), vn16)
        g_tot = gc[c_len - 1:c_len, :] if d == 0 else gc[0:1, :]
        kd = (k * jnp.exp(g_tot - gc)).astype(BF16)
        kv = lax.dot_general(kd, vn16, (((0,), (0,)), ((), ())), preferred_element_type=F32)
        s_ref[d] = st * jnp.exp(g_tot) + kv
        if first:
            o_ref[pl.ds(t0, c_len), :] = o
        else:
            o = o + o_ref[pl.ds(t0, c_len), :]
            o = o * lax.rsqrt(jnp.mean(o * o, axis=-1, keepdims=True) + EPS) * nw_ref[...]
            o_ref[pl.ds(t0, c_len), :] = o * _silu(cz_ref[pl.ds(t0, c_len), :])

    def fwd(c, carry):
        step(c, 0, True)
        return carry

    def bwd(i, carry):
        step(nchunk - 1 - i, 1, False)
        return carry

    lax.fori_loop(0, nchunk, fwd, 0)
    lax.fori_loop(0, nchunk, bwd, 0)
    if not has_state:
        for d in range(2):
            st_ref[d] = s_ref[d]


def _gdn_call(p, conv_w, al_row, dt_row, nw_row, state, b, t, layer):
    def col(off):
        return lambda bi, h: (bi, off // LANES + h)

    in_specs = [pl.BlockSpec((t, LANES), col(OFF_CQ)),
                pl.BlockSpec((t, LANES), col(OFF_CK)),
                pl.BlockSpec((t, LANES), col(OFF_CV)),
                pl.BlockSpec((t, LANES), col(OFF_CZ)),
                pl.BlockSpec((t, LANES), lambda bi, h: (bi, OFF_BA // LANES)),
                pl.BlockSpec((CONV_K, LANES), lambda bi, h: (0, h)),
                pl.BlockSpec((CONV_K, LANES), lambda bi, h: (0, N_HEADS + h)),
                pl.BlockSpec((CONV_K, LANES), lambda bi, h: (0, 2 * N_HEADS + h)),
                pl.BlockSpec((1, LANES), lambda bi, h: (0, 0)),
                pl.BlockSpec((1, LANES), lambda bi, h: (0, 0)),
                pl.BlockSpec((1, LANES), lambda bi, h: (0, 0))]
    args = [p, p, p, p, p, conv_w, conv_w, conv_w, al_row, dt_row, nw_row]
    scratch = [pltpu.VMEM((3, t + 16, LANES), F32),
               pltpu.VMEM((t, LANES), F32),
               pltpu.VMEM((t, LANES), F32),
               pltpu.VMEM((t, LANES), F32),
               pltpu.VMEM((2, HEAD_W, HEAD_W), F32)]
    o_spec = pl.BlockSpec((t, LANES), lambda bi, h: (bi, h))
    o_shape = jax.ShapeDtypeStruct((b * t, N_HEADS * HEAD_W), F32)
    if state is not None:
        in_specs.append(pl.BlockSpec((None, None, 2, None, HEAD_W, HEAD_W),
                                     lambda bi, h: (bi, layer, 0, h, 0, 0)))
        args.append(state)
        out_specs, out_shape = o_spec, o_shape
    else:
        out_specs = [o_spec, pl.BlockSpec((None, 2, None, HEAD_W, HEAD_W),
                                          lambda bi, h: (bi, 0, h, 0, 0))]
        out_shape = [o_shape, jax.ShapeDtypeStruct((b, 2, N_HEADS, HEAD_W, HEAD_W), F32)]
    return pl.pallas_call(
        functools.partial(_gdn_kernel, t=t, has_state=state is not None),
        grid=(b, N_HEADS),
        in_specs=in_specs,
        out_specs=out_specs,
        out_shape=out_shape,
        scratch_shapes=scratch,
        compiler_params=_cparams(("parallel", "parallel")),
    )(*args)


def _out_kernel(ya_ref, yb_ref, yc_ref, mg_ref, x_ref, mod_ref, wb_ref, wo_ref, o_ref, *, d):
    ya = _dot(ya_ref[...].astype(BF16), wb_ref[0])
    yb = _dot(yb_ref[...].astype(BF16), wb_ref[1])
    yc = _dot(yc_ref[...].astype(BF16), wb_ref[2])
    y = (_sigmoid(mg_ref[:, 0:d]) * ya + _sigmoid(mg_ref[:, d:2 * d]) * yb
         + _sigmoid(mg_ref[:, 2 * d:3 * d]) * yc)
    out = _dot(y.astype(BF16), wo_ref[...])
    gate = mod_ref[0][:, 2 * d:3 * d]
    o_ref[...] = x_ref[...] + gate * out


def _out_call(ya, yb, yc, p, x2, mod, mod_row, wb, wo, mg_block, tm):
    n, d = x2.shape
    w_br = ya.shape[1]
    return pl.pallas_call(
        functools.partial(_out_kernel, d=d),
        grid=(n // tm,),
        in_specs=[pl.BlockSpec((tm, w_br), lambda i: (i, 0)),
                  pl.BlockSpec((tm, w_br), lambda i: (i, 0)),
                  pl.BlockSpec((tm, w_br), lambda i: (i, 0)),
                  pl.BlockSpec((tm, 3 * d), lambda i: (i, mg_block)),
                  pl.BlockSpec((tm, d), lambda i: (i, 0)),
                  pl.BlockSpec((1, 1, 3 * d), lambda i: (mod_row(i), 0, 0)),
                  pl.BlockSpec((3, w_br, d), lambda i: (0, 0, 0)),
                  pl.BlockSpec((d, d), lambda i: (0, 0))],
        out_specs=pl.BlockSpec((tm, d), lambda i: (i, 0)),
        out_shape=jax.ShapeDtypeStruct((n, d), F32),
        compiler_params=_cparams(("parallel",)),
    )(ya, yb, yc, p, x2, mod, wb, wo)


def _rope_tables(n_tokens, dtype):
    n_rows = n_tokens // GRID_W
    row = jnp.repeat(jnp.arange(n_rows, dtype=jnp.float32), GRID_W)
    col = jnp.tile(jnp.arange(GRID_W, dtype=jnp.float32), n_rows)
    n_freq = DK_A // 4
    inv = 1.0 / (ROPE_BASE ** (jnp.arange(n_freq, dtype=jnp.float32) / n_freq))
    ar = row[:, None] * inv
    ac = col[:, None] * inv
    ang = jnp.concatenate([ar, ar, ac, ac], axis=-1)
    cos = jnp.tile(jnp.cos(ang).astype(dtype), (1, 2))
    sin = jnp.tile(jnp.sin(ang).astype(dtype), (1, 2))
    first = (jnp.arange(LANES) % 32) < 16
    sin_a = jnp.where(first, -sin, 0.0)
    sin_b = jnp.where(first, 0.0, sin)
    return cos, sin_a, sin_b


def _layer(x2, mod, mod_row, layer, wts, rope, ctx, b, t, tm):
    p = _proj_call(x2, mod, mod_row, wts["norm_w"], wts["w_in"], tm)
    cache = None if ctx is None else (ctx[0], ctx[1])
    prep = _prep_call(p, b, t, wts["wk_row"], rope, cache, layer)
    kall, vall, rq, rk = prep[:4]
    lam_init = 0.8 - 0.6 * math.exp(-0.3 * layer)
    ya = _attn_call(p, kall, vall, wts["wq_row"], wts["diff_lambda"], wts["subln_row"], rope,
                    b, t, lam_init)
    ret = _ret_call(p, rq, rk, wts["d256"], wts["d128"], wts["ret_norm_row"],
                    None if ctx is None else ctx[2], b, t, layer)
    gdn = _gdn_call(p, wts["conv_w"], wts["al_row"], wts["dt_row"], wts["gdn_norm_row"],
                    None if ctx is None else ctx[3], b, t, layer)
    if ctx is None:
        yb, s_ret = ret
        yc, s_gdn = gdn
        extras = (prep[4], prep[5], s_ret, s_gdn)
    else:
        yb, yc = ret, gdn
        extras = None
    mg_block = wts["mg_off"] // (3 * x2.shape[1])
    x2 = _out_call(ya, yb, yc, p, x2, mod, mod_row, wts["w_branch"], wts["w_out"], mg_block, tm)
    return x2, extras


def kernel(x_prompt, x_sample, cache_attn_k, cache_attn_v, state_ret, state_gdn, c, c_ctx,
           norm_w, w_ada, b_ada, w_in, qk_norm_w, diff_lambda, subln_w, ret_decay, ret_norm_w,
           conv_w, gdn_a_log, gdn_dt_bias, gdn_norm_w, w_branch, w_out):
    b_ctx, t_ctx, d = x_prompt.shape
    b_lat, t_lat, _ = x_sample.shape
    depth = w_in.shape[0]
    past = cache_attn_k.shape[2]
    assert b_lat <= 4 and d % LANES == 0

    cond = jnp.zeros((8, d), F32).at[:b_lat].set(c).at[4].set(c_ctx)
    mods = _ada_call(cond, w_ada, b_ada)

    mg_off = -(-(OFF_BA + LANES) // (3 * d)) * (3 * d)
    n_cols = mg_off + 3 * d
    n_cols = -(-n_cols // 1536) * 1536
    w_pad = jnp.zeros((depth, d, n_cols), BF16)
    w_pad = w_pad.at[:, :, :N_MIX + 16].set(w_in[:, :, :N_MIX + 16].astype(BF16))
    w_pad = w_pad.at[:, :, mg_off:mg_off + 3 * d].set(w_in[:, :, N_MIX + 16:].astype(BF16))

    rope = _rope_tables(t_lat, x_sample.dtype)
    cache_k = cache_attn_k.reshape(b_lat, depth, past, N_HEADS * 2 * DK_A)
    cache_v = cache_attn_v.reshape(b_lat, depth, past, N_HEADS * HEAD_W)

    lanes16 = jnp.zeros((depth, LANES), F32)
    al_rows = lanes16.at[:, 8:16].set(gdn_a_log.reshape(depth, 8))
    dt_rows = lanes16.at[:, 8:16].set(gdn_dt_bias.reshape(depth, 8))
    dec = ret_decay.reshape(depth, 2, 2, 2)
    dec = jnp.transpose(dec, (0, 2, 1, 3))
    d256 = jnp.repeat(dec, HEAD_W, axis=-1)
    d128 = jnp.repeat(dec, DK_B, axis=-1)

    y_p = x_prompt.reshape(b_ctx * t_ctx, d)
    y_s = x_sample.reshape(b_lat * t_lat, d)
    tm_ctx, tm_lat = 256, 256
    per_lat = t_lat // tm_lat
    ks, vs, rs, gs = [], [], [], []
    for l in range(depth):
        wts = {
            "norm_w": norm_w[l].reshape(1, d),
            "w_in": w_pad[l],
            "mg_off": mg_off,
            "wq_row": jnp.tile(qk_norm_w[l, 0], 2).reshape(1, LANES),
            "wk_row": jnp.tile(qk_norm_w[l, 1], 2).reshape(1, LANES),
            "diff_lambda": diff_lambda[l],
            "subln_row": subln_w[l].reshape(1, LANES),
            "d256": d256[l], "d128": d128[l],
            "ret_norm_row": ret_norm_w[l].reshape(1, LANES),
            "conv_w": conv_w[l],
            "al_row": al_rows[l].reshape(1, LANES),
            "dt_row": dt_rows[l].reshape(1, LANES),
            "gdn_norm_row": gdn_norm_w[l].reshape(1, LANES),
            "w_branch": w_branch[l].astype(BF16),
            "w_out": w_out[l].astype(BF16),
        }
        mod = mods[l].reshape(8, 1, 3 * d)
        y_p, (k_l, v_l, r_l, g_l) = _layer(y_p, mod, lambda i: 4, l, wts, None, None,
                                           b_ctx, t_ctx, tm_ctx)
        y_s, _ = _layer(y_s, mod, lambda i: i // per_lat, l, wts, rope,
                        (cache_k, cache_v, state_ret, state_gdn), b_lat, t_lat, tm_lat)
        ks.append(k_l)
        vs.append(v_l)
        rs.append(r_l)
        gs.append(g_l)
    new_k = jnp.stack(ks, axis=1).reshape(b_ctx, depth, t_ctx, N_HEADS, 2, DK_A)
    new_v = jnp.stack(vs, axis=1).reshape(b_ctx, depth, t_ctx, N_HEADS, HEAD_W)
    return (y_p.reshape(b_ctx, t_ctx, d), y_s.reshape(b_lat, t_lat, d), new_k, new_v,
            jnp.stack(rs, axis=1), jnp.stack(gs, axis=1))
```

```python
import functools
import math

import jax
import jax.numpy as jnp
from jax import lax
from jax.experimental import pallas as pl
from jax.experimental.pallas import tpu as pltpu

F32 = jnp.float32
BF16 = jnp.bfloat16
HIGHEST = lax.Precision.HIGHEST

N_HEADS = 4
DK_A = 64
DK_B = 64
HEAD_W = 128
CONV_K = 5
GRID_W = 64
ROPE_BASE = 10000.0
EPS = 1e-6
LANES = 128
CHUNK = 128
ATTN_SUB = 256
LOG2E = 1.4426950408889634
GDN_GROUP = 4
VMEM_LIMIT = 56 * 1024 * 1024

OFF_AQ, OFF_AK, OFF_AV, OFF_AZ = 0, 512, 1024, 1536
OFF_BQ, OFF_BK, OFF_BV, OFF_BZ = 2048, 2304, 2560, 3072
OFF_CQ, OFF_CK, OFF_CV, OFF_CZ = 3584, 4096, 4608, 5120
OFF_BA = 5632
N_MIX = 5632


def _cparams(sem):
    return pltpu.CompilerParams(dimension_semantics=sem, vmem_limit_bytes=VMEM_LIMIT)


def _sigmoid(x):
    return 1.0 / (1.0 + jnp.exp(-x))


def _silu(x):
    return x * _sigmoid(x)


def _softplus(x):
    return jnp.maximum(x, 0.0) + jnp.log1p(jnp.exp(-jnp.abs(x)))


def _dot(a, b):
    return jnp.dot(a, b, preferred_element_type=F32)


def _dot_nt(a, b):
    return lax.dot_general(a, b, (((1,), (1,)), ((), ())), preferred_element_type=F32)


def _lane_iota(shape):
    return lax.broadcasted_iota(jnp.int32, shape, len(shape) - 1)


def _row_iota(shape):
    return lax.broadcasted_iota(jnp.int32, shape, len(shape) - 2)


def _norm_halves(x):
    lo = _lane_iota(x.shape) < 64
    x2 = x * x
    s0 = jnp.sum(jnp.where(lo, x2, 0.0), axis=-1, keepdims=True)
    s1 = jnp.sum(jnp.where(lo, 0.0, x2), axis=-1, keepdims=True)
    return x * lax.rsqrt(jnp.where(lo, s0, s1) * (1.0 / 64.0) + EPS)


def _rope(x, cos, sin_a, sin_b):
    return (x * cos + pltpu.roll(x, LANES - 16, 1) * sin_a + pltpu.roll(x, 16, 1) * sin_b)


def _ada_kernel(cond_ref, w_ref, b_ref, o_ref):
    c = cond_ref[...]
    o_ref[0] = jnp.dot(_silu(c), w_ref[0], precision=HIGHEST,
                       preferred_element_type=F32) + b_ref[0]


def _ada_call(cond, w_ada, b_ada):
    depth, d, d3 = w_ada.shape
    tn = 1024
    return pl.pallas_call(
        _ada_kernel,
        grid=(depth, d3 // tn),
        in_specs=[pl.BlockSpec((8, d), lambda l, j: (0, 0)),
                  pl.BlockSpec((1, d, tn), lambda l, j: (l, 0, j)),
                  pl.BlockSpec((1, 1, tn), lambda l, j: (l, 0, j))],
        out_specs=pl.BlockSpec((1, 8, tn), lambda l, j: (l, 0, j)),
        out_shape=jax.ShapeDtypeStruct((depth, 8, d3), F32),
        compiler_params=_cparams(("parallel", "parallel")),
    )(cond, w_ada, b_ada.reshape(depth, 1, d3))


def _proj_kernel(x_ref, mod_ref, nw_ref, w_ref, o_ref, h_ref, *, d):
    @pl.when(pl.program_id(1) == 0)
    def _():
        x = x_ref[...]
        y = x * lax.rsqrt(jnp.mean(x * x, axis=-1, keepdims=True) + EPS) * nw_ref[...]
        mod = mod_ref[0]
        h_ref[...] = (y * (1.0 + mod[:, d:2 * d]) + mod[:, 0:d]).astype(BF16)

    o_ref[...] = _dot(h_ref[...], w_ref[...])


def _proj_call(x2, mod, mod_row, norm_w, w_pad, tm):
    n, d = x2.shape
    n_cols = w_pad.shape[1]
    tn = 1536
    return pl.pallas_call(
        functools.partial(_proj_kernel, d=d),
        grid=(n // tm, n_cols // tn),
        in_specs=[pl.BlockSpec((tm, d), lambda i, j: (i, 0)),
                  pl.BlockSpec((1, 1, 3 * d), lambda i, j: (mod_row(i), 0, 0)),
                  pl.BlockSpec((1, d), lambda i, j: (0, 0)),
                  pl.BlockSpec((d, tn), lambda i, j: (0, j))],
        out_specs=pl.BlockSpec((tm, tn), lambda i, j: (i, j)),
        out_shape=jax.ShapeDtypeStruct((n, n_cols), F32),
        scratch_shapes=[pltpu.VMEM((tm, d), BF16)],
        compiler_params=_cparams(("parallel", "arbitrary")),
    )(x2, mod, norm_w, w_pad)


def _prep_ctx_kernel(ak_ref, av_ref, bq_ref, bk_ref, wk_ref,
                     kall_ref, vall_ref, rq_ref, rk_ref, ka_ref, va_ref):
    wk = wk_ref[...]
    for h in range(N_HEADS):
        sl = slice(h * HEAD_W, (h + 1) * HEAD_W)
        kn = _norm_halves(ak_ref[:, sl]) * wk
        ka_ref[0, :, sl] = kn
        kall_ref[0, h] = kn.astype(BF16)
        v = av_ref[:, sl]
        va_ref[0, :, sl] = v
        vall_ref[0, h] = v.astype(BF16)
    rq_ref[...] = bq_ref[...]
    rk_ref[...] = bk_ref[...] * (DK_B ** -0.5)


def _prep_lat_kernel(ak_ref, av_ref, bq_ref, bk_ref, wk_ref, cos_ref, sa_ref, sb_ref,
                     ck_ref, cv_ref, kall_ref, vall_ref, rq_ref, rk_ref, *, nt):
    t = pl.program_id(1)

    @pl.when(t < nt)
    def _():
        wk = wk_ref[...]
        cos, sa, sb = cos_ref[...], sa_ref[...], sb_ref[...]
        for h in range(N_HEADS):
            sl = slice(h * HEAD_W, (h + 1) * HEAD_W)
            kn = _norm_halves(ak_ref[:, sl]) * wk
            kall_ref[0, h] = _rope(kn, cos, sa, sb).astype(BF16)
            vall_ref[0, h] = av_ref[:, sl].astype(BF16)
        for e in range(2):
            sl = slice(e * LANES, (e + 1) * LANES)
            rq_ref[:, sl] = _rope(bq_ref[:, sl], cos, sa, sb)
            rk_ref[:, sl] = _rope(bk_ref[:, sl], cos, sa, sb) * (DK_B ** -0.5)

    @pl.when(t == nt)
    def _():
        for h in range(N_HEADS):
            sl = slice(h * HEAD_W, (h + 1) * HEAD_W)
            kall_ref[0, h] = ck_ref[:, sl].astype(BF16)
            vall_ref[0, h] = cv_ref[:, sl].astype(BF16)


def _prep_call(p, b, t, wk_row, rope, cache, layer):
    tk = 256
    nt = t // tk
    n = b * t
    if rope is None:
        return pl.pallas_call(
            _prep_ctx_kernel,
            grid=(b, nt),
            in_specs=[pl.BlockSpec((tk, 512), lambda bi, ti: (bi * nt + ti, OFF_AK // 512)),
                      pl.BlockSpec((tk, 512), lambda bi, ti: (bi * nt + ti, OFF_AV // 512)),
                      pl.BlockSpec((tk, 256), lambda bi, ti: (bi * nt + ti, OFF_BQ // 256)),
                      pl.BlockSpec((tk, 256), lambda bi, ti: (bi * nt + ti, OFF_BK // 256)),
                      pl.BlockSpec((1, LANES), lambda bi, ti: (0, 0))],
            out_specs=[pl.BlockSpec((1, N_HEADS, tk, HEAD_W), lambda bi, ti: (bi, 0, ti, 0)),
                       pl.BlockSpec((1, N_HEADS, tk, HEAD_W), lambda bi, ti: (bi, 0, ti, 0)),
                       pl.BlockSpec((tk, 256), lambda bi, ti: (bi * nt + ti, 0)),
                       pl.BlockSpec((tk, 256), lambda bi, ti: (bi * nt + ti, 0)),
                       pl.BlockSpec((1, tk, 512), lambda bi, ti: (bi, ti, 0)),
                       pl.BlockSpec((1, tk, 512), lambda bi, ti: (bi, ti, 0))],
            out_shape=[jax.ShapeDtypeStruct((b, N_HEADS, t, HEAD_W), BF16),
                       jax.ShapeDtypeStruct((b, N_HEADS, t, HEAD_W), BF16),
                       jax.ShapeDtypeStruct((n, 256), F32),
                       jax.ShapeDtypeStruct((n, 256), F32),
                       jax.ShapeDtypeStruct((b, t, 512), F32),
                       jax.ShapeDtypeStruct((b, t, 512), F32)],
            compiler_params=_cparams(("parallel", "parallel")),
        )(p, p, p, p, wk_row)
    cos, sa, sb = rope
    cache_k, cache_v = cache
    past = cache_k.shape[2]
    assert past == tk
    s = t + past

    def tok(bi, ti):
        return bi * nt + jnp.minimum(ti, nt - 1)

    return pl.pallas_call(
        functools.partial(_prep_lat_kernel, nt=nt),
        grid=(b, nt + 1),
        in_specs=[pl.BlockSpec((tk, 512), lambda bi, ti: (tok(bi, ti), OFF_AK // 512)),
                  pl.BlockSpec((tk, 512), lambda bi, ti: (tok(bi, ti), OFF_AV // 512)),
                  pl.BlockSpec((tk, 256), lambda bi, ti: (tok(bi, ti), OFF_BQ // 256)),
                  pl.BlockSpec((tk, 256), lambda bi, ti: (tok(bi, ti), OFF_BK // 256)),
                  pl.BlockSpec((1, LANES), lambda bi, ti: (0, 0)),
                  pl.BlockSpec((tk, LANES), lambda bi, ti: (jnp.minimum(ti, nt - 1), 0)),
                  pl.BlockSpec((tk, LANES), lambda bi, ti: (jnp.minimum(ti, nt - 1), 0)),
                  pl.BlockSpec((tk, LANES), lambda bi, ti: (jnp.minimum(ti, nt - 1), 0)),
                  pl.BlockSpec((None, None, past, 512), lambda bi, ti: (bi, layer, 0, 0)),
                  pl.BlockSpec((None, None, past, 512), lambda bi, ti: (bi, layer, 0, 0))],
        out_specs=[pl.BlockSpec((1, N_HEADS, tk, HEAD_W), lambda bi, ti: (bi, 0, ti, 0)),
                   pl.BlockSpec((1, N_HEADS, tk, HEAD_W), lambda bi, ti: (bi, 0, ti, 0)),
                   pl.BlockSpec((tk, 256), lambda bi, ti: (tok(bi, ti), 0)),
                   pl.BlockSpec((tk, 256), lambda bi, ti: (tok(bi, ti), 0))],
        out_shape=[jax.ShapeDtypeStruct((b, N_HEADS, s, HEAD_W), BF16),
                   jax.ShapeDtypeStruct((b, N_HEADS, s, HEAD_W), BF16),
                   jax.ShapeDtypeStruct((n, 256), F32),
                   jax.ShapeDtypeStruct((n, 256), F32)],
        compiler_params=_cparams(("parallel", "arbitrary")),
    )(p, p, p, p, wk_row, cos, sa, sb, cache_k, cache_v)


def _attn_kernel(*refs, lam_init, rope):
    if rope:
        (q_ref, z_ref, k_ref, v_ref, wq_ref, dl_ref, sw_ref, cos_ref, sa_ref, sb_ref,
         o_ref, s_ref, p_ref) = refs
    else:
        q_ref, z_ref, k_ref, v_ref, wq_ref, dl_ref, sw_ref, o_ref, s_ref, p_ref = refs
    n_keys = k_ref.shape[2]
    tq = q_ref.shape[0]
    keys = k_ref[0, 0]
    vals = v_ref[0, 0]
    lv = dl_ref[...]
    lam = (jnp.exp(jnp.sum(lv[0:1] * lv[1:2], axis=-1, keepdims=True))
           - jnp.exp(jnp.sum(lv[2:3] * lv[3:4], axis=-1, keepdims=True)) + lam_init)
    for r0 in range(0, tq, ATTN_SUB):
        rows = slice(r0, r0 + ATTN_SUB)
        qn = _norm_halves(q_ref[rows, :]) * wq_ref[...]
        if rope:
            qn = _rope(qn, cos_ref[rows, :], sa_ref[rows, :], sb_ref[rows, :])
        qn = qn * (DK_A ** -0.5 * LOG2E)
        lo = _lane_iota(qn.shape) < DK_A
        acc = []
        den = []
        for m in range(2):
            qm = jnp.where(lo if m == 0 else jnp.logical_not(lo), qn, 0.0).astype(BF16)
            s_ref[m, rows, :] = _dot_nt(qm, keys)
            mx = s_ref[m, rows, 0:LANES]
            for j in range(1, n_keys // LANES):
                mx = jnp.maximum(mx, s_ref[m, rows, j * LANES:(j + 1) * LANES])
            mx = jnp.max(mx, axis=-1, keepdims=True)
            l = None
            for j in range(n_keys // LANES):
                sl = slice(j * LANES, (j + 1) * LANES)
                pj = jnp.exp2(s_ref[m, rows, sl] - mx)
                l = pj if l is None else l + pj
                p_ref[m, rows, sl] = pj.astype(BF16)
            den.append(jnp.sum(l, axis=-1, keepdims=True))
            acc.append(_dot(p_ref[m, rows, :], vals))
        o = acc[0] * (1.0 / den[0]) - lam * (acc[1] * (1.0 / den[1]))
        o = o * lax.rsqrt(jnp.mean(o * o, axis=-1, keepdims=True) + EPS) * sw_ref[...]
        o_ref[rows, :] = o * (1.0 - lam_init) * _silu(z_ref[rows, :])


def _attn_call(p, kall, vall, wq_row, dlam, subln_row, rope, b, t, lam_init):
    tq = min(512, t)
    s = kall.shape[2]
    nq = t // tq
    in_specs = [pl.BlockSpec((tq, HEAD_W), lambda bi, h, i: (bi * nq + i, OFF_AQ // HEAD_W + h)),
                pl.BlockSpec((tq, HEAD_W), lambda bi, h, i: (bi * nq + i, OFF_AZ // HEAD_W + h)),
                pl.BlockSpec((1, 1, s, HEAD_W), lambda bi, h, i: (bi, h, 0, 0)),
                pl.BlockSpec((1, 1, s, HEAD_W), lambda bi, h, i: (bi, h, 0, 0)),
                pl.BlockSpec((1, LANES), lambda bi, h, i: (0, 0)),
                pl.BlockSpec((4, DK_A), lambda bi, h, i: (0, 0)),
                pl.BlockSpec((1, LANES), lambda bi, h, i: (0, 0))]
    args = [p, p, kall, vall, wq_row, dlam, subln_row]
    if rope is not None:
        in_specs += [pl.BlockSpec((tq, LANES), lambda bi, h, i: (i, 0))] * 3
        args += list(rope)
    return pl.pallas_call(
        functools.partial(_attn_kernel, lam_init=lam_init, rope=rope is not None),
        grid=(b, N_HEADS, nq),
        in_specs=in_specs,
        out_specs=pl.BlockSpec((tq, HEAD_W), lambda bi, h, i: (bi * nq + i, h)),
        out_shape=jax.ShapeDtypeStruct((b * t, N_HEADS * HEAD_W), F32),
        scratch_shapes=[pltpu.VMEM((2, tq, s), F32), pltpu.VMEM((2, tq, s), BF16)],
        compiler_params=_cparams(("parallel", "parallel", "arbitrary")),
    )(*args)


def _ret_kernel(*refs, nchunk, has_state):
    if has_state:
        (q_ref, k_ref, v_ref, z_ref, d256_ref, d128_ref, nw_ref, s0_ref,
         o_ref, s_ref, dm_ref, qd_ref, kd_ref) = refs
        st_ref = None
    else:
        (q_ref, k_ref, v_ref, z_ref, d256_ref, d128_ref, nw_ref,
         o_ref, st_ref, s_ref, dm_ref, qd_ref, kd_ref) = refs
    c_len = CHUNK
    lg256 = -_softplus(-d256_ref[...])
    lg128 = -_softplus(-d128_ref[...])
    ri = _row_iota((c_len, c_len)).astype(F32)
    ci = _lane_iota((c_len, c_len)).astype(F32)
    rcol256 = _row_iota((c_len, 2 * HEAD_W)).astype(F32)
    rcol128 = _row_iota((c_len, LANES)).astype(F32)
    blockmask = ((_row_iota((2 * DK_B, 2 * HEAD_W)) < DK_B)
                 == (_lane_iota((2 * DK_B, 2 * HEAD_W)) < HEAD_W))
    for d in range(2):
        rel = (ri - ci) if d == 0 else (ci - ri)
        keep = rel >= 0
        for e in range(2):
            lg = lg256[d:d + 1, e * HEAD_W:e * HEAD_W + 1]
            dm_ref[d, e] = jnp.where(keep, jnp.exp(jnp.where(keep, rel, 0.0) * lg), 0.0)
        qpow = (rcol256 + 1.0) if d == 0 else (c_len - rcol256)
        qd_ref[d] = jnp.exp(qpow * lg256[d:d + 1])
        kpow = (c_len - 1.0 - rcol128) if d == 0 else rcol128
        kd_ref[d] = jnp.exp(kpow * lg128[d:d + 1])
        if has_state:
            s_ref[d] = jnp.zeros((2 * DK_B, 2 * HEAD_W), F32)
            for e in range(2):
                s_ref[d, e * DK_B:(e + 1) * DK_B, e * HEAD_W:(e + 1) * HEAD_W] = s0_ref[d, e]
        else:
            s_ref[d] = jnp.zeros((2 * DK_B, 2 * HEAD_W), F32)
    cdec = [jnp.exp(float(c_len) * lg256[d:d + 1]) for d in range(2)]
    lane_lo = _lane_iota((c_len, LANES)) < DK_B
    nw = nw_ref[...]

    def step(c, d, final):
        t0 = pl.multiple_of(c * c_len, c_len)
        q = q_ref[pl.ds(t0, c_len), :]
        k = k_ref[pl.ds(t0, c_len), :]
        v = v_ref[pl.ds(t0, c_len), :]
        st = s_ref[d]
        inter = _dot(q.astype(BF16), st.astype(BF16)) * qd_ref[d]
        kb = k.astype(BF16)
        vb = v.astype(BF16)
        for e in range(2):
            sl = slice(e * HEAD_W, (e + 1) * HEAD_W)
            qe = jnp.where(lane_lo if e == 0 else jnp.logical_not(lane_lo), q, 0.0).astype(BF16)
            sc = _dot_nt(qe, kb) * dm_ref[d, e]
            o_e = _dot(sc.astype(BF16), vb[:, sl]) + inter[:, sl]
            if final:
                o_e = o_e + o_ref[pl.ds(t0, c_len), sl]
                o_e = o_e * lax.rsqrt(jnp.mean(o_e * o_e, axis=-1, keepdims=True) + EPS) * nw
                o_e = o_e * _silu(z_ref[pl.ds(t0, c_len), sl])
            o_ref[pl.ds(t0, c_len), sl] = o_e
        kdk = (k * kd_ref[d]).astype(BF16)
        kv = lax.dot_general(kdk, vb, (((0,), (0,)), ((), ())), preferred_element_type=F32)
        s_ref[d] = jnp.where(blockmask, st * cdec[d] + kv, 0.0)

    def fwd(c, carry):
        step(c, 0, False)
        return carry

    def bwd(i, carry):
        step(nchunk - 1 - i, 1, True)
        return carry

    lax.fori_loop(0, nchunk, fwd, 0)
    lax.fori_loop(0, nchunk, bwd, 0)
    if not has_state:
        for d in range(2):
            for e in range(2):
                st_ref[d, e] = s_ref[d, e * DK_B:(e + 1) * DK_B, e * HEAD_W:(e + 1) * HEAD_W]


def _ret_call(p, rq, rk, d256, d128, nw_row, state, b, t, layer):
    nchunk = t // CHUNK
    hp = N_HEADS // 2
    in_specs = [pl.BlockSpec((t, LANES), lambda bi, h: (bi, h)),
                pl.BlockSpec((t, LANES), lambda bi, h: (bi, h)),
                pl.BlockSpec((t, 256), lambda bi, h: (bi, OFF_BV // 256 + h)),
                pl.BlockSpec((t, 256), lambda bi, h: (bi, OFF_BZ // 256 + h)),
                pl.BlockSpec((None, 2, 256), lambda bi, h: (h, 0, 0)),
                pl.BlockSpec((None, 2, LANES), lambda bi, h: (h, 0, 0)),
                pl.BlockSpec((1, LANES), lambda bi, h: (0, 0))]
    args = [rq, rk, p, p, d256, d128, nw_row]
    scratch = [pltpu.VMEM((2, 2 * DK_B, 2 * HEAD_W), F32),
               pltpu.VMEM((2, 2, CHUNK, CHUNK), F32),
               pltpu.VMEM((2, CHUNK, 2 * HEAD_W), F32),
               pltpu.VMEM((2, CHUNK, LANES), F32)]
    o_spec = pl.BlockSpec((t, 256), lambda bi, h: (bi, h))
    o_shape = jax.ShapeDtypeStruct((b * t, N_HEADS * HEAD_W), F32)
    if state is not None:
        in_specs.append(pl.BlockSpec((None, None, 2, 2, DK_B, HEAD_W),
                                     lambda bi, h: (bi, layer, 0, h, 0, 0)))
        args.append(state)
        out_specs, out_shape = o_spec, o_shape
    else:
        out_specs = [o_spec, pl.BlockSpec((None, 2, 2, DK_B, HEAD_W), lambda bi, h: (bi, 0, h, 0, 0))]
        out_shape = [o_shape, jax.ShapeDtypeStruct((b, 2, N_HEADS, DK_B, HEAD_W), F32)]
    return pl.pallas_call(
        functools.partial(_ret_kernel, nchunk=nchunk, has_state=state is not None),
        grid=(b, hp),
        in_specs=in_specs,
        out_specs=out_specs,
        out_shape=out_shape,
        scratch_shapes=scratch,
        compiler_params=_cparams(("parallel", "parallel")),
    )(*args)


def _gdn_kernel(*refs, t, has_state):
    if has_state:
        (cq_ref, ck_ref, cv_ref, cz_ref, ba_ref, wq_ref, wk_ref, wv_ref, al_ref, dt_ref, nw_ref,
         s0_ref, o_ref, qs_ref, ks_ref, vs_ref, u_ref, w_ref, qk_ref, qg_ref, kdt_ref, et_ref,
         ob_ref, s_ref) = refs
        st_ref = None
    else:
        (cq_ref, ck_ref, cv_ref, cz_ref, ba_ref, wq_ref, wk_ref, wv_ref, al_ref, dt_ref, nw_ref,
         o_ref, st_ref, qs_ref, ks_ref, vs_ref, u_ref, w_ref, qk_ref, qg_ref, kdt_ref, et_ref,
         ob_ref, s_ref) = refs
    c_len = CHUNK
    nchunk = t // c_len
    h = pl.program_id(1)
    pad = 8
    win = 256
    nwin = t // win

    def conv_window(r0, first, last):
        lo = 0 if first else pad
        hi = 0 if last else pad
        start = r0 - lo if isinstance(r0, int) else pl.multiple_of(r0 - lo, pad)
        for a, (src, cw_ref, dst) in enumerate(((cq_ref, wq_ref, qs_ref), (ck_ref, wk_ref, ks_ref),
                                                (cv_ref, wv_ref, vs_ref))):
            xw = src[pl.ds(start, win + lo + hi), :]
            if first:
                xw = jnp.concatenate([jnp.zeros((pad, LANES), F32), xw], axis=0)
            if last:
                xw = jnp.concatenate([xw, jnp.zeros((pad, LANES), F32)], axis=0)
            cw = cw_ref[...]
            acc = None
            for j in range(CONV_K):
                off = pad - CONV_K // 2 + j
                term = xw[off:off + win, :] * cw[j:j + 1, :]
                acc = term if acc is None else acc + term
            y = _silu(acc)
            if a < 2:
                y = y * lax.rsqrt(jnp.sum(y * y, axis=-1, keepdims=True) + EPS)
            if a == 0:
                y = y * (HEAD_W ** -0.5)
            dst[pl.ds(r0, win), :] = y

    if nwin == 1:
        conv_window(0, True, True)
    else:
        conv_window(0, True, False)

        def conv_body(i, carry):
            conv_window(pl.multiple_of(i * win, win), False, False)
            return carry

        lax.fori_loop(1, nwin - 1, conv_body, 0)
        conv_window((nwin - 1) * win, False, True)

    ri = _row_iota((c_len, c_len))
    ci = _lane_iota((c_len, c_len))
    lane = _lane_iota((c_len, LANES))
    alog = al_ref[...]
    dtb = dt_ref[...]
    tri16 = jnp.where(ri >= ci, 1.0, 0.0).astype(BF16)
    g_lanes = jnp.logical_and(lane >= 2 * N_HEADS, lane < 4 * N_HEADS)

    def lanes3(tgt):
        return jnp.logical_or(lane == tgt, jnp.logical_or(lane == tgt + 16, lane == tgt + 32))

    def chunk_chains(c):
        t0 = pl.multiple_of(c * c_len, c_len)
        ba = ba_ref[pl.ds(t0, c_len), :]
        beta_all = _sigmoid(ba)
        g_all = jnp.where(g_lanes, -jnp.exp(alog) * _softplus(ba + dtb), 0.0)
        g_hi = g_all.astype(BF16).astype(F32)
        r1 = g_all - g_hi
        g_mid = r1.astype(BF16).astype(F32)
        g_lo = (r1 - g_mid).astype(BF16).astype(F32)
        parts = (g_hi + pltpu.roll(g_mid, 16, 1) + pltpu.roll(g_lo, 32, 1)).astype(BF16)
        pre = _dot(tri16, parts)
        tgt_f = 2 * N_HEADS + h
        tgt_b = 3 * N_HEADS + h
        gc_f = jnp.sum(jnp.where(lanes3(tgt_f), pre, 0.0), axis=-1, keepdims=True)
        pre_b = jnp.sum(jnp.where(lanes3(tgt_b), pre, 0.0), axis=-1, keepdims=True)
        g_b = jnp.sum(jnp.where(lane == tgt_b, g_all, 0.0), axis=-1, keepdims=True)
        gc_b = pre_b[c_len - 1:c_len, :] - pre_b + g_b
        beta_f = jnp.sum(jnp.where(lane == h, beta_all, 0.0), axis=-1, keepdims=True)
        beta_b = jnp.sum(jnp.where(lane == N_HEADS + h, beta_all, 0.0), axis=-1, keepdims=True)
        rows = jnp.where(lane == 0, gc_f, jnp.where(lane == 1, gc_b, 0.0)).T

        q = qs_ref[pl.ds(t0, c_len), :]
        k = ks_ref[pl.ds(t0, c_len), :]
        v = vs_ref[pl.ds(t0, c_len), :]
        k16 = k.astype(BF16)
        kkqk = _dot_nt(jnp.concatenate([k16, q.astype(BF16)], axis=0), k16)
        kk = kkqk[0:c_len, :]
        qk_raw = kkqk[c_len:2 * c_len, :]
        chains = []
        for d, (gc, beta) in enumerate(((gc_f, beta_f), (gc_b, beta_b))):
            gr = rows[d:d + 1, :]
            incl = (ri >= ci) if d == 0 else (ci >= ri)
            strict = (ri > ci) if d == 0 else (ci > ri)
            ex = jnp.exp(jnp.where(incl, gc - gr, 0.0))
            a_mat = jnp.where(strict, ex, 0.0) * kk * beta
            egc = jnp.exp(gc)
            rhs = jnp.concatenate([v * beta, k * (beta * egc)], axis=1)
            qk_ref[d, pl.ds(t0, c_len), :] = (jnp.where(incl, ex, 0.0) * qk_raw).astype(BF16)
            qg_ref[d, pl.ds(t0, c_len), :] = (q * egc).astype(BF16)
            g_tot = gc[c_len - 1:c_len, :] if d == 0 else gc[0:1, :]
            kdt_ref[d, pl.ds(t0, c_len), :] = (k * jnp.exp(g_tot - gc)).T.astype(BF16)
            et_ref[d, c] = jnp.broadcast_to(jnp.exp(g_tot), (8, LANES))
            chains.append((a_mat, rhs, (d, t0)))
        return chains

    group = min(GDN_GROUP, nchunk)

    def prep_body(i, carry):
        chains = []
        for j in range(group):
            chains += chunk_chains(i * group + j)
        n_mats = [-jnp.where((ri >> 1) == (ci >> 1), a_mat, 0.0) for a_mat, _, _ in chains]
        sh = 1
        while (1 << sh) < c_len:
            off = jnp.logical_and((ri >> (sh + 1)) == (ci >> (sh + 1)), (ri >> sh) != (ci >> sh))
            l_mats = [jnp.where(off, a_mat, 0.0) for a_mat, _, _ in chains]
            x_mats = [l + _dot(l.astype(BF16), n.astype(BF16)) for l, n in zip(l_mats, n_mats)]
            n_mats = [n - x - _dot(n.astype(BF16), x.astype(BF16)) for n, x in zip(n_mats, x_mats)]
            sh += 1
        for n_mat, (_, rhs, (d, t0)) in zip(n_mats, chains):
            uw = rhs + _dot(n_mat.astype(BF16), rhs.astype(BF16))
            u_ref[d, pl.ds(t0, c_len), :] = uw[:, 0:HEAD_W]
            w_ref[d, pl.ds(t0, c_len), :] = uw[:, HEAD_W:2 * HEAD_W].astype(BF16)
        return carry

    lax.fori_loop(0, nchunk // group, prep_body, 0)

    for d in range(2):
        s_ref[d] = s0_ref[d] if has_state else jnp.zeros((HEAD_W, HEAD_W), F32)

    def scan_body(i, carry):
        for d, c in ((0, i), (1, nchunk - 1 - i)):
            t0 = pl.multiple_of(c * c_len, c_len)
            st = s_ref[d]
            st16 = st.astype(BF16)
            v_new = u_ref[d, pl.ds(t0, c_len), :] - _dot(w_ref[d, pl.ds(t0, c_len), :], st16)
            vn16 = v_new.astype(BF16)
            o = (_dot(qg_ref[d, pl.ds(t0, c_len), :], st16)
                 + _dot(qk_ref[d, pl.ds(t0, c_len), :], vn16))
            s_ref[d] = st * et_ref[d, c][0:1, :] + _dot(kdt_ref[d, pl.ds(t0, c_len), :], vn16)
            if d == 0:
                o_ref[pl.ds(t0, c_len), :] = o
            else:
                ob_ref[pl.ds(t0, c_len), :] = o
        return carry

    lax.fori_loop(0, nchunk, scan_body, 0)

    def fin_body(i, carry):
        r0 = pl.multiple_of(i * win, win)
        o = o_ref[pl.ds(r0, win), :] + ob_ref[pl.ds(r0, win), :]
        o = o * lax.rsqrt(jnp.mean(o * o, axis=-1, keepdims=True) + EPS) * nw_ref[...]
        o_ref[pl.ds(r0, win), :] = o * _silu(cz_ref[pl.ds(r0, win), :])
        return carry

    lax.fori_loop(0, nwin, fin_body, 0)
    if not has_state:
        for d in range(2):
            st_ref[d] = s_ref[d]


def _gdn_call(p, conv_w, al_row, dt_row, nw_row, state, b, t, layer):
    def col(off):
        return lambda bi, h: (bi, off // LANES + h)

    in_specs = [pl.BlockSpec((t, LANES), col(OFF_CQ)),
                pl.BlockSpec((t, LANES), col(OFF_CK)),
                pl.BlockSpec((t, LANES), col(OFF_CV)),
                pl.BlockSpec((t, LANES), col(OFF_CZ)),
                pl.BlockSpec((t, LANES), lambda bi, h: (bi, OFF_BA // LANES)),
                pl.BlockSpec((CONV_K, LANES), lambda bi, h: (0, h)),
                pl.BlockSpec((CONV_K, LANES), lambda bi, h: (0, N_HEADS + h)),
                pl.BlockSpec((CONV_K, LANES), lambda bi, h: (0, 2 * N_HEADS + h)),
                pl.BlockSpec((1, LANES), lambda bi, h: (0, 0)),
                pl.BlockSpec((1, LANES), lambda bi, h: (0, 0)),
                pl.BlockSpec((1, LANES), lambda bi, h: (0, 0))]
    args = [p, p, p, p, p, conv_w, conv_w, conv_w, al_row, dt_row, nw_row]
    scratch = [pltpu.VMEM((t, LANES), F32),
               pltpu.VMEM((t, LANES), F32),
               pltpu.VMEM((t, LANES), F32),
               pltpu.VMEM((2, t, LANES), F32),
               pltpu.VMEM((2, t, LANES), BF16),
               pltpu.VMEM((2, t, LANES), BF16),
               pltpu.VMEM((2, t, LANES), BF16),
               pltpu.VMEM((2, t, LANES), BF16),
               pltpu.VMEM((2, t // CHUNK, 8, LANES), F32),
               pltpu.VMEM((t, LANES), F32),
               pltpu.VMEM((2, HEAD_W, HEAD_W), F32)]
    o_spec = pl.BlockSpec((t, LANES), lambda bi, h: (bi, h))
    o_shape = jax.ShapeDtypeStruct((b * t, N_HEADS * HEAD_W), F32)
    if state is not None:
        in_specs.append(pl.BlockSpec((None, None, 2, None, HEAD_W, HEAD_W),
                                     lambda bi, h: (bi, layer, 0, h, 0, 0)))
        args.append(state)
        out_specs, out_shape = o_spec, o_shape
    else:
        out_specs = [o_spec, pl.BlockSpec((None, 2, None, HEAD_W, HEAD_W),
                                          lambda bi, h: (bi, 0, h, 0, 0))]
        out_shape = [o_shape, jax.ShapeDtypeStruct((b, 2, N_HEADS, HEAD_W, HEAD_W), F32)]
    return pl.pallas_call(
        functools.partial(_gdn_kernel, t=t, has_state=state is not None),
        grid=(b, N_HEADS),
        in_specs=in_specs,
        out_specs=out_specs,
        out_shape=out_shape,
        scratch_shapes=scratch,
        compiler_params=_cparams(("parallel", "parallel")),
    )(*args)


def _out_kernel(ya_ref, yb_ref, yc_ref, mg_ref, x_ref, mod_ref, wb_ref, wo_ref, o_ref, *, d):
    ya = _dot(ya_ref[...].astype(BF16), wb_ref[0])
    yb = _dot(yb_ref[...].astype(BF16), wb_ref[1])
    yc = _dot(yc_ref[...].astype(BF16), wb_ref[2])
    y = (_sigmoid(mg_ref[:, 0:d]) * ya + _sigmoid(mg_ref[:, d:2 * d]) * yb
         + _sigmoid(mg_ref[:, 2 * d:3 * d]) * yc)
    out = _dot(y.astype(BF16), wo_ref[...])
    gate = mod_ref[0][:, 2 * d:3 * d]
    o_ref[...] = x_ref[...] + gate * out


def _out_call(ya, yb, yc, p, x2, mod, mod_row, wb, wo, mg_block, tm):
    n, d = x2.shape
    w_br = ya.shape[1]
    return pl.pallas_call(
        functools.partial(_out_kernel, d=d),
        grid=(n // tm,),
        in_specs=[pl.BlockSpec((tm, w_br), lambda i: (i, 0)),
                  pl.BlockSpec((tm, w_br), lambda i: (i, 0)),
                  pl.BlockSpec((tm, w_br), lambda i: (i, 0)),
                  pl.BlockSpec((tm, 3 * d), lambda i: (i, mg_block)),
                  pl.BlockSpec((tm, d), lambda i: (i, 0)),
                  pl.BlockSpec((1, 1, 3 * d), lambda i: (mod_row(i), 0, 0)),
                  pl.BlockSpec((3, w_br, d), lambda i: (0, 0, 0)),
                  pl.BlockSpec((d, d), lambda i: (0, 0))],
        out_specs=pl.BlockSpec((tm, d), lambda i: (i, 0)),
        out_shape=jax.ShapeDtypeStruct((n, d), F32),
        compiler_params=_cparams(("parallel",)),
    )(ya, yb, yc, p, x2, mod, wb, wo)


def _rope_tables(n_tokens, dtype):
    n_rows = n_tokens // GRID_W
    row = jnp.repeat(jnp.arange(n_rows, dtype=jnp.float32), GRID_W)
    col = jnp.tile(jnp.arange(GRID_W, dtype=jnp.float32), n_rows)
    n_freq = DK_A // 4
    inv = 1.0 / (ROPE_BASE ** (jnp.arange(n_freq, dtype=jnp.float32) / n_freq))
    ar = row[:, None] * inv
    ac = col[:, None] * inv
    ang = jnp.concatenate([ar, ar, ac, ac], axis=-1)
    cos = jnp.tile(jnp.cos(ang).astype(dtype), (1, 2))
    sin = jnp.tile(jnp.sin(ang).astype(dtype), (1, 2))
    first = (jnp.arange(LANES) % 32) < 16
    sin_a = jnp.where(first, -sin, 0.0)
    sin_b = jnp.where(first, 0.0, sin)
    return cos, sin_a, sin_b


PROJ_TM = 1024
OUT_TM = 256


def _layer(x2, mod, mod_row_for, layer, wts, rope, ctx, b, t):
    p = _proj_call(x2, mod, mod_row_for(PROJ_TM), wts["norm_w"], wts["w_in"], PROJ_TM)
    cache = None if ctx is None else (ctx[0], ctx[1])
    prep = _prep_call(p, b, t, wts["wk_row"], rope, cache, layer)
    kall, vall, rq, rk = prep[:4]
    lam_init = 0.8 - 0.6 * math.exp(-0.3 * layer)
    ya = _attn_call(p, kall, vall, wts["wq_row"], wts["diff_lambda"], wts["subln_row"], rope,
                    b, t, lam_init)
    ret = _ret_call(p, rq, rk, wts["d256"], wts["d128"], wts["ret_norm_row"],
                    None if ctx is None else ctx[2], b, t, layer)
    gdn = _gdn_call(p, wts["conv_w"], wts["al_row"], wts["dt_row"], wts["gdn_norm_row"],
                    None if ctx is None else ctx[3], b, t, layer)
    if ctx is None:
        yb, s_ret = ret
        yc, s_gdn = gdn
        extras = (prep[4], prep[5], s_ret, s_gdn)
    else:
        yb, yc = ret, gdn
        extras = None
    mg_block = wts["mg_off"] // (3 * x2.shape[1])
    x2 = _out_call(ya, yb, yc, p, x2, mod, mod_row_for(OUT_TM), wts["w_branch"], wts["w_out"],
                   mg_block, OUT_TM)
    return x2, extras


def kernel(x_prompt, x_sample, cache_attn_k, cache_attn_v, state_ret, state_gdn, c, c_ctx,
           norm_w, w_ada, b_ada, w_in, qk_norm_w, diff_lambda, subln_w, ret_decay, ret_norm_w,
           conv_w, gdn_a_log, gdn_dt_bias, gdn_norm_w, w_branch, w_out):
    b_ctx, t_ctx, d = x_prompt.shape
    b_lat, t_lat, _ = x_sample.shape
    depth = w_in.shape[0]
    past = cache_attn_k.shape[2]
    assert b_lat <= 4 and d % LANES == 0

    cond = jnp.zeros((8, d), F32).at[:b_lat].set(c).at[4].set(c_ctx)
    mods = _ada_call(cond, w_ada, b_ada)

    mg_off = -(-(OFF_BA + LANES) // (3 * d)) * (3 * d)
    n_cols = mg_off + 3 * d
    n_cols = -(-n_cols // 1536) * 1536
    w_pad = jnp.zeros((depth, d, n_cols), BF16)
    w_pad = w_pad.at[:, :, :N_MIX + 16].set(w_in[:, :, :N_MIX + 16].astype(BF16))
    w_pad = w_pad.at[:, :, mg_off:mg_off + 3 * d].set(w_in[:, :, N_MIX + 16:].astype(BF16))

    rope = _rope_tables(t_lat, x_sample.dtype)
    cache_k = cache_attn_k.reshape(b_lat, depth, past, N_HEADS * 2 * DK_A)
    cache_v = cache_attn_v.reshape(b_lat, depth, past, N_HEADS * HEAD_W)

    lanes16 = jnp.zeros((depth, LANES), F32)
    al_rows = lanes16.at[:, 8:16].set(gdn_a_log.reshape(depth, 8))
    dt_rows = lanes16.at[:, 8:16].set(gdn_dt_bias.reshape(depth, 8))
    dec = ret_decay.reshape(depth, 2, 2, 2)
    dec = jnp.transpose(dec, (0, 2, 1, 3))
    d256 = jnp.repeat(dec, HEAD_W, axis=-1)
    d128 = jnp.repeat(dec, DK_B, axis=-1)

    y_p = x_prompt.reshape(b_ctx * t_ctx, d)
    y_s = x_sample.reshape(b_lat * t_lat, d)
    assert t_lat % PROJ_TM == 0 and (b_ctx * t_ctx) % PROJ_TM == 0

    def ctx_row(tm):
        return lambda i: 4

    def lat_row(tm):
        return lambda i: i // (t_lat // tm)

    ks, vs, rs, gs = [], [], [], []
    for l in range(depth):
        wts = {
            "norm_w": norm_w[l].reshape(1, d),
            "w_in": w_pad[l],
            "mg_off": mg_off,
            "wq_row": jnp.tile(qk_norm_w[l, 0], 2).reshape(1, LANES),
            "wk_row": jnp.tile(qk_norm_w[l, 1], 2).reshape(1, LANES),
            "diff_lambda": diff_lambda[l],
            "subln_row": subln_w[l].reshape(1, LANES),
            "d256": d256[l], "d128": d128[l],
            "ret_norm_row": ret_norm_w[l].reshape(1, LANES),
            "conv_w": conv_w[l],
            "al_row": al_rows[l].reshape(1, LANES),
            "dt_row": dt_rows[l].reshape(1, LANES),
            "gdn_norm_row": gdn_norm_w[l].reshape(1, LANES),
            "w_branch": w_branch[l].astype(BF16),
            "w_out": w_out[l].astype(BF16),
        }
        mod = mods[l].reshape(8, 1, 3 * d)
        y_p, (k_l, v_l, r_l, g_l) = _layer(y_p, mod, ctx_row, l, wts, None, None, b_ctx, t_ctx)
        y_s, _ = _layer(y_s, mod, lat_row, l, wts, rope,
                        (cache_k, cache_v, state_ret, state_gdn), b_lat, t_lat)
        ks.append(k_l)
        vs.append(v_l)
        rs.append(r_l)
        gs.append(g_l)
    new_k = jnp.stack(ks, axis=1).reshape(b_ctx, depth, t_ctx, N_HEADS, 2, DK_A)
    new_v = jnp.stack(vs, axis=1).reshape(b_ctx, depth, t_ctx, N_HEADS, HEAD_W)
    return (y_p.reshape(b_ctx, t_ctx, d), y_s.reshape(b_lat, t_lat, d), new_k, new_v,
            jnp.stack(rs, axis=1), jnp.stack(gs, axis=1))
```

```python
import functools
import math

import jax
import jax.numpy as jnp
from jax import lax
from jax.experimental import pallas as pl
from jax.experimental.pallas import tpu as pltpu

F32 = jnp.float32
BF16 = jnp.bfloat16
HIGHEST = lax.Precision.HIGHEST

N_HEADS = 4
DK_A = 64
DK_B = 64
HEAD_W = 128
CONV_K = 5
GRID_W = 64
ROPE_BASE = 10000.0
EPS = 1e-6
LANES = 128
CHUNK = 128
ATTN_SUB = 256
ATTN_KEYS = 256
LOG2E = 1.4426950408889634
GDN_GROUP = 4
VMEM_LIMIT = 56 * 1024 * 1024

OFF_AQ, OFF_AK, OFF_AV, OFF_AZ = 0, 512, 1024, 1536
OFF_BQ, OFF_BK, OFF_BV, OFF_BZ = 2048, 2304, 2560, 3072
OFF_CQ, OFF_CK, OFF_CV, OFF_CZ = 3584, 4096, 4608, 5120
OFF_BA = 5632
N_MIX = 5632


def _cparams(sem):
    return pltpu.CompilerParams(dimension_semantics=sem, vmem_limit_bytes=VMEM_LIMIT)


def _sigmoid(x):
    return 1.0 / (1.0 + jnp.exp(-x))


def _silu(x):
    return x * _sigmoid(x)


def _softplus(x):
    return jnp.maximum(x, 0.0) + jnp.log1p(jnp.exp(-jnp.abs(x)))


def _dot(a, b):
    return jnp.dot(a, b, preferred_element_type=F32)


def _dot_nt(a, b):
    return lax.dot_general(a, b, (((1,), (1,)), ((), ())), preferred_element_type=F32)


def _lane_iota(shape):
    return lax.broadcasted_iota(jnp.int32, shape, len(shape) - 1)


def _row_iota(shape):
    return lax.broadcasted_iota(jnp.int32, shape, len(shape) - 2)


def _norm_halves(x):
    lo = _lane_iota(x.shape) < 64
    x2 = x * x
    s0 = jnp.sum(jnp.where(lo, x2, 0.0), axis=-1, keepdims=True)
    s1 = jnp.sum(jnp.where(lo, 0.0, x2), axis=-1, keepdims=True)
    return x * lax.rsqrt(jnp.where(lo, s0, s1) * (1.0 / 64.0) + EPS)


def _ones_column(n):
    return jnp.where(_lane_iota((n, LANES)) == 0, 1.0, 0.0).astype(BF16)


def _rope(x, cos, sin_a, sin_b):
    return (x * cos + pltpu.roll(x, LANES - 16, 1) * sin_a + pltpu.roll(x, 16, 1) * sin_b)


def _ada_kernel(cond_ref, w_ref, b_ref, o_ref):
    c = cond_ref[...]
    o_ref[0] = jnp.dot(_silu(c), w_ref[0], precision=HIGHEST,
                       preferred_element_type=F32) + b_ref[0]


def _ada_call(cond, w_ada, b_ada):
    depth, d, d3 = w_ada.shape
    tn = 1024
    return pl.pallas_call(
        _ada_kernel,
        grid=(depth, d3 // tn),
        in_specs=[pl.BlockSpec((8, d), lambda l, j: (0, 0)),
                  pl.BlockSpec((1, d, tn), lambda l, j: (l, 0, j)),
                  pl.BlockSpec((1, 1, tn), lambda l, j: (l, 0, j))],
        out_specs=pl.BlockSpec((1, 8, tn), lambda l, j: (l, 0, j)),
        out_shape=jax.ShapeDtypeStruct((depth, 8, d3), F32),
        compiler_params=_cparams(("parallel", "parallel")),
    )(cond, w_ada, b_ada.reshape(depth, 1, d3))


def _proj_kernel(x_ref, mod_ref, nw_ref, w_ref, o_ref, ba_ref, h_ref, *, d, ba_tile, ba_off):
    j = pl.program_id(1)

    @pl.when(j == 0)
    def _():
        x = x_ref[...]
        y = x * lax.rsqrt(jnp.mean(x * x, axis=-1, keepdims=True) + EPS) * nw_ref[...]
        mod = mod_ref[0]
        h_ref[...] = (y * (1.0 + mod[:, d:2 * d]) + mod[:, 0:d]).astype(BF16)

    acc = _dot(h_ref[...], w_ref[...])
    o_ref[...] = acc.astype(BF16)

    @pl.when(j == ba_tile)
    def _():
        ba_ref[...] = acc[:, ba_off:ba_off + LANES]


def _proj_call(x2, mod, mod_row, norm_w, w_pad, tm):
    n, d = x2.shape
    n_cols = w_pad.shape[1]
    tn = PROJ_TN
    return pl.pallas_call(
        functools.partial(_proj_kernel, d=d, ba_tile=OFF_BA // tn, ba_off=OFF_BA % tn),
        grid=(n // tm, n_cols // tn),
        in_specs=[pl.BlockSpec((tm, d), lambda i, j: (i, 0)),
                  pl.BlockSpec((1, 1, 3 * d), lambda i, j: (mod_row(i), 0, 0)),
                  pl.BlockSpec((1, d), lambda i, j: (0, 0)),
                  pl.BlockSpec((d, tn), lambda i, j: (0, j))],
        out_specs=[pl.BlockSpec((tm, tn), lambda i, j: (i, j)),
                   pl.BlockSpec((tm, LANES), lambda i, j: (i, 0))],
        out_shape=[jax.ShapeDtypeStruct((n, n_cols), BF16),
                   jax.ShapeDtypeStruct((n, LANES), F32)],
        scratch_shapes=[pltpu.VMEM((tm, d), BF16)],
        compiler_params=_cparams(("parallel", "arbitrary")),
    )(x2, mod, norm_w, w_pad)


def _prep_ctx_kernel(ak_ref, av_ref, bq_ref, bk_ref, wk_ref,
                     kall_ref, vall_ref, rq_ref, rk_ref, ka_ref, va_ref):
    wk = wk_ref[...]
    ones_col = _ones_column(ak_ref.shape[0])
    for h in range(N_HEADS):
        sl = slice(h * HEAD_W, (h + 1) * HEAD_W)
        kn = _norm_halves(ak_ref[:, sl].astype(F32)) * wk
        ka_ref[0, :, sl] = kn
        kall_ref[0, h] = kn.astype(BF16)
        v = av_ref[:, sl]
        va_ref[0, :, sl] = v.astype(F32)
        vall_ref[0, h, :, 0:HEAD_W] = v
        vall_ref[0, h, :, HEAD_W:2 * HEAD_W] = ones_col
    rq_ref[...] = bq_ref[...].astype(F32)
    rk_ref[...] = bk_ref[...].astype(F32) * (DK_B ** -0.5)


def _prep_lat_kernel(ak_ref, av_ref, bq_ref, bk_ref, wk_ref, cos_ref, sa_ref, sb_ref,
                     ck_ref, cv_ref, kall_ref, vall_ref, rq_ref, rk_ref, *, nt):
    t = pl.program_id(1)
    ones_col = _ones_column(ak_ref.shape[0])

    @pl.when(t < nt)
    def _():
        wk = wk_ref[...]
        cos, sa, sb = cos_ref[...], sa_ref[...], sb_ref[...]
        for h in range(N_HEADS):
            sl = slice(h * HEAD_W, (h + 1) * HEAD_W)
            kn = _norm_halves(ak_ref[:, sl].astype(F32)) * wk
            kall_ref[0, h] = _rope(kn, cos, sa, sb).astype(BF16)
            vall_ref[0, h, :, 0:HEAD_W] = av_ref[:, sl]
            vall_ref[0, h, :, HEAD_W:2 * HEAD_W] = ones_col
        for e in range(2):
            sl = slice(e * LANES, (e + 1) * LANES)
            rq_ref[:, sl] = _rope(bq_ref[:, sl].astype(F32), cos, sa, sb)
            rk_ref[:, sl] = _rope(bk_ref[:, sl].astype(F32), cos, sa, sb) * (DK_B ** -0.5)

    @pl.when(t == nt)
    def _():
        for h in range(N_HEADS):
            sl = slice(h * HEAD_W, (h + 1) * HEAD_W)
            kall_ref[0, h] = ck_ref[:, sl].astype(BF16)
            vall_ref[0, h, :, 0:HEAD_W] = cv_ref[:, sl].astype(BF16)
            vall_ref[0, h, :, HEAD_W:2 * HEAD_W] = ones_col


def _prep_call(p, b, t, wk_row, rope, cache, layer):
    tk = 256
    nt = t // tk
    n = b * t
    if rope is None:
        return pl.pallas_call(
            _prep_ctx_kernel,
            grid=(b, nt),
            in_specs=[pl.BlockSpec((tk, 512), lambda bi, ti: (bi * nt + ti, OFF_AK // 512)),
                      pl.BlockSpec((tk, 512), lambda bi, ti: (bi * nt + ti, OFF_AV // 512)),
                      pl.BlockSpec((tk, 256), lambda bi, ti: (bi * nt + ti, OFF_BQ // 256)),
                      pl.BlockSpec((tk, 256), lambda bi, ti: (bi * nt + ti, OFF_BK // 256)),
                      pl.BlockSpec((1, LANES), lambda bi, ti: (0, 0))],
            out_specs=[pl.BlockSpec((1, N_HEADS, tk, HEAD_W), lambda bi, ti: (bi, 0, ti, 0)),
                       pl.BlockSpec((1, N_HEADS, tk, 2 * HEAD_W), lambda bi, ti: (bi, 0, ti, 0)),
                       pl.BlockSpec((tk, 256), lambda bi, ti: (bi * nt + ti, 0)),
                       pl.BlockSpec((tk, 256), lambda bi, ti: (bi * nt + ti, 0)),
                       pl.BlockSpec((1, tk, 512), lambda bi, ti: (bi, ti, 0)),
                       pl.BlockSpec((1, tk, 512), lambda bi, ti: (bi, ti, 0))],
            out_shape=[jax.ShapeDtypeStruct((b, N_HEADS, t, HEAD_W), BF16),
                       jax.ShapeDtypeStruct((b, N_HEADS, t, 2 * HEAD_W), BF16),
                       jax.ShapeDtypeStruct((n, 256), F32),
                       jax.ShapeDtypeStruct((n, 256), F32),
                       jax.ShapeDtypeStruct((b, t, 512), F32),
                       jax.ShapeDtypeStruct((b, t, 512), F32)],
            compiler_params=_cparams(("parallel", "parallel")),
        )(p, p, p, p, wk_row)
    cos, sa, sb = rope
    cache_k, cache_v = cache
    past = cache_k.shape[2]
    assert past == tk
    s = t + past

    def tok(bi, ti):
        return bi * nt + jnp.minimum(ti, nt - 1)

    return pl.pallas_call(
        functools.partial(_prep_lat_kernel, nt=nt),
        grid=(b, nt + 1),
        in_specs=[pl.BlockSpec((tk, 512), lambda bi, ti: (tok(bi, ti), OFF_AK // 512)),
                  pl.BlockSpec((tk, 512), lambda bi, ti: (tok(bi, ti), OFF_AV // 512)),
                  pl.BlockSpec((tk, 256), lambda bi, ti: (tok(bi, ti), OFF_BQ // 256)),
                  pl.BlockSpec((tk, 256), lambda bi, ti: (tok(bi, ti), OFF_BK // 256)),
                  pl.BlockSpec((1, LANES), lambda bi, ti: (0, 0)),
                  pl.BlockSpec((tk, LANES), lambda bi, ti: (jnp.minimum(ti, nt - 1), 0)),
                  pl.BlockSpec((tk, LANES), lambda bi, ti: (jnp.minimum(ti, nt - 1), 0)),
                  pl.BlockSpec((tk, LANES), lambda bi, ti: (jnp.minimum(ti, nt - 1), 0)),
                  pl.BlockSpec((None, None, past, 512), lambda bi, ti: (bi, layer, 0, 0)),
                  pl.BlockSpec((None, None, past, 512), lambda bi, ti: (bi, layer, 0, 0))],
        out_specs=[pl.BlockSpec((1, N_HEADS, tk, HEAD_W), lambda bi, ti: (bi, 0, ti, 0)),
                   pl.BlockSpec((1, N_HEADS, tk, 2 * HEAD_W), lambda bi, ti: (bi, 0, ti, 0)),
                   pl.BlockSpec((tk, 256), lambda bi, ti: (tok(bi, ti), 0)),
                   pl.BlockSpec((tk, 256), lambda bi, ti: (tok(bi, ti), 0))],
        out_shape=[jax.ShapeDtypeStruct((b, N_HEADS, s, HEAD_W), BF16),
                   jax.ShapeDtypeStruct((b, N_HEADS, s, 2 * HEAD_W), BF16),
                   jax.ShapeDtypeStruct((n, 256), F32),
                   jax.ShapeDtypeStruct((n, 256), F32)],
        compiler_params=_cparams(("parallel", "arbitrary")),
    )(p, p, p, p, wk_row, cos, sa, sb, cache_k, cache_v)


def _attn_kernel(*refs, lam_init, rope):
    if rope:
        (q_ref, z_ref, k_ref, v_ref, wq_ref, dl_ref, sw_ref, cos_ref, sa_ref, sb_ref,
         o_ref, s_ref, p_ref) = refs
    else:
        q_ref, z_ref, k_ref, v_ref, wq_ref, dl_ref, sw_ref, o_ref, s_ref, p_ref = refs
    n_keys = k_ref.shape[2]
    tq = q_ref.shape[0]
    vals = v_ref[0, 0]
    lv = dl_ref[...]
    lam = (jnp.exp(jnp.sum(lv[0:1] * lv[1:2], axis=-1, keepdims=True))
           - jnp.exp(jnp.sum(lv[2:3] * lv[3:4], axis=-1, keepdims=True)) + lam_init)
    chains = [(r0, m) for r0 in range(0, tq, ATTN_SUB) for m in range(2)]
    q_scaled = {}
    row_max = {}
    den = {}
    acc = {}

    def stage_scores(r0, m):
        rows = slice(r0, r0 + ATTN_SUB)
        if r0 not in q_scaled:
            qn = _norm_halves(q_ref[rows, :].astype(F32)) * wq_ref[...]
            if rope:
                qn = _rope(qn, cos_ref[rows, :], sa_ref[rows, :], sb_ref[rows, :])
            q_scaled[r0] = qn * (DK_A ** -0.5 * LOG2E)
        qn = q_scaled[r0]
        lo = _lane_iota(qn.shape) < DK_A
        qm = jnp.where(lo if m == 0 else jnp.logical_not(lo), qn, 0.0).astype(BF16)
        mx = None
        for c0 in range(0, n_keys, ATTN_KEYS):
            s = _dot_nt(qm, k_ref[0, 0, c0:c0 + ATTN_KEYS, :])
            s_ref[m, rows, c0:c0 + ATTN_KEYS] = s
            for j in range(ATTN_KEYS // LANES):
                sj = s[:, j * LANES:(j + 1) * LANES]
                mx = sj if mx is None else jnp.maximum(mx, sj)
        row_max[r0, m] = jnp.max(mx, axis=-1, keepdims=True)

    def stage_softmax(r0, m):
        rows = slice(r0, r0 + ATTN_SUB)
        mx = row_max[r0, m]
        for j in range(n_keys // LANES):
            sl = slice(j * LANES, (j + 1) * LANES)
            p_ref[m, rows, sl] = jnp.exp2(s_ref[m, rows, sl] - mx).astype(BF16)

    def stage_values(r0, m):
        rows = slice(r0, r0 + ATTN_SUB)
        pv = _dot(p_ref[m, rows, :], vals)
        acc[r0, m] = pv[:, 0:HEAD_W]
        den[r0, m] = pv[:, HEAD_W:HEAD_W + 1]
        if m == 1:
            o = (acc[r0, 0] * (1.0 / den[r0, 0]) - lam * (acc[r0, 1] * (1.0 / den[r0, 1])))
            o = o * lax.rsqrt(jnp.mean(o * o, axis=-1, keepdims=True) + EPS) * sw_ref[...]
            o_ref[rows, :] = o * (1.0 - lam_init) * _silu(z_ref[rows, :].astype(F32))

    for step in range(len(chains) + 2):
        if step < len(chains):
            stage_scores(*chains[step])
        if 0 <= step - 1 < len(chains):
            stage_softmax(*chains[step - 1])
        if 0 <= step - 2 < len(chains):
            stage_values(*chains[step - 2])


def _attn_call(p, kall, vall, wq_row, dlam, subln_row, rope, b, t, lam_init):
    tq = min(512, t)
    s = kall.shape[2]
    nq = t // tq
    in_specs = [pl.BlockSpec((tq, HEAD_W), lambda bi, h, i: (bi * nq + i, OFF_AQ // HEAD_W + h)),
                pl.BlockSpec((tq, HEAD_W), lambda bi, h, i: (bi * nq + i, OFF_AZ // HEAD_W + h)),
                pl.BlockSpec((1, 1, s, HEAD_W), lambda bi, h, i: (bi, h, 0, 0)),
                pl.BlockSpec((1, 1, s, 2 * HEAD_W), lambda bi, h, i: (bi, h, 0, 0)),
                pl.BlockSpec((1, LANES), lambda bi, h, i: (0, 0)),
                pl.BlockSpec((4, DK_A), lambda bi, h, i: (0, 0)),
                pl.BlockSpec((1, LANES), lambda bi, h, i: (0, 0))]
    args = [p, p, kall, vall, wq_row, dlam, subln_row]
    if rope is not None:
        in_specs += [pl.BlockSpec((tq, LANES), lambda bi, h, i: (i, 0))] * 3
        args += list(rope)
    return pl.pallas_call(
        functools.partial(_attn_kernel, lam_init=lam_init, rope=rope is not None),
        grid=(b, N_HEADS, nq),
        in_specs=in_specs,
        out_specs=pl.BlockSpec((tq, HEAD_W), lambda bi, h, i: (bi * nq + i, h)),
        out_shape=jax.ShapeDtypeStruct((b * t, N_HEADS * HEAD_W), F32),
        scratch_shapes=[pltpu.VMEM((2, tq, s), F32), pltpu.VMEM((2, tq, s), BF16)],
        compiler_params=_cparams(("parallel", "parallel", "arbitrary")),
    )(*args)


def _ret_kernel(*refs, nchunk, has_state):
    if has_state:
        (q_ref, k_ref, v_ref, z_ref, d256_ref, d128_ref, nw_ref, s0_ref,
         o_ref, s_ref, dm_ref, qd_ref, kd_ref, ob_ref) = refs
        st_ref = None
    else:
        (q_ref, k_ref, v_ref, z_ref, d256_ref, d128_ref, nw_ref,
         o_ref, st_ref, s_ref, dm_ref, qd_ref, kd_ref, ob_ref) = refs
    c_len = CHUNK
    lg256 = -_softplus(-d256_ref[...])
    lg128 = -_softplus(-d128_ref[...])
    ri = _row_iota((c_len, c_len)).astype(F32)
    ci = _lane_iota((c_len, c_len)).astype(F32)
    rcol256 = _row_iota((c_len, 2 * HEAD_W)).astype(F32)
    rcol128 = _row_iota((c_len, LANES)).astype(F32)
    blockmask = ((_row_iota((2 * DK_B, 2 * HEAD_W)) < DK_B)
                 == (_lane_iota((2 * DK_B, 2 * HEAD_W)) < HEAD_W))
    for d in range(2):
        rel = (ri - ci) if d == 0 else (ci - ri)
        keep = rel >= 0
        for e in range(2):
            lg = lg256[d:d + 1, e * HEAD_W:e * HEAD_W + 1]
            dm_ref[d, e] = jnp.where(keep, jnp.exp(jnp.where(keep, rel, 0.0) * lg), 0.0)
        qpow = (rcol256 + 1.0) if d == 0 else (c_len - rcol256)
        qd_ref[d] = jnp.exp(qpow * lg256[d:d + 1])
        kpow = (c_len - 1.0 - rcol128) if d == 0 else rcol128
        kd_ref[d] = jnp.exp(kpow * lg128[d:d + 1])
        if has_state:
            s_ref[d] = jnp.zeros((2 * DK_B, 2 * HEAD_W), F32)
            for e in range(2):
                s_ref[d, e * DK_B:(e + 1) * DK_B, e * HEAD_W:(e + 1) * HEAD_W] = s0_ref[d, e]
        else:
            s_ref[d] = jnp.zeros((2 * DK_B, 2 * HEAD_W), F32)
    cdec = [jnp.exp(float(c_len) * lg256[d:d + 1]) for d in range(2)]
    lane_lo = _lane_iota((c_len, LANES)) < DK_B
    nw = nw_ref[...]

    def step(c, d):
        t0 = pl.multiple_of(c * c_len, c_len)
        q = q_ref[pl.ds(t0, c_len), :]
        k = k_ref[pl.ds(t0, c_len), :]
        vb = v_ref[pl.ds(t0, c_len), :]
        st = s_ref[d]
        inter = _dot(q.astype(BF16), st.astype(BF16)) * qd_ref[d]
        kb = k.astype(BF16)
        dst = o_ref if d == 0 else ob_ref
        for e in range(2):
            sl = slice(e * HEAD_W, (e + 1) * HEAD_W)
            qe = jnp.where(lane_lo if e == 0 else jnp.logical_not(lane_lo), q, 0.0).astype(BF16)
            sc = _dot_nt(qe, kb) * dm_ref[d, e]
            dst[pl.ds(t0, c_len), sl] = _dot(sc.astype(BF16), vb[:, sl]) + inter[:, sl]
        kdk = (k * kd_ref[d]).astype(BF16)
        kv = lax.dot_general(kdk, vb, (((0,), (0,)), ((), ())), preferred_element_type=F32)
        s_ref[d] = jnp.where(blockmask, st * cdec[d] + kv, 0.0)

    def scan_body(i, carry):
        step(i, 0)
        step(nchunk - 1 - i, 1)
        return carry

    lax.fori_loop(0, nchunk, scan_body, 0)

    def fin_body(i, carry):
        t0 = pl.multiple_of(i * c_len, c_len)
        for e in range(2):
            sl = slice(e * HEAD_W, (e + 1) * HEAD_W)
            o_e = o_ref[pl.ds(t0, c_len), sl] + ob_ref[pl.ds(t0, c_len), sl]
            o_e = o_e * lax.rsqrt(jnp.mean(o_e * o_e, axis=-1, keepdims=True) + EPS) * nw
            o_ref[pl.ds(t0, c_len), sl] = o_e * _silu(z_ref[pl.ds(t0, c_len), sl].astype(F32))
        return carry

    lax.fori_loop(0, nchunk, fin_body, 0)
    if not has_state:
        for d in range(2):
            for e in range(2):
                st_ref[d, e] = s_ref[d, e * DK_B:(e + 1) * DK_B, e * HEAD_W:(e + 1) * HEAD_W]


def _ret_call(p, rq, rk, d256, d128, nw_row, state, b, t, layer):
    nchunk = t // CHUNK
    hp = N_HEADS // 2
    in_specs = [pl.BlockSpec((t, LANES), lambda bi, h: (bi, h)),
                pl.BlockSpec((t, LANES), lambda bi, h: (bi, h)),
                pl.BlockSpec((t, 256), lambda bi, h: (bi, OFF_BV // 256 + h)),
                pl.BlockSpec((t, 256), lambda bi, h: (bi, OFF_BZ // 256 + h)),
                pl.BlockSpec((None, 2, 256), lambda bi, h: (h, 0, 0)),
                pl.BlockSpec((None, 2, LANES), lambda bi, h: (h, 0, 0)),
                pl.BlockSpec((1, LANES), lambda bi, h: (0, 0))]
    args = [rq, rk, p, p, d256, d128, nw_row]
    scratch = [pltpu.VMEM((2, 2 * DK_B, 2 * HEAD_W), F32),
               pltpu.VMEM((2, 2, CHUNK, CHUNK), F32),
               pltpu.VMEM((2, CHUNK, 2 * HEAD_W), F32),
               pltpu.VMEM((2, CHUNK, LANES), F32),
               pltpu.VMEM((t, 2 * HEAD_W), F32)]
    o_spec = pl.BlockSpec((t, 256), lambda bi, h: (bi, h))
    o_shape = jax.ShapeDtypeStruct((b * t, N_HEADS * HEAD_W), F32)
    if state is not None:
        in_specs.append(pl.BlockSpec((None, None, 2, 2, DK_B, HEAD_W),
                                     lambda bi, h: (bi, layer, 0, h, 0, 0)))
        args.append(state)
        out_specs, out_shape = o_spec, o_shape
    else:
        out_specs = [o_spec, pl.BlockSpec((None, 2, 2, DK_B, HEAD_W), lambda bi, h: (bi, 0, h, 0, 0))]
        out_shape = [o_shape, jax.ShapeDtypeStruct((b, 2, N_HEADS, DK_B, HEAD_W), F32)]
    return pl.pallas_call(
        functools.partial(_ret_kernel, nchunk=nchunk, has_state=state is not None),
        grid=(b, hp),
        in_specs=in_specs,
        out_specs=out_specs,
        out_shape=out_shape,
        scratch_shapes=scratch,
        compiler_params=_cparams(("parallel", "parallel")),
    )(*args)


def _gdn_kernel(*refs, t, has_state):
    if has_state:
        (cq_ref, ck_ref, cv_ref, cz_ref, ba_ref, wq_ref, wk_ref, wv_ref, al_ref, dt_ref, nw_ref,
         s0_ref, o_ref, qs_ref, ks_ref, vs_ref, u_ref, w_ref, qk_ref, qg_ref, kdt_ref, et_ref,
         ob_ref, s_ref) = refs
        st_ref = None
    else:
        (cq_ref, ck_ref, cv_ref, cz_ref, ba_ref, wq_ref, wk_ref, wv_ref, al_ref, dt_ref, nw_ref,
         o_ref, st_ref, qs_ref, ks_ref, vs_ref, u_ref, w_ref, qk_ref, qg_ref, kdt_ref, et_ref,
         ob_ref, s_ref) = refs
    c_len = CHUNK
    nchunk = t // c_len
    h = pl.program_id(1)
    pad = 16
    win = 256
    nwin = t // win

    def conv_window(r0, first, last):
        lo = 0 if first else pad
        hi = 0 if last else pad
        start = r0 - lo if isinstance(r0, int) else pl.multiple_of(r0 - lo, pad)
        for a, (src, cw_ref, dst) in enumerate(((cq_ref, wq_ref, qs_ref), (ck_ref, wk_ref, ks_ref),
                                                (cv_ref, wv_ref, vs_ref))):
            xw = src[pl.ds(start, win + lo + hi), :].astype(F32)
            if first:
                xw = jnp.concatenate([jnp.zeros((pad, LANES), F32), xw], axis=0)
            if last:
                xw = jnp.concatenate([xw, jnp.zeros((pad, LANES), F32)], axis=0)
            cw = cw_ref[...]
            acc = None
            for j in range(CONV_K):
                off = pad - CONV_K // 2 + j
                term = xw[off:off + win, :] * cw[j:j + 1, :]
                acc = term if acc is None else acc + term
            y = _silu(acc)
            if a < 2:
                y = y * lax.rsqrt(jnp.sum(y * y, axis=-1, keepdims=True) + EPS)
            if a == 0:
                y = y * (HEAD_W ** -0.5)
            dst[pl.ds(r0, win), :] = y

    if nwin == 1:
        conv_window(0, True, True)
    else:
        conv_window(0, True, False)

        def conv_body(i, carry):
            conv_window(pl.multiple_of(i * win, win), False, False)
            return carry

        lax.fori_loop(1, nwin - 1, conv_body, 0)
        conv_window((nwin - 1) * win, False, True)

    ri = _row_iota((c_len, c_len))
    ci = _lane_iota((c_len, c_len))
    lane = _lane_iota((c_len, LANES))
    alog = al_ref[...]
    dtb = dt_ref[...]
    tri16 = jnp.where(ri >= ci, 1.0, 0.0).astype(BF16)
    g_lanes = jnp.logical_and(lane >= 2 * N_HEADS, lane < 4 * N_HEADS)

    def lanes3(tgt):
        return jnp.logical_or(lane == tgt, jnp.logical_or(lane == tgt + 16, lane == tgt + 32))

    def chunk_chains(c):
        t0 = pl.multiple_of(c * c_len, c_len)
        ba = ba_ref[pl.ds(t0, c_len), :]
        beta_all = _sigmoid(ba)
        g_all = jnp.where(g_lanes, -jnp.exp(alog) * _softplus(ba + dtb), 0.0)
        g_hi = g_all.astype(BF16).astype(F32)
        r1 = g_all - g_hi
        g_mid = r1.astype(BF16).astype(F32)
        g_lo = (r1 - g_mid).astype(BF16).astype(F32)
        parts = (g_hi + pltpu.roll(g_mid, 16, 1) + pltpu.roll(g_lo, 32, 1)).astype(BF16)
        pre = _dot(tri16, parts)
        tgt_f = 2 * N_HEADS + h
        tgt_b = 3 * N_HEADS + h
        gc_f = jnp.sum(jnp.where(lanes3(tgt_f), pre, 0.0), axis=-1, keepdims=True)
        pre_b = jnp.sum(jnp.where(lanes3(tgt_b), pre, 0.0), axis=-1, keepdims=True)
        g_b = jnp.sum(jnp.where(lane == tgt_b, g_all, 0.0), axis=-1, keepdims=True)
        gc_b = pre_b[c_len - 1:c_len, :] - pre_b + g_b
        beta_f = jnp.sum(jnp.where(lane == h, beta_all, 0.0), axis=-1, keepdims=True)
        beta_b = jnp.sum(jnp.where(lane == N_HEADS + h, beta_all, 0.0), axis=-1, keepdims=True)
        rows = jnp.where(lane == 0, gc_f, jnp.where(lane == 1, gc_b, 0.0)).T

        q = qs_ref[pl.ds(t0, c_len), :]
        k = ks_ref[pl.ds(t0, c_len), :]
        v = vs_ref[pl.ds(t0, c_len), :]
        k16 = k.astype(BF16)
        kkqk = _dot_nt(jnp.concatenate([k16, q.astype(BF16)], axis=0), k16)
        kk = kkqk[0:c_len, :]
        qk_raw = kkqk[c_len:2 * c_len, :]
        chains = []
        for d, (gc, beta) in enumerate(((gc_f, beta_f), (gc_b, beta_b))):
            gr = rows[d:d + 1, :]
            incl = (ri >= ci) if d == 0 else (ci >= ri)
            strict = (ri > ci) if d == 0 else (ci > ri)
            ex = jnp.exp(jnp.where(incl, gc - gr, 0.0))
            a_mat = jnp.where(strict, ex, 0.0) * kk * beta
            egc = jnp.exp(gc)
            rhs = jnp.concatenate([v * beta, k * (beta * egc)], axis=1)
            qk_ref[d, pl.ds(t0, c_len), :] = (jnp.where(incl, ex, 0.0) * qk_raw).astype(BF16)
            qg_ref[d, pl.ds(t0, c_len), :] = (q * egc).astype(BF16)
            g_tot = gc[c_len - 1:c_len, :] if d == 0 else gc[0:1, :]
            kdt_ref[d, pl.ds(t0, c_len), :] = (k * jnp.exp(g_tot - gc)).T.astype(BF16)
            et_ref[d, c] = jnp.broadcast_to(jnp.exp(g_tot), (8, LANES))
            chains.append((a_mat, rhs, (d, t0)))
        return chains

    group = min(GDN_GROUP, nchunk)
    assert nchunk % group == 0

    def prep_body(i, carry):
        chains = []
        for j in range(group):
            chains += chunk_chains(i * group + j)
        n_mats = [-jnp.where((ri >> 1) == (ci >> 1), a_mat, 0.0) for a_mat, _, _ in chains]
        sh = 1
        while (1 << sh) < c_len:
            off = jnp.logical_and((ri >> (sh + 1)) == (ci >> (sh + 1)), (ri >> sh) != (ci >> sh))
            l_mats = [jnp.where(off, a_mat, 0.0) for a_mat, _, _ in chains]
            x_mats = [l + _dot(l.astype(BF16), n.astype(BF16)) for l, n in zip(l_mats, n_mats)]
            n_mats = [n - x - _dot(n.astype(BF16), x.astype(BF16)) for n, x in zip(n_mats, x_mats)]
            sh += 1
        for n_mat, (_, rhs, (d, t0)) in zip(n_mats, chains):
            uw = rhs + _dot(n_mat.astype(BF16), rhs.astype(BF16))
            u_ref[d, pl.ds(t0, c_len), :] = uw[:, 0:HEAD_W]
            w_ref[d, pl.ds(t0, c_len), :] = uw[:, HEAD_W:2 * HEAD_W].astype(BF16)
        return carry

    lax.fori_loop(0, nchunk // group, prep_body, 0)

    for d in range(2):
        s_ref[d] = s0_ref[d] if has_state else jnp.zeros((HEAD_W, HEAD_W), F32)

    def scan_body(i, carry):
        for d, c in ((0, i), (1, nchunk - 1 - i)):
            t0 = pl.multiple_of(c * c_len, c_len)
            st = s_ref[d]
            st16 = st.astype(BF16)
            v_new = u_ref[d, pl.ds(t0, c_len), :] - _dot(w_ref[d, pl.ds(t0, c_len), :], st16)
            vn16 = v_new.astype(BF16)
            o = (_dot(qg_ref[d, pl.ds(t0, c_len), :], st16)
                 + _dot(qk_ref[d, pl.ds(t0, c_len), :], vn16))
            s_ref[d] = st * et_ref[d, c][0:1, :] + _dot(kdt_ref[d, pl.ds(t0, c_len), :], vn16)
            if d == 0:
                o_ref[pl.ds(t0, c_len), :] = o
            else:
                ob_ref[pl.ds(t0, c_len), :] = o
        return carry

    lax.fori_loop(0, nchunk, scan_body, 0)

    def fin_body(i, carry):
        r0 = pl.multiple_of(i * win, win)
        o = o_ref[pl.ds(r0, win), :] + ob_ref[pl.ds(r0, win), :]
        o = o * lax.rsqrt(jnp.mean(o * o, axis=-1, keepdims=True) + EPS) * nw_ref[...]
        o_ref[pl.ds(r0, win), :] = o * _silu(cz_ref[pl.ds(r0, win), :].astype(F32))
        return carry

    lax.fori_loop(0, nwin, fin_body, 0)
    if not has_state:
        for d in range(2):
            st_ref[d] = s_ref[d]


def _gdn_call(p, p_ba, conv_w, al_row, dt_row, nw_row, state, b, t, layer):
    def col(off):
        return lambda bi, h: (bi, off // LANES + h)

    in_specs = [pl.BlockSpec((t, LANES), col(OFF_CQ)),
                pl.BlockSpec((t, LANES), col(OFF_CK)),
                pl.BlockSpec((t, LANES), col(OFF_CV)),
                pl.BlockSpec((t, LANES), col(OFF_CZ)),
                pl.BlockSpec((t, LANES), lambda bi, h: (bi, 0)),
                pl.BlockSpec((CONV_K, LANES), lambda bi, h: (0, h)),
                pl.BlockSpec((CONV_K, LANES), lambda bi, h: (0, N_HEADS + h)),
                pl.BlockSpec((CONV_K, LANES), lambda bi, h: (0, 2 * N_HEADS + h)),
                pl.BlockSpec((1, LANES), lambda bi, h: (0, 0)),
                pl.BlockSpec((1, LANES), lambda bi, h: (0, 0)),
                pl.BlockSpec((1, LANES), lambda bi, h: (0, 0))]
    args = [p, p, p, p, p_ba, conv_w, conv_w, conv_w, al_row, dt_row, nw_row]
    scratch = [pltpu.VMEM((t, LANES), F32),
               pltpu.VMEM((t, LANES), F32),
               pltpu.VMEM((t, LANES), F32),
               pltpu.VMEM((2, t, LANES), F32),
               pltpu.VMEM((2, t, LANES), BF16),
               pltpu.VMEM((2, t, LANES), BF16),
               pltpu.VMEM((2, t, LANES), BF16),
               pltpu.VMEM((2, t, LANES), BF16),
               pltpu.VMEM((2, t // CHUNK, 8, LANES), F32),
               pltpu.VMEM((t, LANES), F32),
               pltpu.VMEM((2, HEAD_W, HEAD_W), F32)]
    o_spec = pl.BlockSpec((t, LANES), lambda bi, h: (bi, h))
    o_shape = jax.ShapeDtypeStruct((b * t, N_HEADS * HEAD_W), F32)
    if state is not None:
        in_specs.append(pl.BlockSpec((None, None, 2, None, HEAD_W, HEAD_W),
                                     lambda bi, h: (bi, layer, 0, h, 0, 0)))
        args.append(state)
        out_specs, out_shape = o_spec, o_shape
    else:
        out_specs = [o_spec, pl.BlockSpec((None, 2, None, HEAD_W, HEAD_W),
                                          lambda bi, h: (bi, 0, h, 0, 0))]
        out_shape = [o_shape, jax.ShapeDtypeStruct((b, 2, N_HEADS, HEAD_W, HEAD_W), F32)]
    return pl.pallas_call(
        functools.partial(_gdn_kernel, t=t, has_state=state is not None),
        grid=(b, N_HEADS),
        in_specs=in_specs,
        out_specs=out_specs,
        out_shape=out_shape,
        scratch_shapes=scratch,
        compiler_params=_cparams(("parallel", "parallel")),
    )(*args)


def _out_kernel(ya_ref, yb_ref, yc_ref, mg_ref, x_ref, mod_ref, wb_ref, wo_ref, o_ref, *, d):
    ya = _dot(ya_ref[...].astype(BF16), wb_ref[0])
    yb = _dot(yb_ref[...].astype(BF16), wb_ref[1])
    yc = _dot(yc_ref[...].astype(BF16), wb_ref[2])
    y = (_sigmoid(mg_ref[:, 0:d].astype(F32)) * ya + _sigmoid(mg_ref[:, d:2 * d].astype(F32)) * yb
         + _sigmoid(mg_ref[:, 2 * d:3 * d].astype(F32)) * yc)
    out = _dot(y.astype(BF16), wo_ref[...])
    gate = mod_ref[0][:, 2 * d:3 * d]
    o_ref[...] = x_ref[...] + gate * out


def _out_call(ya, yb, yc, p, x2, mod, mod_row, wb, wo, mg_block, tm):
    n, d = x2.shape
    w_br = ya.shape[1]
    return pl.pallas_call(
        functools.partial(_out_kernel, d=d),
        grid=(n // tm,),
        in_specs=[pl.BlockSpec((tm, w_br), lambda i: (i, 0)),
                  pl.BlockSpec((tm, w_br), lambda i: (i, 0)),
                  pl.BlockSpec((tm, w_br), lambda i: (i, 0)),
                  pl.BlockSpec((tm, 3 * d), lambda i: (i, mg_block)),
                  pl.BlockSpec((tm, d), lambda i: (i, 0)),
                  pl.BlockSpec((1, 1, 3 * d), lambda i: (mod_row(i), 0, 0)),
                  pl.BlockSpec((3, w_br, d), lambda i: (0, 0, 0)),
                  pl.BlockSpec((d, d), lambda i: (0, 0))],
        out_specs=pl.BlockSpec((tm, d), lambda i: (i, 0)),
        out_shape=jax.ShapeDtypeStruct((n, d), F32),
        compiler_params=_cparams(("parallel",)),
    )(ya, yb, yc, p, x2, mod, wb, wo)


def _rope_tables(n_tokens, dtype):
    n_rows = n_tokens // GRID_W
    row = jnp.repeat(jnp.arange(n_rows, dtype=jnp.float32), GRID_W)
    col = jnp.tile(jnp.arange(GRID_W, dtype=jnp.float32), n_rows)
    n_freq = DK_A // 4
    inv = 1.0 / (ROPE_BASE ** (jnp.arange(n_freq, dtype=jnp.float32) / n_freq))
    ar = row[:, None] * inv
    ac = col[:, None] * inv
    ang = jnp.concatenate([ar, ar, ac, ac], axis=-1)
    cos = jnp.tile(jnp.cos(ang).astype(dtype), (1, 2))
    sin = jnp.tile(jnp.sin(ang).astype(dtype), (1, 2))
    first = (jnp.arange(LANES) % 32) < 16
    sin_a = jnp.where(first, -sin, 0.0)
    sin_b = jnp.where(first, 0.0, sin)
    return cos, sin_a, sin_b


PROJ_TM = 1024
PROJ_TN = 1536
OUT_TM = 256


def _layer(x2, mod, mod_row_for, layer, wts, rope, ctx, b, t):
    p, p_ba = _proj_call(x2, mod, mod_row_for(PROJ_TM), wts["norm_w"], wts["w_in"], PROJ_TM)
    cache = None if ctx is None else (ctx[0], ctx[1])
    prep = _prep_call(p, b, t, wts["wk_row"], rope, cache, layer)
    kall, vall, rq, rk = prep[:4]
    lam_init = 0.8 - 0.6 * math.exp(-0.3 * layer)
    ya = _attn_call(p, kall, vall, wts["wq_row"], wts["diff_lambda"], wts["subln_row"], rope,
                    b, t, lam_init)
    ret = _ret_call(p, rq, rk, wts["d256"], wts["d128"], wts["ret_norm_row"],
                    None if ctx is None else ctx[2], b, t, layer)
    gdn = _gdn_call(p, p_ba, wts["conv_w"], wts["al_row"], wts["dt_row"], wts["gdn_norm_row"],
                    None if ctx is None else ctx[3], b, t, layer)
    if ctx is None:
        yb, s_ret = ret
        yc, s_gdn = gdn
        extras = (prep[4], prep[5], s_ret, s_gdn)
    else:
        yb, yc = ret, gdn
        extras = None
    mg_block = wts["mg_off"] // (3 * x2.shape[1])
    x2 = _out_call(ya, yb, yc, p, x2, mod, mod_row_for(OUT_TM), wts["w_branch"], wts["w_out"],
                   mg_block, OUT_TM)
    return x2, extras


def kernel(x_prompt, x_sample, cache_attn_k, cache_attn_v, state_ret, state_gdn, c, c_ctx,
           norm_w, w_ada, b_ada, w_in, qk_norm_w, diff_lambda, subln_w, ret_decay, ret_norm_w,
           conv_w, gdn_a_log, gdn_dt_bias, gdn_norm_w, w_branch, w_out):
    b_ctx, t_ctx, d = x_prompt.shape
    b_lat, t_lat, _ = x_sample.shape
    depth = w_in.shape[0]
    past = cache_attn_k.shape[2]
    assert b_lat <= 4 and d % LANES == 0

    cond = jnp.zeros((8, d), F32).at[:b_lat].set(c).at[4].set(c_ctx)
    mods = _ada_call(cond, w_ada, b_ada)

    mg_off = -(-(OFF_BA + LANES) // (3 * d)) * (3 * d)
    n_cols = mg_off + 3 * d
    n_cols = -(-n_cols // 1536) * 1536
    w_pad = jnp.zeros((depth, d, n_cols), BF16)
    w_pad = w_pad.at[:, :, :N_MIX + 16].set(w_in[:, :, :N_MIX + 16].astype(BF16))
    w_pad = w_pad.at[:, :, mg_off:mg_off + 3 * d].set(w_in[:, :, N_MIX + 16:].astype(BF16))

    rope = _rope_tables(t_lat, x_sample.dtype)
    cache_k = cache_attn_k.reshape(b_lat, depth, past, N_HEADS * 2 * DK_A)
    cache_v = cache_attn_v.reshape(b_lat, depth, past, N_HEADS * HEAD_W)

    lanes16 = jnp.zeros((depth, LANES), F32)
    al_rows = lanes16.at[:, 8:16].set(gdn_a_log.reshape(depth, 8))
    dt_rows = lanes16.at[:, 8:16].set(gdn_dt_bias.reshape(depth, 8))
    dec = ret_decay.reshape(depth, 2, 2, 2)
    dec = jnp.transpose(dec, (0, 2, 1, 3))
    d256 = jnp.repeat(dec, HEAD_W, axis=-1)
    d128 = jnp.repeat(dec, DK_B, axis=-1)

    y_p = x_prompt.reshape(b_ctx * t_ctx, d)
    y_s = x_sample.reshape(b_lat * t_lat, d)
    assert t_lat % PROJ_TM == 0 and (b_ctx * t_ctx) % PROJ_TM == 0

    def ctx_row(tm):
        return lambda i: 4

    def lat_row(tm):
        return lambda i: i // (t_lat // tm)

    ks, vs, rs, gs = [], [], [], []
    for l in range(depth):
        wts = {
            "norm_w": norm_w[l].reshape(1, d),
            "w_in": w_pad[l],
            "mg_off": mg_off,
            "wq_row": jnp.tile(qk_norm_w[l, 0], 2).reshape(1, LANES),
            "wk_row": jnp.tile(qk_norm_w[l, 1], 2).reshape(1, LANES),
            "diff_lambda": diff_lambda[l],
            "subln_row": subln_w[l].reshape(1, LANES),
            "d256": d256[l], "d128": d128[l],
            "ret_norm_row": ret_norm_w[l].reshape(1, LANES),
            "conv_w": conv_w[l],
            "al_row": al_rows[l].reshape(1, LANES),
            "dt_row": dt_rows[l].reshape(1, LANES),
            "gdn_norm_row": gdn_norm_w[l].reshape(1, LANES),
            "w_branch": w_branch[l].astype(BF16),
            "w_out": w_out[l].astype(BF16),
        }
        mod = mods[l].reshape(8, 1, 3 * d)
        y_p, (k_l, v_l, r_l, g_l) = _layer(y_p, mod, ctx_row, l, wts, None, None, b_ctx, t_ctx)
        y_s, _ = _layer(y_s, mod, lat_row, l, wts, rope,
                        (cache_k, cache_v, state_ret, state_gdn), b_lat, t_lat)
        ks.append(k_l)
        vs.append(v_l)
        rs.append(r_l)
        gs.append(g_l)
    new_k = jnp.stack(ks, axis=1).reshape(b_ctx, depth, t_ctx, N_HEADS, 2, DK_A)
    new_v = jnp.stack(vs, axis=1).reshape(b_ctx, depth, t_ctx, N_HEADS, HEAD_W)
    return (y_p.reshape(b_ctx, t_ctx, d), y_s.reshape(b_lat, t_lat, d), new_k, new_v,
            jnp.stack(rs, axis=1), jnp.stack(gs, axis=1))
```

```python
import functools
import itertools
import math

import jax
import jax.numpy as jnp
from jax import lax
from jax.experimental import pallas as pl
from jax.experimental.pallas import tpu as pltpu

F32 = jnp.float32
BF16 = jnp.bfloat16
HIGHEST = lax.Precision.HIGHEST

N_HEADS = 4
DK_A = 64
DK_B = 64
HEAD_W = 128
CONV_K = 5
GRID_W = 64
ROPE_BASE = 10000.0
EPS = 1e-6
LANES = 128
CHUNK = 128
ATTN_SUB = 256
ATTN_KEYS = 256
LOG2E = 1.4426950408889634
GDN_GROUP = 4
GDN_VMEM_BUDGET = 40 * 1024 * 1024
VMEM_LIMIT = 56 * 1024 * 1024

OFF_AQ, OFF_AK, OFF_AV, OFF_AZ = 0, 512, 1024, 1536
OFF_BQ, OFF_BK, OFF_BV, OFF_BZ = 2048, 2304, 2560, 3072
OFF_CQ, OFF_CK, OFF_CV, OFF_CZ = 3584, 4096, 4608, 5120
OFF_BA = 5632
N_MIX = 5632


def _cparams(sem):
    return pltpu.CompilerParams(dimension_semantics=sem, vmem_limit_bytes=VMEM_LIMIT)


def _sigmoid(x):
    return 1.0 / (1.0 + jnp.exp(-x))


def _silu(x):
    return x * _sigmoid(x)


def _softplus(x):
    return jnp.maximum(x, 0.0) + jnp.log1p(jnp.exp(-jnp.abs(x)))


def _dot(a, b):
    return jnp.dot(a, b, preferred_element_type=F32)


def _dot_nt(a, b):
    return lax.dot_general(a, b, (((1,), (1,)), ((), ())), preferred_element_type=F32)


def _lane_iota(shape):
    return lax.broadcasted_iota(jnp.int32, shape, len(shape) - 1)


def _row_iota(shape):
    return lax.broadcasted_iota(jnp.int32, shape, len(shape) - 2)


def _norm_halves(x):
    lo = _lane_iota(x.shape) < 64
    x2 = x * x
    s0 = jnp.sum(jnp.where(lo, x2, 0.0), axis=-1, keepdims=True)
    s1 = jnp.sum(jnp.where(lo, 0.0, x2), axis=-1, keepdims=True)
    return x * lax.rsqrt(jnp.where(lo, s0, s1) * (1.0 / 64.0) + EPS)


def _ones_column(n):
    return jnp.where(_lane_iota((n, LANES)) == 0, 1.0, 0.0).astype(BF16)


def _rope(x, cos, sin_a, sin_b):
    return (x * cos + pltpu.roll(x, LANES - 16, 1) * sin_a + pltpu.roll(x, 16, 1) * sin_b)


def _ada_kernel(cond_ref, w_ref, b_ref, o_ref):
    c = cond_ref[...]
    o_ref[0] = jnp.dot(_silu(c), w_ref[0], precision=HIGHEST,
                       preferred_element_type=F32) + b_ref[0]


def _ada_call(cond, w_ada, b_ada):
    depth, d, d3 = w_ada.shape
    tn = 1024
    return pl.pallas_call(
        _ada_kernel,
        grid=(depth, d3 // tn),
        in_specs=[pl.BlockSpec((8, d), lambda l, j: (0, 0)),
                  pl.BlockSpec((1, d, tn), lambda l, j: (l, 0, j)),
                  pl.BlockSpec((1, 1, tn), lambda l, j: (l, 0, j))],
        out_specs=pl.BlockSpec((1, 8, tn), lambda l, j: (l, 0, j)),
        out_shape=jax.ShapeDtypeStruct((depth, 8, d3), F32),
        compiler_params=_cparams(("parallel", "parallel")),
    )(cond, w_ada, b_ada.reshape(depth, 1, d3))


def _proj_kernel(x_ref, mod_ref, nw_ref, w_ref, o_ref, ba_ref, h_ref, *, d, ba_tile, ba_off):
    j = pl.program_id(1)

    @pl.when(j == 0)
    def _():
        x = x_ref[...]
        y = x * lax.rsqrt(jnp.mean(x * x, axis=-1, keepdims=True) + EPS) * nw_ref[...]
        mod = mod_ref[0]
        h_ref[...] = (y * (1.0 + mod[:, d:2 * d]) + mod[:, 0:d]).astype(BF16)

    acc = _dot(h_ref[...], w_ref[...])
    o_ref[...] = acc.astype(BF16)

    @pl.when(j == ba_tile)
    def _():
        ba_ref[...] = acc[:, ba_off:ba_off + LANES]


def _proj_call(x2, mod, mod_row, norm_w, w_pad, tm):
    n, d = x2.shape
    n_cols = w_pad.shape[1]
    tn = PROJ_TN
    return pl.pallas_call(
        functools.partial(_proj_kernel, d=d, ba_tile=OFF_BA // tn, ba_off=OFF_BA % tn),
        grid=(n // tm, n_cols // tn),
        in_specs=[pl.BlockSpec((tm, d), lambda i, j: (i, 0)),
                  pl.BlockSpec((1, 1, 3 * d), lambda i, j: (mod_row(i), 0, 0)),
                  pl.BlockSpec((1, d), lambda i, j: (0, 0)),
                  pl.BlockSpec((d, tn), lambda i, j: (0, j))],
        out_specs=[pl.BlockSpec((tm, tn), lambda i, j: (i, j)),
                   pl.BlockSpec((tm, LANES), lambda i, j: (i, 0))],
        out_shape=[jax.ShapeDtypeStruct((n, n_cols), BF16),
                   jax.ShapeDtypeStruct((n, LANES), F32)],
        scratch_shapes=[pltpu.VMEM((tm, d), BF16)],
        compiler_params=_cparams(("parallel", "arbitrary")),
    )(x2, mod, norm_w, w_pad)


def _prep_ctx_kernel(ak_ref, av_ref, bq_ref, bk_ref, wk_ref,
                     kall_ref, vall_ref, rq_ref, rk_ref, ka_ref, va_ref):
    wk = wk_ref[...]
    ones_col = _ones_column(ak_ref.shape[0])
    for h in range(N_HEADS):
        sl = slice(h * HEAD_W, (h + 1) * HEAD_W)
        kn = _norm_halves(ak_ref[:, sl].astype(F32)) * wk
        ka_ref[0, :, sl] = kn
        kall_ref[0, h] = kn.astype(BF16)
        v = av_ref[:, sl]
        va_ref[0, :, sl] = v.astype(F32)
        vall_ref[0, h, :, 0:HEAD_W] = v
        vall_ref[0, h, :, HEAD_W:2 * HEAD_W] = ones_col
    rq_ref[...] = bq_ref[...].astype(F32)
    rk_ref[...] = bk_ref[...].astype(F32) * (DK_B ** -0.5)


def _prep_lat_kernel(ak_ref, av_ref, bq_ref, bk_ref, wk_ref, cos_ref, sa_ref, sb_ref,
                     ck_ref, cv_ref, kall_ref, vall_ref, rq_ref, rk_ref, *, nt):
    t = pl.program_id(1)
    ones_col = _ones_column(ak_ref.shape[0])

    @pl.when(t < nt)
    def _():
        wk = wk_ref[...]
        cos, sa, sb = cos_ref[...], sa_ref[...], sb_ref[...]
        for h in range(N_HEADS):
            sl = slice(h * HEAD_W, (h + 1) * HEAD_W)
            kn = _norm_halves(ak_ref[:, sl].astype(F32)) * wk
            kall_ref[0, h] = _rope(kn, cos, sa, sb).astype(BF16)
            vall_ref[0, h, :, 0:HEAD_W] = av_ref[:, sl]
            vall_ref[0, h, :, HEAD_W:2 * HEAD_W] = ones_col
        for e in range(2):
            sl = slice(e * LANES, (e + 1) * LANES)
            rq_ref[:, sl] = _rope(bq_ref[:, sl].astype(F32), cos, sa, sb)
            rk_ref[:, sl] = _rope(bk_ref[:, sl].astype(F32), cos, sa, sb) * (DK_B ** -0.5)

    @pl.when(t == nt)
    def _():
        for h in range(N_HEADS):
            sl = slice(h * HEAD_W, (h + 1) * HEAD_W)
            kall_ref[0, h] = ck_ref[:, sl].astype(BF16)
            vall_ref[0, h, :, 0:HEAD_W] = cv_ref[:, sl].astype(BF16)
            vall_ref[0, h, :, HEAD_W:2 * HEAD_W] = ones_col


def _prep_call(p, b, t, wk_row, rope, cache, layer):
    tk = 256
    nt = t // tk
    n = b * t
    if rope is None:
        return pl.pallas_call(
            _prep_ctx_kernel,
            grid=(b, nt),
            in_specs=[pl.BlockSpec((tk, 512), lambda bi, ti: (bi * nt + ti, OFF_AK // 512)),
                      pl.BlockSpec((tk, 512), lambda bi, ti: (bi * nt + ti, OFF_AV // 512)),
                      pl.BlockSpec((tk, 256), lambda bi, ti: (bi * nt + ti, OFF_BQ // 256)),
                      pl.BlockSpec((tk, 256), lambda bi, ti: (bi * nt + ti, OFF_BK // 256)),
                      pl.BlockSpec((1, LANES), lambda bi, ti: (0, 0))],
            out_specs=[pl.BlockSpec((1, N_HEADS, tk, HEAD_W), lambda bi, ti: (bi, 0, ti, 0)),
                       pl.BlockSpec((1, N_HEADS, tk, 2 * HEAD_W), lambda bi, ti: (bi, 0, ti, 0)),
                       pl.BlockSpec((tk, 256), lambda bi, ti: (bi * nt + ti, 0)),
                       pl.BlockSpec((tk, 256), lambda bi, ti: (bi * nt + ti, 0)),
                       pl.BlockSpec((1, tk, 512), lambda bi, ti: (bi, ti, 0)),
                       pl.BlockSpec((1, tk, 512), lambda bi, ti: (bi, ti, 0))],
            out_shape=[jax.ShapeDtypeStruct((b, N_HEADS, t, HEAD_W), BF16),
                       jax.ShapeDtypeStruct((b, N_HEADS, t, 2 * HEAD_W), BF16),
                       jax.ShapeDtypeStruct((n, 256), F32),
                       jax.ShapeDtypeStruct((n, 256), F32),
                       jax.ShapeDtypeStruct((b, t, 512), F32),
                       jax.ShapeDtypeStruct((b, t, 512), F32)],
            compiler_params=_cparams(("parallel", "parallel")),
        )(p, p, p, p, wk_row)
    cos, sa, sb = rope
    cache_k, cache_v = cache
    past = cache_k.shape[2]
    assert past == tk
    s = t + past

    def tok(bi, ti):
        return bi * nt + jnp.minimum(ti, nt - 1)

    return pl.pallas_call(
        functools.partial(_prep_lat_kernel, nt=nt),
        grid=(b, nt + 1),
        in_specs=[pl.BlockSpec((tk, 512), lambda bi, ti: (tok(bi, ti), OFF_AK // 512)),
                  pl.BlockSpec((tk, 512), lambda bi, ti: (tok(bi, ti), OFF_AV // 512)),
                  pl.BlockSpec((tk, 256), lambda bi, ti: (tok(bi, ti), OFF_BQ // 256)),
                  pl.BlockSpec((tk, 256), lambda bi, ti: (tok(bi, ti), OFF_BK // 256)),
                  pl.BlockSpec((1, LANES), lambda bi, ti: (0, 0)),
                  pl.BlockSpec((tk, LANES), lambda bi, ti: (jnp.minimum(ti, nt - 1), 0)),
                  pl.BlockSpec((tk, LANES), lambda bi, ti: (jnp.minimum(ti, nt - 1), 0)),
                  pl.BlockSpec((tk, LANES), lambda bi, ti: (jnp.minimum(ti, nt - 1), 0)),
                  pl.BlockSpec((None, None, past, 512), lambda bi, ti: (bi, layer, 0, 0)),
                  pl.BlockSpec((None, None, past, 512), lambda bi, ti: (bi, layer, 0, 0))],
        out_specs=[pl.BlockSpec((1, N_HEADS, tk, HEAD_W), lambda bi, ti: (bi, 0, ti, 0)),
                   pl.BlockSpec((1, N_HEADS, tk, 2 * HEAD_W), lambda bi, ti: (bi, 0, ti, 0)),
                   pl.BlockSpec((tk, 256), lambda bi, ti: (tok(bi, ti), 0)),
                   pl.BlockSpec((tk, 256), lambda bi, ti: (tok(bi, ti), 0))],
        out_shape=[jax.ShapeDtypeStruct((b, N_HEADS, s, HEAD_W), BF16),
                   jax.ShapeDtypeStruct((b, N_HEADS, s, 2 * HEAD_W), BF16),
                   jax.ShapeDtypeStruct((n, 256), F32),
                   jax.ShapeDtypeStruct((n, 256), F32)],
        compiler_params=_cparams(("parallel", "arbitrary")),
    )(p, p, p, p, wk_row, cos, sa, sb, cache_k, cache_v)


def _attn_kernel(*refs, lam_init, rope):
    if rope:
        (q_ref, z_ref, k_ref, v_ref, wq_ref, dl_ref, sw_ref, cos_ref, sa_ref, sb_ref,
         o_ref, s_ref, p_ref) = refs
    else:
        q_ref, z_ref, k_ref, v_ref, wq_ref, dl_ref, sw_ref, o_ref, s_ref, p_ref = refs
    n_keys = k_ref.shape[2]
    tq = q_ref.shape[0]
    vals = v_ref[0, 0]
    lv = dl_ref[...]
    lam = (jnp.exp(jnp.sum(lv[0:1] * lv[1:2], axis=-1, keepdims=True))
           - jnp.exp(jnp.sum(lv[2:3] * lv[3:4], axis=-1, keepdims=True)) + lam_init)
    chains = [(r0, m) for r0 in range(0, tq, ATTN_SUB) for m in range(2)]
    q_scaled = {}
    row_max = {}
    den = {}
    acc = {}

    def stage_scores(r0, m):
        rows = slice(r0, r0 + ATTN_SUB)
        if r0 not in q_scaled:
            qn = _norm_halves(q_ref[rows, :].astype(F32)) * wq_ref[...]
            if rope:
                qn = _rope(qn, cos_ref[rows, :], sa_ref[rows, :], sb_ref[rows, :])
            q_scaled[r0] = qn * (DK_A ** -0.5 * LOG2E)
        qn = q_scaled[r0]
        lo = _lane_iota(qn.shape) < DK_A
        qm = jnp.where(lo if m == 0 else jnp.logical_not(lo), qn, 0.0).astype(BF16)
        mx = None
        for c0 in range(0, n_keys, ATTN_KEYS):
            s = _dot_nt(qm, k_ref[0, 0, c0:c0 + ATTN_KEYS, :])
            s_ref[m, rows, c0:c0 + ATTN_KEYS] = s
            for j in range(ATTN_KEYS // LANES):
                sj = s[:, j * LANES:(j + 1) * LANES]
                mx = sj if mx is None else jnp.maximum(mx, sj)
        row_max[r0, m] = jnp.max(mx, axis=-1, keepdims=True)

    def stage_softmax(r0, m):
        rows = slice(r0, r0 + ATTN_SUB)
        mx = row_max[r0, m]
        for j in range(n_keys // LANES):
            sl = slice(j * LANES, (j + 1) * LANES)
            p_ref[m, rows, sl] = jnp.exp2(s_ref[m, rows, sl] - mx).astype(BF16)

    def stage_values(r0, m):
        rows = slice(r0, r0 + ATTN_SUB)
        pv = _dot(p_ref[m, rows, :], vals)
        acc[r0, m] = pv[:, 0:HEAD_W]
        den[r0, m] = pv[:, HEAD_W:HEAD_W + 1]
        if m == 1:
            o = (acc[r0, 0] * (1.0 / den[r0, 0]) - lam * (acc[r0, 1] * (1.0 / den[r0, 1])))
            o = o * lax.rsqrt(jnp.mean(o * o, axis=-1, keepdims=True) + EPS) * sw_ref[...]
            o_ref[rows, :] = o * (1.0 - lam_init) * _silu(z_ref[rows, :].astype(F32))

    for step in range(len(chains) + 2):
        if step < len(chains):
            stage_scores(*chains[step])
        if 0 <= step - 1 < len(chains):
            stage_softmax(*chains[step - 1])
        if 0 <= step - 2 < len(chains):
            stage_values(*chains[step - 2])


def _attn_call(p, kall, vall, wq_row, dlam, subln_row, rope, b, t, lam_init):
    tq = min(512, t)
    s = kall.shape[2]
    nq = t // tq
    in_specs = [pl.BlockSpec((tq, HEAD_W), lambda bi, h, i: (bi * nq + i, OFF_AQ // HEAD_W + h)),
                pl.BlockSpec((tq, HEAD_W), lambda bi, h, i: (bi * nq + i, OFF_AZ // HEAD_W + h)),
                pl.BlockSpec((1, 1, s, HEAD_W), lambda bi, h, i: (bi, h, 0, 0)),
                pl.BlockSpec((1, 1, s, 2 * HEAD_W), lambda bi, h, i: (bi, h, 0, 0)),
                pl.BlockSpec((1, LANES), lambda bi, h, i: (0, 0)),
                pl.BlockSpec((4, DK_A), lambda bi, h, i: (0, 0)),
                pl.BlockSpec((1, LANES), lambda bi, h, i: (0, 0))]
    args = [p, p, kall, vall, wq_row, dlam, subln_row]
    if rope is not None:
        in_specs += [pl.BlockSpec((tq, LANES), lambda bi, h, i: (i, 0))] * 3
        args += list(rope)
    return pl.pallas_call(
        functools.partial(_attn_kernel, lam_init=lam_init, rope=rope is not None),
        grid=(b, N_HEADS, nq),
        in_specs=in_specs,
        out_specs=pl.BlockSpec((tq, HEAD_W), lambda bi, h, i: (bi * nq + i, h)),
        out_shape=jax.ShapeDtypeStruct((b * t, N_HEADS * HEAD_W), F32),
        scratch_shapes=[pltpu.VMEM((2, tq, s), F32), pltpu.VMEM((2, tq, s), BF16)],
        compiler_params=_cparams(("parallel", "parallel", "arbitrary")),
    )(*args)


def _ret_kernel(*refs, nchunk, has_state):
    if has_state:
        (q_ref, k_ref, v_ref, z_ref, d256_ref, d128_ref, nw_ref, s0_ref,
         o_ref, s_ref, dm_ref, qd_ref, kd_ref, ob_ref) = refs
        st_ref = None
    else:
        (q_ref, k_ref, v_ref, z_ref, d256_ref, d128_ref, nw_ref,
         o_ref, st_ref, s_ref, dm_ref, qd_ref, kd_ref, ob_ref) = refs
    c_len = CHUNK
    lg256 = -_softplus(-d256_ref[...])
    lg128 = -_softplus(-d128_ref[...])
    ri = _row_iota((c_len, c_len)).astype(F32)
    ci = _lane_iota((c_len, c_len)).astype(F32)
    rcol256 = _row_iota((c_len, 2 * HEAD_W)).astype(F32)
    rcol128 = _row_iota((c_len, LANES)).astype(F32)
    blockmask = ((_row_iota((2 * DK_B, 2 * HEAD_W)) < DK_B)
                 == (_lane_iota((2 * DK_B, 2 * HEAD_W)) < HEAD_W))
    for d in range(2):
        rel = (ri - ci) if d == 0 else (ci - ri)
        keep = rel >= 0
        for e in range(2):
            lg = lg256[d:d + 1, e * HEAD_W:e * HEAD_W + 1]
            dm_ref[d, e] = jnp.where(keep, jnp.exp(jnp.where(keep, rel, 0.0) * lg), 0.0)
        qpow = (rcol256 + 1.0) if d == 0 else (c_len - rcol256)
        qd_ref[d] = jnp.exp(qpow * lg256[d:d + 1])
        kpow = (c_len - 1.0 - rcol128) if d == 0 else rcol128
        kd_ref[d] = jnp.exp(kpow * lg128[d:d + 1])
        if has_state:
            s_ref[d] = jnp.zeros((2 * DK_B, 2 * HEAD_W), F32)
            for e in range(2):
                s_ref[d, e * DK_B:(e + 1) * DK_B, e * HEAD_W:(e + 1) * HEAD_W] = s0_ref[d, e]
        else:
            s_ref[d] = jnp.zeros((2 * DK_B, 2 * HEAD_W), F32)
    cdec = [jnp.exp(float(c_len) * lg256[d:d + 1]) for d in range(2)]
    lane_lo = _lane_iota((c_len, LANES)) < DK_B
    nw = nw_ref[...]

    def step(c, d):
        t0 = pl.multiple_of(c * c_len, c_len)
        q = q_ref[pl.ds(t0, c_len), :]
        k = k_ref[pl.ds(t0, c_len), :]
        vb = v_ref[pl.ds(t0, c_len), :]
        st = s_ref[d]
        inter = _dot(q.astype(BF16), st.astype(BF16)) * qd_ref[d]
        kb = k.astype(BF16)
        dst = o_ref if d == 0 else ob_ref
        for e in range(2):
            sl = slice(e * HEAD_W, (e + 1) * HEAD_W)
            qe = jnp.where(lane_lo if e == 0 else jnp.logical_not(lane_lo), q, 0.0).astype(BF16)
            sc = _dot_nt(qe, kb) * dm_ref[d, e]
            dst[pl.ds(t0, c_len), sl] = _dot(sc.astype(BF16), vb[:, sl]) + inter[:, sl]
        kdk = (k * kd_ref[d]).astype(BF16)
        kv = lax.dot_general(kdk, vb, (((0,), (0,)), ((), ())), preferred_element_type=F32)
        s_ref[d] = jnp.where(blockmask, st * cdec[d] + kv, 0.0)

    def scan_body(i, carry):
        step(i, 0)
        step(nchunk - 1 - i, 1)
        return carry

    lax.fori_loop(0, nchunk, scan_body, 0)

    def fin_body(i, carry):
        t0 = pl.multiple_of(i * c_len, c_len)
        for e in range(2):
            sl = slice(e * HEAD_W, (e + 1) * HEAD_W)
            o_e = o_ref[pl.ds(t0, c_len), sl] + ob_ref[pl.ds(t0, c_len), sl]
            o_e = o_e * lax.rsqrt(jnp.mean(o_e * o_e, axis=-1, keepdims=True) + EPS) * nw
            o_ref[pl.ds(t0, c_len), sl] = o_e * _silu(z_ref[pl.ds(t0, c_len), sl].astype(F32))
        return carry

    lax.fori_loop(0, nchunk, fin_body, 0)
    if not has_state:
        for d in range(2):
            for e in range(2):
                st_ref[d, e] = s_ref[d, e * DK_B:(e + 1) * DK_B, e * HEAD_W:(e + 1) * HEAD_W]


def _ret_call(p, rq, rk, d256, d128, nw_row, state, b, t, layer):
    nchunk = t // CHUNK
    hp = N_HEADS // 2
    in_specs = [pl.BlockSpec((t, LANES), lambda bi, h: (bi, h)),
                pl.BlockSpec((t, LANES), lambda bi, h: (bi, h)),
                pl.BlockSpec((t, 256), lambda bi, h: (bi, OFF_BV // 256 + h)),
                pl.BlockSpec((t, 256), lambda bi, h: (bi, OFF_BZ // 256 + h)),
                pl.BlockSpec((None, 2, 256), lambda bi, h: (h, 0, 0)),
                pl.BlockSpec((None, 2, LANES), lambda bi, h: (h, 0, 0)),
                pl.BlockSpec((1, LANES), lambda bi, h: (0, 0))]
    args = [rq, rk, p, p, d256, d128, nw_row]
    scratch = [pltpu.VMEM((2, 2 * DK_B, 2 * HEAD_W), F32),
               pltpu.VMEM((2, 2, CHUNK, CHUNK), F32),
               pltpu.VMEM((2, CHUNK, 2 * HEAD_W), F32),
               pltpu.VMEM((2, CHUNK, LANES), F32),
               pltpu.VMEM((t, 2 * HEAD_W), F32)]
    o_spec = pl.BlockSpec((t, 256), lambda bi, h: (bi, h))
    o_shape = jax.ShapeDtypeStruct((b * t, N_HEADS * HEAD_W), F32)
    if state is not None:
        in_specs.append(pl.BlockSpec((None, None, 2, 2, DK_B, HEAD_W),
                                     lambda bi, h: (bi, layer, 0, h, 0, 0)))
        args.append(state)
        out_specs, out_shape = o_spec, o_shape
    else:
        out_specs = [o_spec, pl.BlockSpec((None, 2, 2, DK_B, HEAD_W), lambda bi, h: (bi, 0, h, 0, 0))]
        out_shape = [o_shape, jax.ShapeDtypeStruct((b, 2, N_HEADS, DK_B, HEAD_W), F32)]
    return pl.pallas_call(
        functools.partial(_ret_kernel, nchunk=nchunk, has_state=state is not None),
        grid=(b, hp),
        in_specs=in_specs,
        out_specs=out_specs,
        out_shape=out_shape,
        scratch_shapes=scratch,
        compiler_params=_cparams(("parallel", "parallel")),
    )(*args)


def _gdn_kernel(*refs, t, nh, has_state):
    if has_state:
        (cq_ref, ck_ref, cv_ref, cz_ref, ba_ref, wq_ref, wk_ref, wv_ref, al_ref, dt_ref, nw_ref,
         s0_ref, o_ref, qs_ref, ks_ref, vs_ref, u_ref, w_ref, qk_ref, qg_ref, kdt_ref, et_ref,
         ob_ref, s_ref) = refs
        st_ref = None
    else:
        (cq_ref, ck_ref, cv_ref, cz_ref, ba_ref, wq_ref, wk_ref, wv_ref, al_ref, dt_ref, nw_ref,
         o_ref, st_ref, qs_ref, ks_ref, vs_ref, u_ref, w_ref, qk_ref, qg_ref, kdt_ref, et_ref,
         ob_ref, s_ref) = refs
    c_len = CHUNK
    nchunk = t // c_len
    head0 = pl.program_id(1) * nh
    pad = 16
    win = 256
    nwin = t // win

    def conv_window(r0, first, last):
        lo = 0 if first else pad
        hi = 0 if last else pad
        start = r0 - lo if isinstance(r0, int) else pl.multiple_of(r0 - lo, pad)
        for hh, (a, (src, cw_ref, dst)) in itertools.product(
                range(nh), enumerate(((cq_ref, wq_ref, qs_ref), (ck_ref, wk_ref, ks_ref),
                                      (cv_ref, wv_ref, vs_ref)))):
            hl = slice(hh * HEAD_W, (hh + 1) * HEAD_W)
            xw = src[pl.ds(start, win + lo + hi), hl].astype(F32)
            if first:
                xw = jnp.concatenate([jnp.zeros((pad, LANES), F32), xw], axis=0)
            if last:
                xw = jnp.concatenate([xw, jnp.zeros((pad, LANES), F32)], axis=0)
            cw = cw_ref[:, hl]
            acc = None
            for j in range(CONV_K):
                off = pad - CONV_K // 2 + j
                term = xw[off:off + win, :] * cw[j:j + 1, :]
                acc = term if acc is None else acc + term
            y = _silu(acc)
            if a < 2:
                y = y * lax.rsqrt(jnp.sum(y * y, axis=-1, keepdims=True) + EPS)
            if a == 0:
                y = y * (HEAD_W ** -0.5)
            dst[hh, pl.ds(r0, win), :] = y

    if nwin == 1:
        conv_window(0, True, True)
    else:
        conv_window(0, True, False)

        def conv_body(i, carry):
            conv_window(pl.multiple_of(i * win, win), False, False)
            return carry

        lax.fori_loop(1, nwin - 1, conv_body, 0)
        conv_window((nwin - 1) * win, False, True)

    ri = _row_iota((c_len, c_len))
    ci = _lane_iota((c_len, c_len))
    lane = _lane_iota((c_len, LANES))
    alog = al_ref[...]
    dtb = dt_ref[...]
    tri16 = jnp.where(ri >= ci, 1.0, 0.0).astype(BF16)
    g_lanes = jnp.logical_and(lane >= 2 * N_HEADS, lane < 4 * N_HEADS)

    def lanes3(tgt):
        return jnp.logical_or(lane == tgt, jnp.logical_or(lane == tgt + 16, lane == tgt + 32))

    def chunk_chains(c):
        t0 = pl.multiple_of(c * c_len, c_len)
        ba = ba_ref[pl.ds(t0, c_len), :]
        beta_all = _sigmoid(ba)
        g_all = jnp.where(g_lanes, -jnp.exp(alog) * _softplus(ba + dtb), 0.0)
        g_hi = g_all.astype(BF16).astype(F32)
        r1 = g_all - g_hi
        g_mid = r1.astype(BF16).astype(F32)
        g_lo = (r1 - g_mid).astype(BF16).astype(F32)
        parts = (g_hi + pltpu.roll(g_mid, 16, 1) + pltpu.roll(g_lo, 32, 1)).astype(BF16)
        pre = _dot(tri16, parts)
        chains = []
        for hh in range(nh):
            h = head0 + hh
            tgt_f = 2 * N_HEADS + h
            tgt_b = 3 * N_HEADS + h
            gc_f = jnp.sum(jnp.where(lanes3(tgt_f), pre, 0.0), axis=-1, keepdims=True)
            pre_b = jnp.sum(jnp.where(lanes3(tgt_b), pre, 0.0), axis=-1, keepdims=True)
            g_b = jnp.sum(jnp.where(lane == tgt_b, g_all, 0.0), axis=-1, keepdims=True)
            gc_b = pre_b[c_len - 1:c_len, :] - pre_b + g_b
            beta_f = jnp.sum(jnp.where(lane == h, beta_all, 0.0), axis=-1, keepdims=True)
            beta_b = jnp.sum(jnp.where(lane == N_HEADS + h, beta_all, 0.0), axis=-1,
                             keepdims=True)
            rows = jnp.where(lane == 0, gc_f, jnp.where(lane == 1, gc_b, 0.0)).T

            q = qs_ref[hh, pl.ds(t0, c_len), :]
            k = ks_ref[hh, pl.ds(t0, c_len), :]
            v = vs_ref[hh, pl.ds(t0, c_len), :]
            k16 = k.astype(BF16)
            kkqk = _dot_nt(jnp.concatenate([k16, q.astype(BF16)], axis=0), k16)
            kk = kkqk[0:c_len, :]
            qk_raw = kkqk[c_len:2 * c_len, :]
            for d, (gc, beta) in enumerate(((gc_f, beta_f), (gc_b, beta_b))):
                gr = rows[d:d + 1, :]
                incl = (ri >= ci) if d == 0 else (ci >= ri)
                strict = (ri > ci) if d == 0 else (ci > ri)
                ex = jnp.exp(jnp.where(incl, gc - gr, 0.0))
                a_mat = jnp.where(strict, ex, 0.0) * kk * beta
                egc = jnp.exp(gc)
                rhs = jnp.concatenate([v * beta, k * (beta * egc)], axis=1)
                qk_ref[hh, d, pl.ds(t0, c_len), :] = (jnp.where(incl, ex, 0.0)
                                                      * qk_raw).astype(BF16)
                qg_ref[hh, d, pl.ds(t0, c_len), :] = (q * egc).astype(BF16)
                g_tot = gc[c_len - 1:c_len, :] if d == 0 else gc[0:1, :]
                kdt_ref[hh, d, pl.ds(t0, c_len), :] = (k * jnp.exp(g_tot - gc)).T.astype(BF16)
                et_ref[hh, d, c] = jnp.broadcast_to(jnp.exp(g_tot), (8, LANES))
                chains.append((a_mat, rhs, (hh, d, t0)))
        return chains

    group = min(GDN_GROUP, nchunk)
    assert nchunk % group == 0

    def prep_body(i, carry):
        chains = []
        for j in range(group):
            chains += chunk_chains(i * group + j)
        n_mats = [-jnp.where((ri >> 1) == (ci >> 1), a_mat, 0.0) for a_mat, _, _ in chains]
        sh = 1
        while (1 << sh) < c_len:
            off = jnp.logical_and((ri >> (sh + 1)) == (ci >> (sh + 1)), (ri >> sh) != (ci >> sh))
            l_mats = [jnp.where(off, a_mat, 0.0) for a_mat, _, _ in chains]
            x_mats = [l + _dot(l.astype(BF16), n.astype(BF16)) for l, n in zip(l_mats, n_mats)]
            n_mats = [n - x - _dot(n.astype(BF16), x.astype(BF16)) for n, x in zip(n_mats, x_mats)]
            sh += 1
        for n_mat, (_, rhs, (hh, d, t0)) in zip(n_mats, chains):
            uw = rhs + _dot(n_mat.astype(BF16), rhs.astype(BF16))
            u_ref[hh, d, pl.ds(t0, c_len), :] = uw[:, 0:HEAD_W]
            w_ref[hh, d, pl.ds(t0, c_len), :] = uw[:, HEAD_W:2 * HEAD_W].astype(BF16)
        return carry

    lax.fori_loop(0, nchunk // group, prep_body, 0)

    for hh, d in itertools.product(range(nh), range(2)):
        s_ref[hh, d] = s0_ref[d, hh] if has_state else jnp.zeros((HEAD_W, HEAD_W), F32)

    def scan_body(i, carry):
        for hh, (d, c) in itertools.product(range(nh), ((0, i), (1, nchunk - 1 - i))):
            t0 = pl.multiple_of(c * c_len, c_len)
            hl = slice(hh * HEAD_W, (hh + 1) * HEAD_W)
            st = s_ref[hh, d]
            st16 = st.astype(BF16)
            v_new = (u_ref[hh, d, pl.ds(t0, c_len), :]
                     - _dot(w_ref[hh, d, pl.ds(t0, c_len), :], st16))
            vn16 = v_new.astype(BF16)
            o = (_dot(qg_ref[hh, d, pl.ds(t0, c_len), :], st16)
                 + _dot(qk_ref[hh, d, pl.ds(t0, c_len), :], vn16))
            s_ref[hh, d] = (st * et_ref[hh, d, c][0:1, :]
                            + _dot(kdt_ref[hh, d, pl.ds(t0, c_len), :], vn16))
            if d == 0:
                o_ref[pl.ds(t0, c_len), hl] = o
            else:
                ob_ref[pl.ds(t0, c_len), hl] = o
        return carry

    lax.fori_loop(0, nchunk, scan_body, 0)

    def fin_body(i, carry):
        r0 = pl.multiple_of(i * win, win)
        for hh in range(nh):
            hl = slice(hh * HEAD_W, (hh + 1) * HEAD_W)
            o = o_ref[pl.ds(r0, win), hl] + ob_ref[pl.ds(r0, win), hl]
            o = o * lax.rsqrt(jnp.mean(o * o, axis=-1, keepdims=True) + EPS) * nw_ref[...]
            o_ref[pl.ds(r0, win), hl] = o * _silu(cz_ref[pl.ds(r0, win), hl].astype(F32))
        return carry

    lax.fori_loop(0, nwin, fin_body, 0)
    if not has_state:
        for hh, d in itertools.product(range(nh), range(2)):
            st_ref[d, hh] = s_ref[hh, d]


def _gdn_heads_per_program(t):
    per_head = t * LANES * (4 * 2 * 2 + 4 * 2 + 3 * 4 + 2 * 4 + 4 * 2 * 2 + 4)
    nh = N_HEADS
    while nh > 1 and nh * per_head > GDN_VMEM_BUDGET:
        nh //= 2
    return nh


def _gdn_call(p, p_ba, conv_w, al_row, dt_row, nw_row, state, b, t, layer):
    nh = _gdn_heads_per_program(t)
    wd = nh * HEAD_W

    def col(off):
        return lambda bi, h: (bi, off // wd + h)

    in_specs = [pl.BlockSpec((t, wd), col(OFF_CQ)),
                pl.BlockSpec((t, wd), col(OFF_CK)),
                pl.BlockSpec((t, wd), col(OFF_CV)),
                pl.BlockSpec((t, wd), col(OFF_CZ)),
                pl.BlockSpec((t, LANES), lambda bi, h: (bi, 0)),
                pl.BlockSpec((CONV_K, wd), lambda bi, h: (0, h)),
                pl.BlockSpec((CONV_K, wd), lambda bi, h: (0, N_HEADS // nh + h)),
                pl.BlockSpec((CONV_K, wd), lambda bi, h: (0, 2 * (N_HEADS // nh) + h)),
                pl.BlockSpec((1, LANES), lambda bi, h: (0, 0)),
                pl.BlockSpec((1, LANES), lambda bi, h: (0, 0)),
                pl.BlockSpec((1, LANES), lambda bi, h: (0, 0))]
    args = [p, p, p, p, p_ba, conv_w, conv_w, conv_w, al_row, dt_row, nw_row]
    scratch = [pltpu.VMEM((nh, t, LANES), F32),
               pltpu.VMEM((nh, t, LANES), F32),
               pltpu.VMEM((nh, t, LANES), F32),
               pltpu.VMEM((nh, 2, t, LANES), F32),
               pltpu.VMEM((nh, 2, t, LANES), BF16),
               pltpu.VMEM((nh, 2, t, LANES), BF16),
               pltpu.VMEM((nh, 2, t, LANES), BF16),
               pltpu.VMEM((nh, 2, t, LANES), BF16),
               pltpu.VMEM((nh, 2, t // CHUNK, 8, LANES), F32),
               pltpu.VMEM((t, wd), F32),
               pltpu.VMEM((nh, 2, HEAD_W, HEAD_W), F32)]
    o_spec = pl.BlockSpec((t, wd), lambda bi, h: (bi, h))
    o_shape = jax.ShapeDtypeStruct((b * t, N_HEADS * HEAD_W), F32)
    if state is not None:
        in_specs.append(pl.BlockSpec((None, None, 2, nh, HEAD_W, HEAD_W),
                                     lambda bi, h: (bi, layer, 0, h, 0, 0)))
        args.append(state)
        out_specs, out_shape = o_spec, o_shape
    else:
        out_specs = [o_spec, pl.BlockSpec((None, 2, nh, HEAD_W, HEAD_W),
                                          lambda bi, h: (bi, 0, h, 0, 0))]
        out_shape = [o_shape, jax.ShapeDtypeStruct((b, 2, N_HEADS, HEAD_W, HEAD_W), F32)]
    return pl.pallas_call(
        functools.partial(_gdn_kernel, t=t, nh=nh, has_state=state is not None),
        grid=(b, N_HEADS // nh),
        in_specs=in_specs,
        out_specs=out_specs,
        out_shape=out_shape,
        scratch_shapes=scratch,
        compiler_params=_cparams(("parallel", "parallel")),
    )(*args)


def _out_kernel(ya_ref, yb_ref, yc_ref, mg_ref, x_ref, mod_ref, wb_ref, wo_ref, o_ref, *, d):
    ya = _dot(ya_ref[...].astype(BF16), wb_ref[0])
    yb = _dot(yb_ref[...].astype(BF16), wb_ref[1])
    yc = _dot(yc_ref[...].astype(BF16), wb_ref[2])
    y = (_sigmoid(mg_ref[:, 0:d].astype(F32)) * ya + _sigmoid(mg_ref[:, d:2 * d].astype(F32)) * yb
         + _sigmoid(mg_ref[:, 2 * d:3 * d].astype(F32)) * yc)
    out = _dot(y.astype(BF16), wo_ref[...])
    gate = mod_ref[0][:, 2 * d:3 * d]
    o_ref[...] = x_ref[...] + gate * out


def _out_call(ya, yb, yc, p, x2, mod, mod_row, wb, wo, mg_block, tm):
    n, d = x2.shape
    w_br = ya.shape[1]
    return pl.pallas_call(
        functools.partial(_out_kernel, d=d),
        grid=(n // tm,),
        in_specs=[pl.BlockSpec((tm, w_br), lambda i: (i, 0)),
                  pl.BlockSpec((tm, w_br), lambda i: (i, 0)),
                  pl.BlockSpec((tm, w_br), lambda i: (i, 0)),
                  pl.BlockSpec((tm, 3 * d), lambda i: (i, mg_block)),
                  pl.BlockSpec((tm, d), lambda i: (i, 0)),
                  pl.BlockSpec((1, 1, 3 * d), lambda i: (mod_row(i), 0, 0)),
                  pl.BlockSpec((3, w_br, d), lambda i: (0, 0, 0)),
                  pl.BlockSpec((d, d), lambda i: (0, 0))],
        out_specs=pl.BlockSpec((tm, d), lambda i: (i, 0)),
        out_shape=jax.ShapeDtypeStruct((n, d), F32),
        compiler_params=_cparams(("parallel",)),
    )(ya, yb, yc, p, x2, mod, wb, wo)


def _rope_tables(n_tokens, dtype):
    n_rows = n_tokens // GRID_W
    row = jnp.repeat(jnp.arange(n_rows, dtype=jnp.float32), GRID_W)
    col = jnp.tile(jnp.arange(GRID_W, dtype=jnp.float32), n_rows)
    n_freq = DK_A // 4
    inv = 1.0 / (ROPE_BASE ** (jnp.arange(n_freq, dtype=jnp.float32) / n_freq))
    ar = row[:, None] * inv
    ac = col[:, None] * inv
    ang = jnp.concatenate([ar, ar, ac, ac], axis=-1)
    cos = jnp.tile(jnp.cos(ang).astype(dtype), (1, 2))
    sin = jnp.tile(jnp.sin(ang).astype(dtype), (1, 2))
    first = (jnp.arange(LANES) % 32) < 16
    sin_a = jnp.where(first, -sin, 0.0)
    sin_b = jnp.where(first, 0.0, sin)
    return cos, sin_a, sin_b


PROJ_TM = 1024
PROJ_TN = 1536
OUT_TM = 256


def _layer(x2, mod, mod_row_for, layer, wts, rope, ctx, b, t):
    p, p_ba = _proj_call(x2, mod, mod_row_for(PROJ_TM), wts["norm_w"], wts["w_in"], PROJ_TM)
    cache = None if ctx is None else (ctx[0], ctx[1])
    prep = _prep_call(p, b, t, wts["wk_row"], rope, cache, layer)
    kall, vall, rq, rk = prep[:4]
    lam_init = 0.8 - 0.6 * math.exp(-0.3 * layer)
    ya = _attn_call(p, kall, vall, wts["wq_row"], wts["diff_lambda"], wts["subln_row"], rope,
                    b, t, lam_init)
    ret = _ret_call(p, rq, rk, wts["d256"], wts["d128"], wts["ret_norm_row"],
                    None if ctx is None else ctx[2], b, t, layer)
    gdn = _gdn_call(p, p_ba, wts["conv_w"], wts["al_row"], wts["dt_row"], wts["gdn_norm_row"],
                    None if ctx is None else ctx[3], b, t, layer)
    if ctx is None:
        yb, s_ret = ret
        yc, s_gdn = gdn
        extras = (prep[4], prep[5], s_ret, s_gdn)
    else:
        yb, yc = ret, gdn
        extras = None
    mg_block = wts["mg_off"] // (3 * x2.shape[1])
    x2 = _out_call(ya, yb, yc, p, x2, mod, mod_row_for(OUT_TM), wts["w_branch"], wts["w_out"],
                   mg_block, OUT_TM)
    return x2, extras


def kernel(x_prompt, x_sample, cache_attn_k, cache_attn_v, state_ret, state_gdn, c, c_ctx,
           norm_w, w_ada, b_ada, w_in, qk_norm_w, diff_lambda, subln_w, ret_decay, ret_norm_w,
           conv_w, gdn_a_log, gdn_dt_bias, gdn_norm_w, w_branch, w_out):
    b_ctx, t_ctx, d = x_prompt.shape
    b_lat, t_lat, _ = x_sample.shape
    depth = w_in.shape[0]
    past = cache_attn_k.shape[2]
    assert b_lat <= 4 and d % LANES == 0

    cond = jnp.zeros((8, d), F32).at[:b_lat].set(c).at[4].set(c_ctx)
    mods = _ada_call(cond, w_ada, b_ada)

    mg_off = -(-(OFF_BA + LANES) // (3 * d)) * (3 * d)
    n_cols = mg_off + 3 * d
    n_cols = -(-n_cols // 1536) * 1536
    n_ba = N_MIX + 4 * N_HEADS
    w_pad = jnp.concatenate(
        [w_in[:, :, :n_ba].astype(BF16), jnp.zeros((depth, d, mg_off - n_ba), BF16),
         w_in[:, :, n_ba:].astype(BF16), jnp.zeros((depth, d, n_cols - mg_off - 3 * d), BF16)],
        axis=2)

    rope = _rope_tables(t_lat, x_sample.dtype)
    cache_k = cache_attn_k.reshape(b_lat, depth, past, N_HEADS * 2 * DK_A)
    cache_v = cache_attn_v.reshape(b_lat, depth, past, N_HEADS * HEAD_W)

    lanes16 = jnp.zeros((depth, LANES), F32)
    al_rows = lanes16.at[:, 8:16].set(gdn_a_log.reshape(depth, 8))
    dt_rows = lanes16.at[:, 8:16].set(gdn_dt_bias.reshape(depth, 8))
    dec = ret_decay.reshape(depth, 2, 2, 2)
    dec = jnp.transpose(dec, (0, 2, 1, 3))
    d256 = jnp.repeat(dec, HEAD_W, axis=-1)
    d128 = jnp.repeat(dec, DK_B, axis=-1)

    y_p = x_prompt.reshape(b_ctx * t_ctx, d)
    y_s = x_sample.reshape(b_lat * t_lat, d)
    assert t_lat % PROJ_TM == 0 and (b_ctx * t_ctx) % PROJ_TM == 0

    def ctx_row(tm):
        return lambda i: 4

    def lat_row(tm):
        return lambda i: i // (t_lat // tm)

    ks, vs, rs, gs = [], [], [], []
    for l in range(depth):
        wts = {
            "norm_w": norm_w[l].reshape(1, d),
            "w_in": w_pad[l],
            "mg_off": mg_off,
            "wq_row": jnp.tile(qk_norm_w[l, 0], 2).reshape(1, LANES),
            "wk_row": jnp.tile(qk_norm_w[l, 1], 2).reshape(1, LANES),
            "diff_lambda": diff_lambda[l],
            "subln_row": subln_w[l].reshape(1, LANES),
            "d256": d256[l], "d128": d128[l],
            "ret_norm_row": ret_norm_w[l].reshape(1, LANES),
            "conv_w": conv_w[l],
            "al_row": al_rows[l].reshape(1, LANES),
            "dt_row": dt_rows[l].reshape(1, LANES),
            "gdn_norm_row": gdn_norm_w[l].reshape(1, LANES),
            "w_branch": w_branch[l].astype(BF16),
            "w_out": w_out[l].astype(BF16),
        }
        mod = mods[l].reshape(8, 1, 3 * d)
        y_p, (k_l, v_l, r_l, g_l) = _layer(y_p, mod, ctx_row, l, wts, None, None, b_ctx, t_ctx)
        y_s, _ = _layer(y_s, mod, lat_row, l, wts, rope,
                        (cache_k, cache_v, state_ret, state_gdn), b_lat, t_lat)
        ks.append(k_l)
        vs.append(v_l)
        rs.append(r_l)
        gs.append(g_l)
    new_k = jnp.stack(ks, axis=1).reshape(b_ctx, depth, t_ctx, N_HEADS, 2, DK_A)
    new_v = jnp.stack(vs, axis=1).reshape(b_ctx, depth, t_ctx, N_HEADS, HEAD_W)
    return (y_p.reshape(b_ctx, t_ctx, d), y_s.reshape(b_lat, t_lat, d), new_k, new_v,
            jnp.stack(rs, axis=1), jnp.stack(gs, axis=1))
```

```python
import functools
import itertools
import math

import jax
import jax.numpy as jnp
from jax import lax
from jax.experimental import pallas as pl
from jax.experimental.pallas import tpu as pltpu

F32 = jnp.float32
BF16 = jnp.bfloat16
HIGHEST = lax.Precision.HIGHEST

N_HEADS = 4
DK_A = 64
DK_B = 64
HEAD_W = 128
CONV_K = 5
GRID_W = 64
ROPE_BASE = 10000.0
EPS = 1e-6
LANES = 128
CHUNK = 128
ATTN_SUB = 256
ATTN_KEYS = 256
ATTN_TQ = 1024
ATTN_SLOTS = 4
VT_ROWS = HEAD_W + 16
LOG2E = 1.4426950408889634
GDN_GROUP = 4
GDN_VMEM_BUDGET = 40 * 1024 * 1024
VMEM_LIMIT = 56 * 1024 * 1024

OFF_AQ, OFF_AK, OFF_AV, OFF_AZ = 0, 512, 1024, 1536
OFF_BQ, OFF_BK, OFF_BV, OFF_BZ = 2048, 2304, 2560, 3072
OFF_CQ, OFF_CK, OFF_CV, OFF_CZ = 3584, 4096, 4608, 5120
OFF_BA = 5632
N_MIX = 5632


def _cparams(sem):
    return pltpu.CompilerParams(dimension_semantics=sem, vmem_limit_bytes=VMEM_LIMIT)


def _sigmoid(x):
    return 1.0 / (1.0 + jnp.exp(-x))


def _silu(x):
    return x * _sigmoid(x)


def _softplus(x):
    return jnp.maximum(x, 0.0) + jnp.log1p(jnp.exp(-jnp.abs(x)))


def _dot(a, b):
    return jnp.dot(a, b, preferred_element_type=F32)


def _dot_nt(a, b):
    return lax.dot_general(a, b, (((1,), (1,)), ((), ())), preferred_element_type=F32)


def _lane_iota(shape):
    return lax.broadcasted_iota(jnp.int32, shape, len(shape) - 1)


def _row_iota(shape):
    return lax.broadcasted_iota(jnp.int32, shape, len(shape) - 2)


def _norm_halves(x):
    lo = _lane_iota(x.shape) < 64
    x2 = x * x
    s0 = jnp.sum(jnp.where(lo, x2, 0.0), axis=-1, keepdims=True)
    s1 = jnp.sum(jnp.where(lo, 0.0, x2), axis=-1, keepdims=True)
    return x * lax.rsqrt(jnp.where(lo, s0, s1) * (1.0 / 64.0) + EPS)


def _store_values_t(vall_ref, h, v):
    vall_ref[0, h, 0:HEAD_W, :] = v.astype(F32).T.astype(BF16)
    vall_ref[0, h, HEAD_W:VT_ROWS, :] = jnp.ones((VT_ROWS - HEAD_W, v.shape[0]), BF16)


def _rope(x, cos, sin_a, sin_b):
    return (x * cos + pltpu.roll(x, LANES - 16, 1) * sin_a + pltpu.roll(x, 16, 1) * sin_b)


def _ada_kernel(cond_ref, w_ref, b_ref, o_ref):
    c = cond_ref[...]
    o_ref[0] = jnp.dot(_silu(c), w_ref[0], precision=HIGHEST,
                       preferred_element_type=F32) + b_ref[0]


def _ada_call(cond, w_ada, b_ada):
    depth, d, d3 = w_ada.shape
    tn = 1024
    return pl.pallas_call(
        _ada_kernel,
        grid=(depth, d3 // tn),
        in_specs=[pl.BlockSpec((8, d), lambda l, j: (0, 0)),
                  pl.BlockSpec((1, d, tn), lambda l, j: (l, 0, j)),
                  pl.BlockSpec((1, 1, tn), lambda l, j: (l, 0, j))],
        out_specs=pl.BlockSpec((1, 8, tn), lambda l, j: (l, 0, j)),
        out_shape=jax.ShapeDtypeStruct((depth, 8, d3), F32),
        compiler_params=_cparams(("parallel", "parallel")),
    )(cond, w_ada, b_ada.reshape(depth, 1, d3))


def _proj_kernel(x_ref, mod_ref, nw_ref, w_ref, o_ref, ba_ref, h_ref, *, d, ba_tile, ba_off):
    j = pl.program_id(1)

    @pl.when(j == 0)
    def _():
        x = x_ref[...]
        y = x * lax.rsqrt(jnp.mean(x * x, axis=-1, keepdims=True) + EPS) * nw_ref[...]
        mod = mod_ref[0]
        h_ref[...] = (y * (1.0 + mod[:, d:2 * d]) + mod[:, 0:d]).astype(BF16)

    acc = _dot(h_ref[...], w_ref[...])
    o_ref[...] = acc.astype(BF16)

    @pl.when(j == ba_tile)
    def _():
        ba_ref[...] = acc[:, ba_off:ba_off + LANES]


def _proj_call(x2, mod, mod_row, norm_w, w_pad, tm):
    n, d = x2.shape
    n_cols = w_pad.shape[1]
    tn = PROJ_TN
    return pl.pallas_call(
        functools.partial(_proj_kernel, d=d, ba_tile=OFF_BA // tn, ba_off=OFF_BA % tn),
        grid=(n // tm, n_cols // tn),
        in_specs=[pl.BlockSpec((tm, d), lambda i, j: (i, 0)),
                  pl.BlockSpec((1, 1, 3 * d), lambda i, j: (mod_row(i), 0, 0)),
                  pl.BlockSpec((1, d), lambda i, j: (0, 0)),
                  pl.BlockSpec((d, tn), lambda i, j: (0, j))],
        out_specs=[pl.BlockSpec((tm, tn), lambda i, j: (i, j)),
                   pl.BlockSpec((tm, LANES), lambda i, j: (i, 0))],
        out_shape=[jax.ShapeDtypeStruct((n, n_cols), BF16),
                   jax.ShapeDtypeStruct((n, LANES), F32)],
        scratch_shapes=[pltpu.VMEM((tm, d), BF16)],
        compiler_params=_cparams(("parallel", "arbitrary")),
    )(x2, mod, norm_w, w_pad)


def _prep_ctx_kernel(ak_ref, av_ref, bq_ref, bk_ref, wk_ref,
                     kall_ref, vall_ref, rq_ref, rk_ref, ka_ref, va_ref):
    wk = wk_ref[...]
    for h in range(N_HEADS):
        sl = slice(h * HEAD_W, (h + 1) * HEAD_W)
        kn = _norm_halves(ak_ref[:, sl].astype(F32)) * wk
        ka_ref[0, :, sl] = kn
        kall_ref[0, h] = kn.astype(BF16)
        v = av_ref[:, sl]
        va_ref[0, :, sl] = v.astype(F32)
        _store_values_t(vall_ref, h, v)
    rq_ref[...] = bq_ref[...].astype(F32)
    rk_ref[...] = bk_ref[...].astype(F32) * (DK_B ** -0.5)


def _prep_lat_kernel(ak_ref, av_ref, bq_ref, bk_ref, wk_ref, cos_ref, sa_ref, sb_ref,
                     ck_ref, cv_ref, kall_ref, vall_ref, rq_ref, rk_ref, *, nt):
    t = pl.program_id(1)

    @pl.when(t < nt)
    def _():
        wk = wk_ref[...]
        cos, sa, sb = cos_ref[...], sa_ref[...], sb_ref[...]
        for h in range(N_HEADS):
            sl = slice(h * HEAD_W, (h + 1) * HEAD_W)
            kn = _norm_halves(ak_ref[:, sl].astype(F32)) * wk
            kall_ref[0, h] = _rope(kn, cos, sa, sb).astype(BF16)
            _store_values_t(vall_ref, h, av_ref[:, sl])
        for e in range(2):
            sl = slice(e * LANES, (e + 1) * LANES)
            rq_ref[:, sl] = _rope(bq_ref[:, sl].astype(F32), cos, sa, sb)
            rk_ref[:, sl] = _rope(bk_ref[:, sl].astype(F32), cos, sa, sb) * (DK_B ** -0.5)

    @pl.when(t == nt)
    def _():
        for h in range(N_HEADS):
            sl = slice(h * HEAD_W, (h + 1) * HEAD_W)
            kall_ref[0, h] = ck_ref[:, sl].astype(BF16)
            _store_values_t(vall_ref, h, cv_ref[:, sl])


def _prep_call(p, b, t, wk_row, rope, cache, layer):
    tk = 256
    nt = t // tk
    n = b * t
    if rope is None:
        return pl.pallas_call(
            _prep_ctx_kernel,
            grid=(b, nt),
            in_specs=[pl.BlockSpec((tk, 512), lambda bi, ti: (bi * nt + ti, OFF_AK // 512)),
                      pl.BlockSpec((tk, 512), lambda bi, ti: (bi * nt + ti, OFF_AV // 512)),
                      pl.BlockSpec((tk, 256), lambda bi, ti: (bi * nt + ti, OFF_BQ // 256)),
                      pl.BlockSpec((tk, 256), lambda bi, ti: (bi * nt + ti, OFF_BK // 256)),
                      pl.BlockSpec((1, LANES), lambda bi, ti: (0, 0))],
            out_specs=[pl.BlockSpec((1, N_HEADS, tk, HEAD_W), lambda bi, ti: (bi, 0, ti, 0)),
                       pl.BlockSpec((1, N_HEADS, VT_ROWS, tk), lambda bi, ti: (bi, 0, 0, ti)),
                       pl.BlockSpec((tk, 256), lambda bi, ti: (bi * nt + ti, 0)),
                       pl.BlockSpec((tk, 256), lambda bi, ti: (bi * nt + ti, 0)),
                       pl.BlockSpec((1, tk, 512), lambda bi, ti: (bi, ti, 0)),
                       pl.BlockSpec((1, tk, 512), lambda bi, ti: (bi, ti, 0))],
            out_shape=[jax.ShapeDtypeStruct((b, N_HEADS, t, HEAD_W), BF16),
                       jax.ShapeDtypeStruct((b, N_HEADS, VT_ROWS, t), BF16),
                       jax.ShapeDtypeStruct((n, 256), F32),
                       jax.ShapeDtypeStruct((n, 256), F32),
                       jax.ShapeDtypeStruct((b, t, 512), F32),
                       jax.ShapeDtypeStruct((b, t, 512), F32)],
            compiler_params=_cparams(("parallel", "parallel")),
        )(p, p, p, p, wk_row)
    cos, sa, sb = rope
    cache_k, cache_v = cache
    past = cache_k.shape[2]
    assert past == tk
    s = t + past

    def tok(bi, ti):
        return bi * nt + jnp.minimum(ti, nt - 1)

    return pl.pallas_call(
        functools.partial(_prep_lat_kernel, nt=nt),
        grid=(b, nt + 1),
        in_specs=[pl.BlockSpec((tk, 512), lambda bi, ti: (tok(bi, ti), OFF_AK // 512)),
                  pl.BlockSpec((tk, 512), lambda bi, ti: (tok(bi, ti), OFF_AV // 512)),
                  pl.BlockSpec((tk, 256), lambda bi, ti: (tok(bi, ti), OFF_BQ // 256)),
                  pl.BlockSpec((tk, 256), lambda bi, ti: (tok(bi, ti), OFF_BK // 256)),
                  pl.BlockSpec((1, LANES), lambda bi, ti: (0, 0)),
                  pl.BlockSpec((tk, LANES), lambda bi, ti: (jnp.minimum(ti, nt - 1), 0)),
                  pl.BlockSpec((tk, LANES), lambda bi, ti: (jnp.minimum(ti, nt - 1), 0)),
                  pl.BlockSpec((tk, LANES), lambda bi, ti: (jnp.minimum(ti, nt - 1), 0)),
                  pl.BlockSpec((None, None, past, 512), lambda bi, ti: (bi, layer, 0, 0)),
                  pl.BlockSpec((None, None, past, 512), lambda bi, ti: (bi, layer, 0, 0))],
        out_specs=[pl.BlockSpec((1, N_HEADS, tk, HEAD_W), lambda bi, ti: (bi, 0, ti, 0)),
                   pl.BlockSpec((1, N_HEADS, VT_ROWS, tk), lambda bi, ti: (bi, 0, 0, ti)),
                   pl.BlockSpec((tk, 256), lambda bi, ti: (tok(bi, ti), 0)),
                   pl.BlockSpec((tk, 256), lambda bi, ti: (tok(bi, ti), 0))],
        out_shape=[jax.ShapeDtypeStruct((b, N_HEADS, s, HEAD_W), BF16),
                   jax.ShapeDtypeStruct((b, N_HEADS, VT_ROWS, s), BF16),
                   jax.ShapeDtypeStruct((n, 256), F32),
                   jax.ShapeDtypeStruct((n, 256), F32)],
        compiler_params=_cparams(("parallel", "arbitrary")),
    )(p, p, p, p, wk_row, cos, sa, sb, cache_k, cache_v)


def _attn_kernel(*refs, lam_init, rope):
    if rope:
        (q_ref, z_ref, k_ref, v_ref, wq_ref, dl_ref, sw_ref, cos_ref, sa_ref, sb_ref,
         o_ref, s_ref, p_ref) = refs
    else:
        q_ref, z_ref, k_ref, v_ref, wq_ref, dl_ref, sw_ref, o_ref, s_ref, p_ref = refs
    n_keys = k_ref.shape[2]
    tq = q_ref.shape[0]
    lv = dl_ref[...]
    lam = (jnp.exp(jnp.sum(lv[0:1] * lv[1:2], axis=-1, keepdims=True))
           - jnp.exp(jnp.sum(lv[2:3] * lv[3:4], axis=-1, keepdims=True)) + lam_init)
    chains = [(r0, m) for r0 in range(0, tq, ATTN_SUB) for m in range(2)]
    slot = {chain: i % ATTN_SLOTS for i, chain in enumerate(chains)}
    q_t = {}
    col_max = {}
    den = {}
    acc = {}

    def stage_scores(r0, m):
        rows = slice(r0, r0 + ATTN_SUB)
        if r0 not in q_t:
            qn = _norm_halves(q_ref[rows, :].astype(F32)) * wq_ref[...]
            if rope:
                qn = _rope(qn, cos_ref[rows, :], sa_ref[rows, :], sb_ref[rows, :])
            q_t[r0] = (qn * (DK_A ** -0.5 * LOG2E)).T
        qt = q_t[r0]
        lo = _row_iota(qt.shape) < DK_A
        qm = jnp.where(lo if m == 0 else jnp.logical_not(lo), qt, 0.0).astype(BF16)
        for c0 in range(0, n_keys, ATTN_KEYS):
            s_ref[slot[r0, m], c0:c0 + ATTN_KEYS, :] = _dot(k_ref[0, 0, c0:c0 + ATTN_KEYS, :], qm)

    def stage_softmax(r0, m):
        rows = slice(r0, r0 + ATTN_SUB)
        sl = slot[r0, m]
        mx = s_ref[sl, 0:8, :]
        for j in range(1, n_keys // 8):
            mx = jnp.maximum(mx, s_ref[sl, j * 8:(j + 1) * 8, :])
        mx = jnp.max(mx, axis=0, keepdims=True)
        for c0 in range(0, n_keys, ATTN_KEYS):
            ks = slice(c0, c0 + ATTN_KEYS)
            p_ref[sl, ks, :] = jnp.exp2(s_ref[sl, ks, :] - mx).astype(BF16)

    def stage_values(r0, m):
        rows = slice(r0, r0 + ATTN_SUB)
        pv = _dot(v_ref[0, 0], p_ref[slot[r0, m]])
        acc[r0, m] = pv[0:HEAD_W, :]
        den[r0, m] = pv[HEAD_W:HEAD_W + 1, :]
        if m == 1:
            o = (acc[r0, 0] * (1.0 / den[r0, 0]) - lam * (acc[r0, 1] * (1.0 / den[r0, 1])))
            o = o.T
            o = o * lax.rsqrt(jnp.mean(o * o, axis=-1, keepdims=True) + EPS) * sw_ref[...]
            o_ref[rows, :] = o * (1.0 - lam_init) * _silu(z_ref[rows, :].astype(F32))

    for step in range(len(chains) + 2):
        if step < len(chains):
            stage_scores(*chains[step])
        if 0 <= step - 1 < len(chains):
            stage_softmax(*chains[step - 1])
        if 0 <= step - 2 < len(chains):
            stage_values(*chains[step - 2])


def _attn_call(p, kall, vall, wq_row, dlam, subln_row, rope, b, t, lam_init):
    tq = min(ATTN_TQ, t)
    s = kall.shape[2]
    nq = t // tq
    in_specs = [pl.BlockSpec((tq, HEAD_W), lambda bi, h, i: (bi * nq + i, OFF_AQ // HEAD_W + h)),
                pl.BlockSpec((tq, HEAD_W), lambda bi, h, i: (bi * nq + i, OFF_AZ // HEAD_W + h)),
                pl.BlockSpec((1, 1, s, HEAD_W), lambda bi, h, i: (bi, h, 0, 0)),
                pl.BlockSpec((1, 1, VT_ROWS, s), lambda bi, h, i: (bi, h, 0, 0)),
                pl.BlockSpec((1, LANES), lambda bi, h, i: (0, 0)),
                pl.BlockSpec((4, DK_A), lambda bi, h, i: (0, 0)),
                pl.BlockSpec((1, LANES), lambda bi, h, i: (0, 0))]
    args = [p, p, kall, vall, wq_row, dlam, subln_row]
    if rope is not None:
        in_specs += [pl.BlockSpec((tq, LANES), lambda bi, h, i: (i, 0))] * 3
        args += list(rope)
    return pl.pallas_call(
        functools.partial(_attn_kernel, lam_init=lam_init, rope=rope is not None),
        grid=(b, N_HEADS, nq),
        in_specs=in_specs,
        out_specs=pl.BlockSpec((tq, HEAD_W), lambda bi, h, i: (bi * nq + i, h)),
        out_shape=jax.ShapeDtypeStruct((b * t, N_HEADS * HEAD_W), F32),
        scratch_shapes=[pltpu.VMEM((ATTN_SLOTS, s, ATTN_SUB), F32),
                        pltpu.VMEM((ATTN_SLOTS, s, ATTN_SUB), BF16)],
        compiler_params=_cparams(("parallel", "parallel", "arbitrary")),
    )(*args)


def _ret_kernel(*refs, nchunk, has_state):
    if has_state:
        (q_ref, k_ref, v_ref, z_ref, d256_ref, d128_ref, nw_ref, s0_ref,
         o_ref, s_ref, dm_ref, qd_ref, kd_ref, ob_ref) = refs
        st_ref = None
    else:
        (q_ref, k_ref, v_ref, z_ref, d256_ref, d128_ref, nw_ref,
         o_ref, st_ref, s_ref, dm_ref, qd_ref, kd_ref, ob_ref) = refs
    c_len = CHUNK
    lg256 = -_softplus(-d256_ref[...])
    lg128 = -_softplus(-d128_ref[...])
    ri = _row_iota((c_len, c_len)).astype(F32)
    ci = _lane_iota((c_len, c_len)).astype(F32)
    rcol256 = _row_iota((c_len, 2 * HEAD_W)).astype(F32)
    rcol128 = _row_iota((c_len, LANES)).astype(F32)
    blockmask = ((_row_iota((2 * DK_B, 2 * HEAD_W)) < DK_B)
                 == (_lane_iota((2 * DK_B, 2 * HEAD_W)) < HEAD_W))
    for d in range(2):
        rel = (ri - ci) if d == 0 else (ci - ri)
        keep = rel >= 0
        for e in range(2):
            lg = lg256[d:d + 1, e * HEAD_W:e * HEAD_W + 1]
            dm_ref[d, e] = jnp.where(keep, jnp.exp(jnp.where(keep, rel, 0.0) * lg), 0.0)
        qpow = (rcol256 + 1.0) if d == 0 else (c_len - rcol256)
        qd_ref[d] = jnp.exp(qpow * lg256[d:d + 1])
        kpow = (c_len - 1.0 - rcol128) if d == 0 else rcol128
        kd_ref[d] = jnp.exp(kpow * lg128[d:d + 1])
        if has_state:
            s_ref[d] = jnp.zeros((2 * DK_B, 2 * HEAD_W), F32)
            for e in range(2):
                s_ref[d, e * DK_B:(e + 1) * DK_B, e * HEAD_W:(e + 1) * HEAD_W] = s0_ref[d, e]
        else:
            s_ref[d] = jnp.zeros((2 * DK_B, 2 * HEAD_W), F32)
    cdec = [jnp.exp(float(c_len) * lg256[d:d + 1]) for d in range(2)]
    lane_lo = _lane_iota((c_len, LANES)) < DK_B
    nw = nw_ref[...]

    def scan_body(i, carry):
        dirs = ((0, pl.multiple_of(i * c_len, c_len)),
                (1, pl.multiple_of((nchunk - 1 - i) * c_len, c_len)))
        q = [q_ref[pl.ds(t0, c_len), :] for _, t0 in dirs]
        k = [k_ref[pl.ds(t0, c_len), :] for _, t0 in dirs]
        vb = [v_ref[pl.ds(t0, c_len), :] for _, t0 in dirs]
        st = [s_ref[d] for d, _ in dirs]
        kb = [x.astype(BF16) for x in k]
        heads = [(d, e) for d, _ in dirs for e in range(2)]
        qe = [jnp.where(lane_lo if e == 0 else jnp.logical_not(lane_lo), q[d], 0.0).astype(BF16)
              for d, e in heads]
        sc = [_dot_nt(qe_i, kb[d]) * dm_ref[d, e] for qe_i, (d, e) in zip(qe, heads)]
        inter = [_dot(q[d].astype(BF16), st[d].astype(BF16)) * qd_ref[d] for d, _ in dirs]
        kv = [lax.dot_general((k[d] * kd_ref[d]).astype(BF16), vb[d], (((0,), (0,)), ((), ())),
                              preferred_element_type=F32) for d, _ in dirs]
        for sc_i, (d, e) in zip(sc, heads):
            sl = slice(e * HEAD_W, (e + 1) * HEAD_W)
            dst = o_ref if d == 0 else ob_ref
            dst[pl.ds(dirs[d][1], c_len), sl] = (_dot(sc_i.astype(BF16), vb[d][:, sl])
                                                 + inter[d][:, sl])
        for d, _ in dirs:
            s_ref[d] = jnp.where(blockmask, st[d] * cdec[d] + kv[d], 0.0)
        return carry

    lax.fori_loop(0, nchunk, scan_body, 0)

    def fin_body(i, carry):
        t0 = pl.multiple_of(i * c_len, c_len)
        for e in range(2):
            sl = slice(e * HEAD_W, (e + 1) * HEAD_W)
            o_e = o_ref[pl.ds(t0, c_len), sl] + ob_ref[pl.ds(t0, c_len), sl]
            o_e = o_e * lax.rsqrt(jnp.mean(o_e * o_e, axis=-1, keepdims=True) + EPS) * nw
            o_ref[pl.ds(t0, c_len), sl] = o_e * _silu(z_ref[pl.ds(t0, c_len), sl].astype(F32))
        return carry

    lax.fori_loop(0, nchunk, fin_body, 0)
    if not has_state:
        for d in range(2):
            for e in range(2):
                st_ref[d, e] = s_ref[d, e * DK_B:(e + 1) * DK_B, e * HEAD_W:(e + 1) * HEAD_W]


def _ret_call(p, rq, rk, d256, d128, nw_row, state, b, t, layer):
    nchunk = t // CHUNK
    hp = N_HEADS // 2
    in_specs = [pl.BlockSpec((t, LANES), lambda bi, h: (bi, h)),
                pl.BlockSpec((t, LANES), lambda bi, h: (bi, h)),
                pl.BlockSpec((t, 256), lambda bi, h: (bi, OFF_BV // 256 + h)),
                pl.BlockSpec((t, 256), lambda bi, h: (bi, OFF_BZ // 256 + h)),
                pl.BlockSpec((None, 2, 256), lambda bi, h: (h, 0, 0)),
                pl.BlockSpec((None, 2, LANES), lambda bi, h: (h, 0, 0)),
                pl.BlockSpec((1, LANES), lambda bi, h: (0, 0))]
    args = [rq, rk, p, p, d256, d128, nw_row]
    scratch = [pltpu.VMEM((2, 2 * DK_B, 2 * HEAD_W), F32),
               pltpu.VMEM((2, 2, CHUNK, CHUNK), F32),
               pltpu.VMEM((2, CHUNK, 2 * HEAD_W), F32),
               pltpu.VMEM((2, CHUNK, LANES), F32),
               pltpu.VMEM((t, 2 * HEAD_W), F32)]
    o_spec = pl.BlockSpec((t, 256), lambda bi, h: (bi, h))
    o_shape = jax.ShapeDtypeStruct((b * t, N_HEADS * HEAD_W), F32)
    if state is not None:
        in_specs.append(pl.BlockSpec((None, None, 2, 2, DK_B, HEAD_W),
                                     lambda bi, h: (bi, layer, 0, h, 0, 0)))
        args.append(state)
        out_specs, out_shape = o_spec, o_shape
    else:
        out_specs = [o_spec, pl.BlockSpec((None, 2, 2, DK_B, HEAD_W), lambda bi, h: (bi, 0, h, 0, 0))]
        out_shape = [o_shape, jax.ShapeDtypeStruct((b, 2, N_HEADS, DK_B, HEAD_W), F32)]
    return pl.pallas_call(
        functools.partial(_ret_kernel, nchunk=nchunk, has_state=state is not None),
        grid=(b, hp),
        in_specs=in_specs,
        out_specs=out_specs,
        out_shape=out_shape,
        scratch_shapes=scratch,
        compiler_params=_cparams(("parallel", "parallel")),
    )(*args)


def _gdn_kernel(*refs, t, nh, has_state):
    if has_state:
        (cq_ref, ck_ref, cv_ref, cz_ref, ba_ref, wq_ref, wk_ref, wv_ref, al_ref, dt_ref, nw_ref,
         s0_ref, o_ref, qs_ref, ks_ref, vs_ref, u_ref, w_ref, qk_ref, qg_ref, kdt_ref, et_ref,
         ob_ref, s_ref) = refs
        st_ref = None
    else:
        (cq_ref, ck_ref, cv_ref, cz_ref, ba_ref, wq_ref, wk_ref, wv_ref, al_ref, dt_ref, nw_ref,
         o_ref, st_ref, qs_ref, ks_ref, vs_ref, u_ref, w_ref, qk_ref, qg_ref, kdt_ref, et_ref,
         ob_ref, s_ref) = refs
    c_len = CHUNK
    nchunk = t // c_len
    head0 = pl.program_id(1) * nh
    pad = 16
    win = 256
    nwin = t // win

    def conv_window(r0, first, last):
        lo = 0 if first else pad
        hi = 0 if last else pad
        start = r0 - lo if isinstance(r0, int) else pl.multiple_of(r0 - lo, pad)
        for hh, (a, (src, cw_ref, dst)) in itertools.product(
                range(nh), enumerate(((cq_ref, wq_ref, qs_ref), (ck_ref, wk_ref, ks_ref),
                                      (cv_ref, wv_ref, vs_ref)))):
            hl = slice(hh * HEAD_W, (hh + 1) * HEAD_W)
            xw = src[pl.ds(start, win + lo + hi), hl].astype(F32)
            if first:
                xw = jnp.concatenate([jnp.zeros((pad, LANES), F32), xw], axis=0)
            if last:
                xw = jnp.concatenate([xw, jnp.zeros((pad, LANES), F32)], axis=0)
            cw = cw_ref[:, hl]
            acc = None
            for j in range(CONV_K):
                off = pad - CONV_K // 2 + j
                term = xw[off:off + win, :] * cw[j:j + 1, :]
                acc = term if acc is None else acc + term
            y = _silu(acc)
            if a < 2:
                y = y * lax.rsqrt(jnp.sum(y * y, axis=-1, keepdims=True) + EPS)
            if a == 0:
                y = y * (HEAD_W ** -0.5)
            dst[hh, pl.ds(r0, win), :] = y

    if nwin == 1:
        conv_window(0, True, True)
    else:
        conv_window(0, True, False)

        def conv_body(i, carry):
            conv_window(pl.multiple_of(i * win, win), False, False)
            return carry

        lax.fori_loop(1, nwin - 1, conv_body, 0)
        conv_window((nwin - 1) * win, False, True)

    ri = _row_iota((c_len, c_len))
    ci = _lane_iota((c_len, c_len))
    lane = _lane_iota((c_len, LANES))
    alog = al_ref[...]
    dtb = dt_ref[...]
    tri16 = jnp.where(ri >= ci, 1.0, 0.0).astype(BF16)
    g_lanes = jnp.logical_and(lane >= 2 * N_HEADS, lane < 4 * N_HEADS)

    def lanes3(tgt):
        return jnp.logical_or(lane == tgt, jnp.logical_or(lane == tgt + 16, lane == tgt + 32))

    def chunk_chains(c):
        t0 = pl.multiple_of(c * c_len, c_len)
        ba = ba_ref[pl.ds(t0, c_len), :]
        beta_all = _sigmoid(ba)
        g_all = jnp.where(g_lanes, -jnp.exp(alog) * _softplus(ba + dtb), 0.0)
        g_hi = g_all.astype(BF16).astype(F32)
        r1 = g_all - g_hi
        g_mid = r1.astype(BF16).astype(F32)
        g_lo = (r1 - g_mid).astype(BF16).astype(F32)
        parts = (g_hi + pltpu.roll(g_mid, 16, 1) + pltpu.roll(g_lo, 32, 1)).astype(BF16)
        pre = _dot(tri16, parts)
        chains = []
        for hh in range(nh):
            h = head0 + hh
            tgt_f = 2 * N_HEADS + h
            tgt_b = 3 * N_HEADS + h
            gc_f = jnp.sum(jnp.where(lanes3(tgt_f), pre, 0.0), axis=-1, keepdims=True)
            pre_b = jnp.sum(jnp.where(lanes3(tgt_b), pre, 0.0), axis=-1, keepdims=True)
            g_b = jnp.sum(jnp.where(lane == tgt_b, g_all, 0.0), axis=-1, keepdims=True)
            gc_b = pre_b[c_len - 1:c_len, :] - pre_b + g_b
            beta_f = jnp.sum(jnp.where(lane == h, beta_all, 0.0), axis=-1, keepdims=True)
            beta_b = jnp.sum(jnp.where(lane == N_HEADS + h, beta_all, 0.0), axis=-1,
                             keepdims=True)
            rows = jnp.where(lane == 0, gc_f, jnp.where(lane == 1, gc_b, 0.0)).T

            q = qs_ref[hh, pl.ds(t0, c_len), :]
            k = ks_ref[hh, pl.ds(t0, c_len), :]
            v = vs_ref[hh, pl.ds(t0, c_len), :]
            k16 = k.astype(BF16)
            kkqk = _dot_nt(jnp.concatenate([k16, q.astype(BF16)], axis=0), k16)
            kk = kkqk[0:c_len, :]
            qk_raw = kkqk[c_len:2 * c_len, :]
            for d, (gc, beta) in enumerate(((gc_f, beta_f), (gc_b, beta_b))):
                gr = rows[d:d + 1, :]
                incl = (ri >= ci) if d == 0 else (ci >= ri)
                strict = (ri > ci) if d == 0 else (ci > ri)
                ex = jnp.exp(jnp.where(incl, gc - gr, 0.0))
                a_mat = jnp.where(strict, ex, 0.0) * kk * beta
                egc = jnp.exp(gc)
                rhs = jnp.concatenate([v * beta, k * (beta * egc)], axis=1)
                qk_ref[hh, d, pl.ds(t0, c_len), :] = (jnp.where(incl, ex, 0.0)
                                                      * qk_raw).astype(BF16)
                qg_ref[hh, d, pl.ds(t0, c_len), :] = (q * egc).astype(BF16)
                g_tot = gc[c_len - 1:c_len, :] if d == 0 else gc[0:1, :]
                kdt_ref[hh, d, pl.ds(t0, c_len), :] = (k * jnp.exp(g_tot - gc)).T.astype(BF16)
                et_ref[hh, d, c] = jnp.broadcast_to(jnp.exp(g_tot), (8, LANES))
                chains.append((a_mat, rhs, (hh, d, t0)))
        return chains

    group = min(GDN_GROUP, nchunk)
    assert nchunk % group == 0

    def prep_body(i, carry):
        chains = []
        for j in range(group):
            chains += chunk_chains(i * group + j)
        n_mats = [-jnp.where((ri >> 1) == (ci >> 1), a_mat, 0.0) for a_mat, _, _ in chains]
        sh = 1
        while (1 << sh) < c_len:
            off = jnp.logical_and((ri >> (sh + 1)) == (ci >> (sh + 1)), (ri >> sh) != (ci >> sh))
            l_mats = [jnp.where(off, a_mat, 0.0) for a_mat, _, _ in chains]
            x_mats = [l + _dot(l.astype(BF16), n.astype(BF16)) for l, n in zip(l_mats, n_mats)]
            n_mats = [n - x - _dot(n.astype(BF16), x.astype(BF16)) for n, x in zip(n_mats, x_mats)]
            sh += 1
        for n_mat, (_, rhs, (hh, d, t0)) in zip(n_mats, chains):
            uw = rhs + _dot(n_mat.astype(BF16), rhs.astype(BF16))
            u_ref[hh, d, pl.ds(t0, c_len), :] = uw[:, 0:HEAD_W]
            w_ref[hh, d, pl.ds(t0, c_len), :] = uw[:, HEAD_W:2 * HEAD_W].astype(BF16)
        return carry

    lax.fori_loop(0, nchunk // group, prep_body, 0)

    for hh, d in itertools.product(range(nh), range(2)):
        s_ref[hh, d] = s0_ref[d, hh] if has_state else jnp.zeros((HEAD_W, HEAD_W), F32)

    def scan_body(i, carry):
        chains = [(hh, d, c, pl.multiple_of(c * c_len, c_len))
                  for hh, (d, c) in itertools.product(range(nh), ((0, i), (1, nchunk - 1 - i)))]
        st = [s_ref[hh, d] for hh, d, _, _ in chains]
        st16 = [x.astype(BF16) for x in st]
        ws = [_dot(w_ref[hh, d, pl.ds(t0, c_len), :], s16)
              for (hh, d, _, t0), s16 in zip(chains, st16)]
        qs = [_dot(qg_ref[hh, d, pl.ds(t0, c_len), :], s16)
              for (hh, d, _, t0), s16 in zip(chains, st16)]
        vn16 = [(u_ref[hh, d, pl.ds(t0, c_len), :] - x).astype(BF16)
                for (hh, d, _, t0), x in zip(chains, ws)]
        kv = [_dot(kdt_ref[hh, d, pl.ds(t0, c_len), :], v16)
              for (hh, d, _, t0), v16 in zip(chains, vn16)]
        for (hh, d, c, _), s_old, x in zip(chains, st, kv):
            s_ref[hh, d] = s_old * et_ref[hh, d, c][0:1, :] + x
        for (hh, d, _, t0), x, v16 in zip(chains, qs, vn16):
            o = x + _dot(qk_ref[hh, d, pl.ds(t0, c_len), :], v16)
            dst = o_ref if d == 0 else ob_ref
            dst[pl.ds(t0, c_len), hh * HEAD_W:(hh + 1) * HEAD_W] = o
        return carry

    lax.fori_loop(0, nchunk, scan_body, 0)

    def fin_body(i, carry):
        r0 = pl.multiple_of(i * win, win)
        for hh in range(nh):
            hl = slice(hh * HEAD_W, (hh + 1) * HEAD_W)
            o = o_ref[pl.ds(r0, win), hl] + ob_ref[pl.ds(r0, win), hl]
            o = o * lax.rsqrt(jnp.mean(o * o, axis=-1, keepdims=True) + EPS) * nw_ref[...]
            o_ref[pl.ds(r0, win), hl] = o * _silu(cz_ref[pl.ds(r0, win), hl].astype(F32))
        return carry

    lax.fori_loop(0, nwin, fin_body, 0)
    if not has_state:
        for hh, d in itertools.product(range(nh), range(2)):
            st_ref[d, hh] = s_ref[hh, d]


def _gdn_heads_per_program(t):
    per_head = t * LANES * (4 * 2 * 2 + 4 * 2 + 3 * 4 + 2 * 4 + 4 * 2 * 2 + 4)
    nh = N_HEADS
    while nh > 1 and nh * per_head > GDN_VMEM_BUDGET:
        nh //= 2
    return nh


def _gdn_call(p, p_ba, conv_w, al_row, dt_row, nw_row, state, b, t, layer):
    nh = _gdn_heads_per_program(t)
    wd = nh * HEAD_W

    def col(off):
        return lambda bi, h: (bi, off // wd + h)

    in_specs = [pl.BlockSpec((t, wd), col(OFF_CQ)),
                pl.BlockSpec((t, wd), col(OFF_CK)),
                pl.BlockSpec((t, wd), col(OFF_CV)),
                pl.BlockSpec((t, wd), col(OFF_CZ)),
                pl.BlockSpec((t, LANES), lambda bi, h: (bi, 0)),
                pl.BlockSpec((CONV_K, wd), lambda bi, h: (0, h)),
                pl.BlockSpec((CONV_K, wd), lambda bi, h: (0, N_HEADS // nh + h)),
                pl.BlockSpec((CONV_K, wd), lambda bi, h: (0, 2 * (N_HEADS // nh) + h)),
                pl.BlockSpec((1, LANES), lambda bi, h: (0, 0)),
                pl.BlockSpec((1, LANES), lambda bi, h: (0, 0)),
                pl.BlockSpec((1, LANES), lambda bi, h: (0, 0))]
    args = [p, p, p, p, p_ba, conv_w, conv_w, conv_w, al_row, dt_row, nw_row]
    scratch = [pltpu.VMEM((nh, t, LANES), F32),
               pltpu.VMEM((nh, t, LANES), F32),
               pltpu.VMEM((nh, t, LANES), F32),
               pltpu.VMEM((nh, 2, t, LANES), F32),
               pltpu.VMEM((nh, 2, t, LANES), BF16),
               pltpu.VMEM((nh, 2, t, LANES), BF16),
               pltpu.VMEM((nh, 2, t, LANES), BF16),
               pltpu.VMEM((nh, 2, t, LANES), BF16),
               pltpu.VMEM((nh, 2, t // CHUNK, 8, LANES), F32),
               pltpu.VMEM((t, wd), F32),
               pltpu.VMEM((nh, 2, HEAD_W, HEAD_W), F32)]
    o_spec = pl.BlockSpec((t, wd), lambda bi, h: (bi, h))
    o_shape = jax.ShapeDtypeStruct((b * t, N_HEADS * HEAD_W), F32)
    if state is not None:
        in_specs.append(pl.BlockSpec((None, None, 2, nh, HEAD_W, HEAD_W),
                                     lambda bi, h: (bi, layer, 0, h, 0, 0)))
        args.append(state)
        out_specs, out_shape = o_spec, o_shape
    else:
        out_specs = [o_spec, pl.BlockSpec((None, 2, nh, HEAD_W, HEAD_W),
                                          lambda bi, h: (bi, 0, h, 0, 0))]
        out_shape = [o_shape, jax.ShapeDtypeStruct((b, 2, N_HEADS, HEAD_W, HEAD_W), F32)]
    return pl.pallas_call(
        functools.partial(_gdn_kernel, t=t, nh=nh, has_state=state is not None),
        grid=(b, N_HEADS // nh),
        in_specs=in_specs,
        out_specs=out_specs,
        out_shape=out_shape,
        scratch_shapes=scratch,
        compiler_params=_cparams(("parallel", "parallel")),
    )(*args)


def _out_kernel(ya_ref, yb_ref, yc_ref, mg_ref, x_ref, mod_ref, wb_ref, wo_ref, o_ref, *, d):
    ya = _dot(ya_ref[...].astype(BF16), wb_ref[0])
    yb = _dot(yb_ref[...].astype(BF16), wb_ref[1])
    yc = _dot(yc_ref[...].astype(BF16), wb_ref[2])
    y = (_sigmoid(mg_ref[:, 0:d].astype(F32)) * ya + _sigmoid(mg_ref[:, d:2 * d].astype(F32)) * yb
         + _sigmoid(mg_ref[:, 2 * d:3 * d].astype(F32)) * yc)
    out = _dot(y.astype(BF16), wo_ref[...])
    gate = mod_ref[0][:, 2 * d:3 * d]
    o_ref[...] = x_ref[...] + gate * out


def _out_call(ya, yb, yc, p, x2, mod, mod_row, wb, wo, mg_block, tm):
    n, d = x2.shape
    w_br = ya.shape[1]
    return pl.pallas_call(
        functools.partial(_out_kernel, d=d),
        grid=(n // tm,),
        in_specs=[pl.BlockSpec((tm, w_br), lambda i: (i, 0)),
                  pl.BlockSpec((tm, w_br), lambda i: (i, 0)),
                  pl.BlockSpec((tm, w_br), lambda i: (i, 0)),
                  pl.BlockSpec((tm, 3 * d), lambda i: (i, mg_block)),
                  pl.BlockSpec((tm, d), lambda i: (i, 0)),
                  pl.BlockSpec((1, 1, 3 * d), lambda i: (mod_row(i), 0, 0)),
                  pl.BlockSpec((3, w_br, d), lambda i: (0, 0, 0)),
                  pl.BlockSpec((d, d), lambda i: (0, 0))],
        out_specs=pl.BlockSpec((tm, d), lambda i: (i, 0)),
        out_shape=jax.ShapeDtypeStruct((n, d), F32),
        compiler_params=_cparams(("parallel",)),
    )(ya, yb, yc, p, x2, mod, wb, wo)


def _rope_tables(n_tokens, dtype):
    n_rows = n_tokens // GRID_W
    row = jnp.repeat(jnp.arange(n_rows, dtype=jnp.float32), GRID_W)
    col = jnp.tile(jnp.arange(GRID_W, dtype=jnp.float32), n_rows)
    n_freq = DK_A // 4
    inv = 1.0 / (ROPE_BASE ** (jnp.arange(n_freq, dtype=jnp.float32) / n_freq))
    ar = row[:, None] * inv
    ac = col[:, None] * inv
    ang = jnp.concatenate([ar, ar, ac, ac], axis=-1)
    cos = jnp.tile(jnp.cos(ang).astype(dtype), (1, 2))
    sin = jnp.tile(jnp.sin(ang).astype(dtype), (1, 2))
    first = (jnp.arange(LANES) % 32) < 16
    sin_a = jnp.where(first, -sin, 0.0)
    sin_b = jnp.where(first, 0.0, sin)
    return cos, sin_a, sin_b


PROJ_TM = 1024
PROJ_TN = 1536
OUT_TM = 256


def _layer(x2, mod, mod_row_for, layer, wts, rope, ctx, b, t):
    p, p_ba = _proj_call(x2, mod, mod_row_for(PROJ_TM), wts["norm_w"], wts["w_in"], PROJ_TM)
    cache = None if ctx is None else (ctx[0], ctx[1])
    prep = _prep_call(p, b, t, wts["wk_row"], rope, cache, layer)
    kall, vall, rq, rk = prep[:4]
    lam_init = 0.8 - 0.6 * math.exp(-0.3 * layer)
    ya = _attn_call(p, kall, vall, wts["wq_row"], wts["diff_lambda"], wts["subln_row"], rope,
                    b, t, lam_init)
    ret = _ret_call(p, rq, rk, wts["d256"], wts["d128"], wts["ret_norm_row"],
                    None if ctx is None else ctx[2], b, t, layer)
    gdn = _gdn_call(p, p_ba, wts["conv_w"], wts["al_row"], wts["dt_row"], wts["gdn_norm_row"],
                    None if ctx is None else ctx[3], b, t, layer)
    if ctx is None:
        yb, s_ret = ret
        yc, s_gdn = gdn
        extras = (prep[4], prep[5], s_ret, s_gdn)
    else:
        yb, yc = ret, gdn
        extras = None
    mg_block = wts["mg_off"] // (3 * x2.shape[1])
    x2 = _out_call(ya, yb, yc, p, x2, mod, mod_row_for(OUT_TM), wts["w_branch"], wts["w_out"],
                   mg_block, OUT_TM)
    return x2, extras


def kernel(x_prompt, x_sample, cache_attn_k, cache_attn_v, state_ret, state_gdn, c, c_ctx,
           norm_w, w_ada, b_ada, w_in, qk_norm_w, diff_lambda, subln_w, ret_decay, ret_norm_w,
           conv_w, gdn_a_log, gdn_dt_bias, gdn_norm_w, w_branch, w_out):
    b_ctx, t_ctx, d = x_prompt.shape
    b_lat, t_lat, _ = x_sample.shape
    depth = w_in.shape[0]
    past = cache_attn_k.shape[2]
    assert b_lat <= 4 and d % LANES == 0

    cond = jnp.zeros((8, d), F32).at[:b_lat].set(c).at[4].set(c_ctx)
    mods = _ada_call(cond, w_ada, b_ada)

    mg_off = -(-(OFF_BA + LANES) // (3 * d)) * (3 * d)
    n_cols = mg_off + 3 * d
    n_cols = -(-n_cols // 1536) * 1536
    n_ba = N_MIX + 4 * N_HEADS
    w_pad = jnp.concatenate(
        [w_in[:, :, :n_ba].astype(BF16), jnp.zeros((depth, d, mg_off - n_ba), BF16),
         w_in[:, :, n_ba:].astype(BF16), jnp.zeros((depth, d, n_cols - mg_off - 3 * d), BF16)],
        axis=2)

    rope = _rope_tables(t_lat, x_sample.dtype)
    cache_k = cache_attn_k.reshape(b_lat, depth, past, N_HEADS * 2 * DK_A)
    cache_v = cache_attn_v.reshape(b_lat, depth, past, N_HEADS * HEAD_W)

    lanes16 = jnp.zeros((depth, LANES), F32)
    al_rows = lanes16.at[:, 8:16].set(gdn_a_log.reshape(depth, 8))
    dt_rows = lanes16.at[:, 8:16].set(gdn_dt_bias.reshape(depth, 8))
    dec = ret_decay.reshape(depth, 2, 2, 2)
    dec = jnp.transpose(dec, (0, 2, 1, 3))
    d256 = jnp.repeat(dec, HEAD_W, axis=-1)
    d128 = jnp.repeat(dec, DK_B, axis=-1)

    y_p = x_prompt.reshape(b_ctx * t_ctx, d)
    y_s = x_sample.reshape(b_lat * t_lat, d)
    assert t_lat % PROJ_TM == 0 and (b_ctx * t_ctx) % PROJ_TM == 0

    def ctx_row(tm):
        return lambda i: 4

    def lat_row(tm):
        return lambda i: i // (t_lat // tm)

    ks, vs, rs, gs = [], [], [], []
    for l in range(depth):
        wts = {
            "norm_w": norm_w[l].reshape(1, d),
            "w_in": w_pad[l],
            "mg_off": mg_off,
            "wq_row": jnp.tile(qk_norm_w[l, 0], 2).reshape(1, LANES),
            "wk_row": jnp.tile(qk_norm_w[l, 1], 2).reshape(1, LANES),
            "diff_lambda": diff_lambda[l],
            "subln_row": subln_w[l].reshape(1, LANES),
            "d256": d256[l], "d128": d128[l],
            "ret_norm_row": ret_norm_w[l].reshape(1, LANES),
            "conv_w": conv_w[l],
            "al_row": al_rows[l].reshape(1, LANES),
            "dt_row": dt_rows[l].reshape(1, LANES),
            "gdn_norm_row": gdn_norm_w[l].reshape(1, LANES),
            "w_branch": w_branch[l].astype(BF16),
            "w_out": w_out[l].astype(BF16),
        }
        mod = mods[l].reshape(8, 1, 3 * d)
        y_p, (k_l, v_l, r_l, g_l) = _layer(y_p, mod, ctx_row, l, wts, None, None, b_ctx, t_ctx)
        y_s, _ = _layer(y_s, mod, lat_row, l, wts, rope,
                        (cache_k, cache_v, state_ret, state_gdn), b_lat, t_lat)
        ks.append(k_l)
        vs.append(v_l)
        rs.append(r_l)
        gs.append(g_l)
    new_k = jnp.stack(ks, axis=1).reshape(b_ctx, depth, t_ctx, N_HEADS, 2, DK_A)
    new_v = jnp.stack(vs, axis=1).reshape(b_ctx, depth, t_ctx, N_HEADS, HEAD_W)
    return (y_p.reshape(b_ctx, t_ctx, d), y_s.reshape(b_lat, t_lat, d), new_k, new_v,
            jnp.stack(rs, axis=1), jnp.stack(gs, axis=1))
```

```python
import functools
import itertools
import math

import jax
import jax.numpy as jnp
from jax import lax
from jax.experimental import pallas as pl
from jax.experimental.pallas import tpu as pltpu

F32 = jnp.float32
BF16 = jnp.bfloat16
HIGHEST = lax.Precision.HIGHEST

N_HEADS = 4
DK_A = 64
DK_B = 64
HEAD_W = 128
CONV_K = 5
GRID_W = 64
ROPE_BASE = 10000.0
EPS = 1e-6
LANES = 128
CHUNK = 128
ATTN_SUB = 256
ATTN_KEYS = 256
ATTN_TQ = 1024
ATTN_SLOTS = 4
VT_ROWS = HEAD_W + 16
LOG2E = 1.4426950408889634
GDN_GROUP = 8
GDN_VMEM_BUDGET = 40 * 1024 * 1024
VMEM_LIMIT = 56 * 1024 * 1024

OFF_AQ, OFF_AK, OFF_AV, OFF_AZ = 0, 512, 1024, 1536
OFF_BQ, OFF_BK, OFF_BV, OFF_BZ = 2048, 2304, 2560, 3072
OFF_CQ, OFF_CK, OFF_CV, OFF_CZ = 3584, 4096, 4608, 5120
OFF_BA = 5632
N_MIX = 5632


def _cparams(sem):
    return pltpu.CompilerParams(dimension_semantics=sem, vmem_limit_bytes=VMEM_LIMIT)


def _sigmoid(x):
    return 1.0 / (1.0 + jnp.exp(-x))


def _silu(x):
    return x * _sigmoid(x)


def _softplus(x):
    return jnp.maximum(x, 0.0) + jnp.log1p(jnp.exp(-jnp.abs(x)))


def _dot(a, b):
    return jnp.dot(a, b, preferred_element_type=F32)


def _dot_nt(a, b):
    return lax.dot_general(a, b, (((1,), (1,)), ((), ())), preferred_element_type=F32)


def _lane_iota(shape):
    return lax.broadcasted_iota(jnp.int32, shape, len(shape) - 1)


def _row_iota(shape):
    return lax.broadcasted_iota(jnp.int32, shape, len(shape) - 2)


def _norm_halves(x):
    lo = _lane_iota(x.shape) < 64
    x2 = x * x
    s0 = jnp.sum(jnp.where(lo, x2, 0.0), axis=-1, keepdims=True)
    s1 = jnp.sum(jnp.where(lo, 0.0, x2), axis=-1, keepdims=True)
    return x * lax.rsqrt(jnp.where(lo, s0, s1) * (1.0 / 64.0) + EPS)


def _store_values_t(vall_ref, h, v):
    vall_ref[0, h, 0:HEAD_W, :] = v.astype(F32).T.astype(BF16)
    vall_ref[0, h, HEAD_W:VT_ROWS, :] = jnp.ones((VT_ROWS - HEAD_W, v.shape[0]), BF16)


def _rope(x, cos, sin_a, sin_b):
    return (x * cos + pltpu.roll(x, LANES - 16, 1) * sin_a + pltpu.roll(x, 16, 1) * sin_b)


def _ada_kernel(cond_ref, w_ref, b_ref, o_ref):
    c = cond_ref[...]
    o_ref[0] = jnp.dot(_silu(c), w_ref[0], precision=HIGHEST,
                       preferred_element_type=F32) + b_ref[0]


def _ada_call(cond, w_ada, b_ada):
    depth, d, d3 = w_ada.shape
    tn = 1024
    return pl.pallas_call(
        _ada_kernel,
        grid=(depth, d3 // tn),
        in_specs=[pl.BlockSpec((8, d), lambda l, j: (0, 0)),
                  pl.BlockSpec((1, d, tn), lambda l, j: (l, 0, j)),
                  pl.BlockSpec((1, 1, tn), lambda l, j: (l, 0, j))],
        out_specs=pl.BlockSpec((1, 8, tn), lambda l, j: (l, 0, j)),
        out_shape=jax.ShapeDtypeStruct((depth, 8, d3), F32),
        compiler_params=_cparams(("parallel", "parallel")),
    )(cond, w_ada, b_ada.reshape(depth, 1, d3))


def _proj_kernel(x_ref, mod_ref, nw_ref, w_ref, o_ref, ba_ref, h_ref, *, d, ba_tile, ba_off):
    j = pl.program_id(1)

    @pl.when(j == 0)
    def _():
        x = x_ref[...]
        y = x * lax.rsqrt(jnp.mean(x * x, axis=-1, keepdims=True) + EPS) * nw_ref[...]
        mod = mod_ref[0]
        h_ref[...] = (y * (1.0 + mod[:, d:2 * d]) + mod[:, 0:d]).astype(BF16)

    acc = _dot(h_ref[...], w_ref[...])
    o_ref[...] = acc.astype(BF16)

    @pl.when(j == ba_tile)
    def _():
        ba_ref[...] = acc[:, ba_off:ba_off + LANES]


def _proj_call(x2, mod, mod_row, norm_w, w_pad, tm):
    n, d = x2.shape
    n_cols = w_pad.shape[1]
    tn = PROJ_TN
    return pl.pallas_call(
        functools.partial(_proj_kernel, d=d, ba_tile=OFF_BA // tn, ba_off=OFF_BA % tn),
        grid=(n // tm, n_cols // tn),
        in_specs=[pl.BlockSpec((tm, d), lambda i, j: (i, 0)),
                  pl.BlockSpec((1, 1, 3 * d), lambda i, j: (mod_row(i), 0, 0)),
                  pl.BlockSpec((1, d), lambda i, j: (0, 0)),
                  pl.BlockSpec((d, tn), lambda i, j: (0, j))],
        out_specs=[pl.BlockSpec((tm, tn), lambda i, j: (i, j)),
                   pl.BlockSpec((tm, LANES), lambda i, j: (i, 0))],
        out_shape=[jax.ShapeDtypeStruct((n, n_cols), BF16),
                   jax.ShapeDtypeStruct((n, LANES), F32)],
        scratch_shapes=[pltpu.VMEM((tm, d), BF16)],
        compiler_params=_cparams(("parallel", "arbitrary")),
    )(x2, mod, norm_w, w_pad)


def _prep_ctx_kernel(ak_ref, av_ref, bq_ref, bk_ref, wk_ref,
                     kall_ref, vall_ref, rq_ref, rk_ref, ka_ref, va_ref):
    wk = wk_ref[...]
    for h in range(N_HEADS):
        sl = slice(h * HEAD_W, (h + 1) * HEAD_W)
        kn = _norm_halves(ak_ref[:, sl].astype(F32)) * wk
        ka_ref[0, :, sl] = kn
        kall_ref[0, h] = kn.astype(BF16)
        v = av_ref[:, sl]
        va_ref[0, :, sl] = v.astype(F32)
        _store_values_t(vall_ref, h, v)
    rq_ref[...] = bq_ref[...].astype(F32)
    rk_ref[...] = bk_ref[...].astype(F32) * (DK_B ** -0.5)


def _prep_lat_kernel(ak_ref, av_ref, bq_ref, bk_ref, wk_ref, cos_ref, sa_ref, sb_ref,
                     ck_ref, cv_ref, kall_ref, vall_ref, rq_ref, rk_ref, *, nt):
    t = pl.program_id(1)

    @pl.when(t < nt)
    def _():
        wk = wk_ref[...]
        cos, sa, sb = cos_ref[...], sa_ref[...], sb_ref[...]
        for h in range(N_HEADS):
            sl = slice(h * HEAD_W, (h + 1) * HEAD_W)
            kn = _norm_halves(ak_ref[:, sl].astype(F32)) * wk
            kall_ref[0, h] = _rope(kn, cos, sa, sb).astype(BF16)
            _store_values_t(vall_ref, h, av_ref[:, sl])
        for e in range(2):
            sl = slice(e * LANES, (e + 1) * LANES)
            rq_ref[:, sl] = _rope(bq_ref[:, sl].astype(F32), cos, sa, sb)
            rk_ref[:, sl] = _rope(bk_ref[:, sl].astype(F32), cos, sa, sb) * (DK_B ** -0.5)

    @pl.when(t == nt)
    def _():
        for h in range(N_HEADS):
            sl = slice(h * HEAD_W, (h + 1) * HEAD_W)
            kall_ref[0, h] = ck_ref[:, sl].astype(BF16)
            _store_values_t(vall_ref, h, cv_ref[:, sl])


def _prep_call(p, b, t, wk_row, rope, cache, layer):
    tk = 256
    nt = t // tk
    n = b * t
    if rope is None:
        return pl.pallas_call(
            _prep_ctx_kernel,
            grid=(b, nt),
            in_specs=[pl.BlockSpec((tk, 512), lambda bi, ti: (bi * nt + ti, OFF_AK // 512)),
                      pl.BlockSpec((tk, 512), lambda bi, ti: (bi * nt + ti, OFF_AV // 512)),
                      pl.BlockSpec((tk, 256), lambda bi, ti: (bi * nt + ti, OFF_BQ // 256)),
                      pl.BlockSpec((tk, 256), lambda bi, ti: (bi * nt + ti, OFF_BK // 256)),
                      pl.BlockSpec((1, LANES), lambda bi, ti: (0, 0))],
            out_specs=[pl.BlockSpec((1, N_HEADS, tk, HEAD_W), lambda bi, ti: (bi, 0, ti, 0)),
                       pl.BlockSpec((1, N_HEADS, VT_ROWS, tk), lambda bi, ti: (bi, 0, 0, ti)),
                       pl.BlockSpec((tk, 256), lambda bi, ti: (bi * nt + ti, 0)),
                       pl.BlockSpec((tk, 256), lambda bi, ti: (bi * nt + ti, 0)),
                       pl.BlockSpec((1, tk, 512), lambda bi, ti: (bi, ti, 0)),
                       pl.BlockSpec((1, tk, 512), lambda bi, ti: (bi, ti, 0))],
            out_shape=[jax.ShapeDtypeStruct((b, N_HEADS, t, HEAD_W), BF16),
                       jax.ShapeDtypeStruct((b, N_HEADS, VT_ROWS, t), BF16),
                       jax.ShapeDtypeStruct((n, 256), F32),
                       jax.ShapeDtypeStruct((n, 256), F32),
                       jax.ShapeDtypeStruct((b, t, 512), F32),
                       jax.ShapeDtypeStruct((b, t, 512), F32)],
            compiler_params=_cparams(("parallel", "parallel")),
        )(p, p, p, p, wk_row)
    cos, sa, sb = rope
    cache_k, cache_v = cache
    past = cache_k.shape[2]
    assert past == tk
    s = t + past

    def tok(bi, ti):
        return bi * nt + jnp.minimum(ti, nt - 1)

    return pl.pallas_call(
        functools.partial(_prep_lat_kernel, nt=nt),
        grid=(b, nt + 1),
        in_specs=[pl.BlockSpec((tk, 512), lambda bi, ti: (tok(bi, ti), OFF_AK // 512)),
                  pl.BlockSpec((tk, 512), lambda bi, ti: (tok(bi, ti), OFF_AV // 512)),
                  pl.BlockSpec((tk, 256), lambda bi, ti: (tok(bi, ti), OFF_BQ // 256)),
                  pl.BlockSpec((tk, 256), lambda bi, ti: (tok(bi, ti), OFF_BK // 256)),
                  pl.BlockSpec((1, LANES), lambda bi, ti: (0, 0)),
                  pl.BlockSpec((tk, LANES), lambda bi, ti: (jnp.minimum(ti, nt - 1), 0)),
                  pl.BlockSpec((tk, LANES), lambda bi, ti: (jnp.minimum(ti, nt - 1), 0)),
                  pl.BlockSpec((tk, LANES), lambda bi, ti: (jnp.minimum(ti, nt - 1), 0)),
                  pl.BlockSpec((None, None, past, 512), lambda bi, ti: (bi, layer, 0, 0)),
                  pl.BlockSpec((None, None, past, 512), lambda bi, ti: (bi, layer, 0, 0))],
        out_specs=[pl.BlockSpec((1, N_HEADS, tk, HEAD_W), lambda bi, ti: (bi, 0, ti, 0)),
                   pl.BlockSpec((1, N_HEADS, VT_ROWS, tk), lambda bi, ti: (bi, 0, 0, ti)),
                   pl.BlockSpec((tk, 256), lambda bi, ti: (tok(bi, ti), 0)),
                   pl.BlockSpec((tk, 256), lambda bi, ti: (tok(bi, ti), 0))],
        out_shape=[jax.ShapeDtypeStruct((b, N_HEADS, s, HEAD_W), BF16),
                   jax.ShapeDtypeStruct((b, N_HEADS, VT_ROWS, s), BF16),
                   jax.ShapeDtypeStruct((n, 256), F32),
                   jax.ShapeDtypeStruct((n, 256), F32)],
        compiler_params=_cparams(("parallel", "arbitrary")),
    )(p, p, p, p, wk_row, cos, sa, sb, cache_k, cache_v)


def _attn_kernel(*refs, lam_init, rope):
    if rope:
        (q_ref, z_ref, k_ref, v_ref, wq_ref, dl_ref, sw_ref, cos_ref, sa_ref, sb_ref,
         o_ref, s_ref, p_ref) = refs
    else:
        q_ref, z_ref, k_ref, v_ref, wq_ref, dl_ref, sw_ref, o_ref, s_ref, p_ref = refs
    n_keys = k_ref.shape[2]
    tq = q_ref.shape[0]
    lv = dl_ref[...]
    lam = (jnp.exp(jnp.sum(lv[0:1] * lv[1:2], axis=-1, keepdims=True))
           - jnp.exp(jnp.sum(lv[2:3] * lv[3:4], axis=-1, keepdims=True)) + lam_init)
    chains = [(r0, m) for r0 in range(0, tq, ATTN_SUB) for m in range(2)]
    slot = {chain: i % ATTN_SLOTS for i, chain in enumerate(chains)}
    q_t = {}
    col_max = {}
    den = {}
    acc = {}

    def stage_scores(r0, m):
        rows = slice(r0, r0 + ATTN_SUB)
        if r0 not in q_t:
            qn = _norm_halves(q_ref[rows, :].astype(F32)) * wq_ref[...]
            if rope:
                qn = _rope(qn, cos_ref[rows, :], sa_ref[rows, :], sb_ref[rows, :])
            q_t[r0] = (qn * (DK_A ** -0.5 * LOG2E)).T
        qt = q_t[r0]
        lo = _row_iota(qt.shape) < DK_A
        qm = jnp.where(lo if m == 0 else jnp.logical_not(lo), qt, 0.0).astype(BF16)
        for c0 in range(0, n_keys, ATTN_KEYS):
            s_ref[slot[r0, m], c0:c0 + ATTN_KEYS, :] = _dot(k_ref[0, 0, c0:c0 + ATTN_KEYS, :], qm)

    def stage_softmax(r0, m):
        rows = slice(r0, r0 + ATTN_SUB)
        sl = slot[r0, m]
        mx = s_ref[sl, 0:8, :]
        for j in range(1, n_keys // 8):
            mx = jnp.maximum(mx, s_ref[sl, j * 8:(j + 1) * 8, :])
        mx = jnp.max(mx, axis=0, keepdims=True)
        for c0 in range(0, n_keys, ATTN_KEYS):
            ks = slice(c0, c0 + ATTN_KEYS)
            p_ref[sl, ks, :] = jnp.exp2(s_ref[sl, ks, :] - mx).astype(BF16)

    def stage_values(r0, m):
        rows = slice(r0, r0 + ATTN_SUB)
        pv = _dot(v_ref[0, 0], p_ref[slot[r0, m]])
        acc[r0, m] = pv[0:HEAD_W, :]
        den[r0, m] = pv[HEAD_W:HEAD_W + 1, :]
        if m == 1:
            o = (acc[r0, 0] * (1.0 / den[r0, 0]) - lam * (acc[r0, 1] * (1.0 / den[r0, 1])))
            o = o.T
            o = o * lax.rsqrt(jnp.mean(o * o, axis=-1, keepdims=True) + EPS) * sw_ref[...]
            o_ref[rows, :] = o * (1.0 - lam_init) * _silu(z_ref[rows, :].astype(F32))

    for step in range(len(chains) + 2):
        if step < len(chains):
            stage_scores(*chains[step])
        if 0 <= step - 1 < len(chains):
            stage_softmax(*chains[step - 1])
        if 0 <= step - 2 < len(chains):
            stage_values(*chains[step - 2])


def _attn_call(p, kall, vall, wq_row, dlam, subln_row, rope, b, t, lam_init):
    tq = min(ATTN_TQ, t)
    s = kall.shape[2]
    nq = t // tq
    in_specs = [pl.BlockSpec((tq, HEAD_W), lambda bi, h, i: (bi * nq + i, OFF_AQ // HEAD_W + h)),
                pl.BlockSpec((tq, HEAD_W), lambda bi, h, i: (bi * nq + i, OFF_AZ // HEAD_W + h)),
                pl.BlockSpec((1, 1, s, HEAD_W), lambda bi, h, i: (bi, h, 0, 0)),
                pl.BlockSpec((1, 1, VT_ROWS, s), lambda bi, h, i: (bi, h, 0, 0)),
                pl.BlockSpec((1, LANES), lambda bi, h, i: (0, 0)),
                pl.BlockSpec((4, DK_A), lambda bi, h, i: (0, 0)),
                pl.BlockSpec((1, LANES), lambda bi, h, i: (0, 0))]
    args = [p, p, kall, vall, wq_row, dlam, subln_row]
    if rope is not None:
        in_specs += [pl.BlockSpec((tq, LANES), lambda bi, h, i: (i, 0))] * 3
        args += list(rope)
    return pl.pallas_call(
        functools.partial(_attn_kernel, lam_init=lam_init, rope=rope is not None),
        grid=(b, N_HEADS, nq),
        in_specs=in_specs,
        out_specs=pl.BlockSpec((tq, HEAD_W), lambda bi, h, i: (bi * nq + i, h)),
        out_shape=jax.ShapeDtypeStruct((b * t, N_HEADS * HEAD_W), F32),
        scratch_shapes=[pltpu.VMEM((ATTN_SLOTS, s, ATTN_SUB), F32),
                        pltpu.VMEM((ATTN_SLOTS, s, ATTN_SUB), BF16)],
        compiler_params=_cparams(("parallel", "parallel", "arbitrary")),
    )(*args)


def _ret_kernel(*refs, nchunk, has_state):
    if has_state:
        (q_ref, k_ref, v_ref, z_ref, d256_ref, d128_ref, nw_ref, s0_ref,
         o_ref, s_ref, dm_ref, qd_ref, kd_ref, ob_ref) = refs
        st_ref = None
    else:
        (q_ref, k_ref, v_ref, z_ref, d256_ref, d128_ref, nw_ref,
         o_ref, st_ref, s_ref, dm_ref, qd_ref, kd_ref, ob_ref) = refs
    c_len = CHUNK
    lg256 = -_softplus(-d256_ref[...])
    lg128 = -_softplus(-d128_ref[...])
    ri = _row_iota((c_len, c_len)).astype(F32)
    ci = _lane_iota((c_len, c_len)).astype(F32)
    rcol256 = _row_iota((c_len, 2 * HEAD_W)).astype(F32)
    rcol128 = _row_iota((c_len, LANES)).astype(F32)
    blockmask = ((_row_iota((2 * DK_B, 2 * HEAD_W)) < DK_B)
                 == (_lane_iota((2 * DK_B, 2 * HEAD_W)) < HEAD_W))
    for d in range(2):
        rel = (ri - ci) if d == 0 else (ci - ri)
        keep = rel >= 0
        for e in range(2):
            lg = lg256[d:d + 1, e * HEAD_W:e * HEAD_W + 1]
            dm_ref[d, e] = jnp.where(keep, jnp.exp(jnp.where(keep, rel, 0.0) * lg), 0.0)
        qpow = (rcol256 + 1.0) if d == 0 else (c_len - rcol256)
        qd_ref[d] = jnp.exp(qpow * lg256[d:d + 1])
        kpow = (c_len - 1.0 - rcol128) if d == 0 else rcol128
        kd_ref[d] = jnp.exp(kpow * lg128[d:d + 1])
        if has_state:
            s_ref[d] = jnp.zeros((2 * DK_B, 2 * HEAD_W), F32)
            for e in range(2):
                s_ref[d, e * DK_B:(e + 1) * DK_B, e * HEAD_W:(e + 1) * HEAD_W] = s0_ref[d, e]
        else:
            s_ref[d] = jnp.zeros((2 * DK_B, 2 * HEAD_W), F32)
    cdec = [jnp.exp(float(c_len) * lg256[d:d + 1]) for d in range(2)]
    lane_lo = _lane_iota((c_len, LANES)) < DK_B
    nw = nw_ref[...]

    def scan_body(i, carry):
        dirs = ((0, pl.multiple_of(i * c_len, c_len)),
                (1, pl.multiple_of((nchunk - 1 - i) * c_len, c_len)))
        q = [q_ref[pl.ds(t0, c_len), :] for _, t0 in dirs]
        k = [k_ref[pl.ds(t0, c_len), :] for _, t0 in dirs]
        vb = [v_ref[pl.ds(t0, c_len), :] for _, t0 in dirs]
        st = [s_ref[d] for d, _ in dirs]
        kb = [x.astype(BF16) for x in k]
        heads = [(d, e) for d, _ in dirs for e in range(2)]
        qe = [jnp.where(lane_lo if e == 0 else jnp.logical_not(lane_lo), q[d], 0.0).astype(BF16)
              for d, e in heads]
        sc = [_dot_nt(qe_i, kb[d]) * dm_ref[d, e] for qe_i, (d, e) in zip(qe, heads)]
        inter = [_dot(q[d].astype(BF16), st[d].astype(BF16)) * qd_ref[d] for d, _ in dirs]
        kv = [lax.dot_general((k[d] * kd_ref[d]).astype(BF16), vb[d], (((0,), (0,)), ((), ())),
                              preferred_element_type=F32) for d, _ in dirs]
        for sc_i, (d, e) in zip(sc, heads):
            sl = slice(e * HEAD_W, (e + 1) * HEAD_W)
            dst = o_ref if d == 0 else ob_ref
            dst[pl.ds(dirs[d][1], c_len), sl] = (_dot(sc_i.astype(BF16), vb[d][:, sl])
                                                 + inter[d][:, sl])
        for d, _ in dirs:
            s_ref[d] = jnp.where(blockmask, st[d] * cdec[d] + kv[d], 0.0)
        return carry

    lax.fori_loop(0, nchunk, scan_body, 0)

    def fin_body(i, carry):
        t0 = pl.multiple_of(i * c_len, c_len)
        for e in range(2):
            sl = slice(e * HEAD_W, (e + 1) * HEAD_W)
            o_e = o_ref[pl.ds(t0, c_len), sl] + ob_ref[pl.ds(t0, c_len), sl]
            o_e = o_e * lax.rsqrt(jnp.mean(o_e * o_e, axis=-1, keepdims=True) + EPS) * nw
            o_ref[pl.ds(t0, c_len), sl] = o_e * _silu(z_ref[pl.ds(t0, c_len), sl].astype(F32))
        return carry

    lax.fori_loop(0, nchunk, fin_body, 0)
    if not has_state:
        for d in range(2):
            for e in range(2):
                st_ref[d, e] = s_ref[d, e * DK_B:(e + 1) * DK_B, e * HEAD_W:(e + 1) * HEAD_W]


def _ret_call(p, rq, rk, d256, d128, nw_row, state, b, t, layer):
    nchunk = t // CHUNK
    hp = N_HEADS // 2
    in_specs = [pl.BlockSpec((t, LANES), lambda bi, h: (bi, h)),
                pl.BlockSpec((t, LANES), lambda bi, h: (bi, h)),
                pl.BlockSpec((t, 256), lambda bi, h: (bi, OFF_BV // 256 + h)),
                pl.BlockSpec((t, 256), lambda bi, h: (bi, OFF_BZ // 256 + h)),
                pl.BlockSpec((None, 2, 256), lambda bi, h: (h, 0, 0)),
                pl.BlockSpec((None, 2, LANES), lambda bi, h: (h, 0, 0)),
                pl.BlockSpec((1, LANES), lambda bi, h: (0, 0))]
    args = [rq, rk, p, p, d256, d128, nw_row]
    scratch = [pltpu.VMEM((2, 2 * DK_B, 2 * HEAD_W), F32),
               pltpu.VMEM((2, 2, CHUNK, CHUNK), F32),
               pltpu.VMEM((2, CHUNK, 2 * HEAD_W), F32),
               pltpu.VMEM((2, CHUNK, LANES), F32),
               pltpu.VMEM((t, 2 * HEAD_W), F32)]
    o_spec = pl.BlockSpec((t, 256), lambda bi, h: (bi, h))
    o_shape = jax.ShapeDtypeStruct((b * t, N_HEADS * HEAD_W), F32)
    if state is not None:
        in_specs.append(pl.BlockSpec((None, None, 2, 2, DK_B, HEAD_W),
                                     lambda bi, h: (bi, layer, 0, h, 0, 0)))
        args.append(state)
        out_specs, out_shape = o_spec, o_shape
    else:
        out_specs = [o_spec, pl.BlockSpec((None, 2, 2, DK_B, HEAD_W), lambda bi, h: (bi, 0, h, 0, 0))]
        out_shape = [o_shape, jax.ShapeDtypeStruct((b, 2, N_HEADS, DK_B, HEAD_W), F32)]
    return pl.pallas_call(
        functools.partial(_ret_kernel, nchunk=nchunk, has_state=state is not None),
        grid=(b, hp),
        in_specs=in_specs,
        out_specs=out_specs,
        out_shape=out_shape,
        scratch_shapes=scratch,
        compiler_params=_cparams(("parallel", "parallel")),
    )(*args)


def _gdn_kernel(*refs, t, nh, has_state):
    if has_state:
        (cq_ref, ck_ref, cv_ref, cz_ref, ba_ref, wq_ref, wk_ref, wv_ref, al_ref, dt_ref, nw_ref,
         s0_ref, o_ref, qs_ref, ks_ref, vs_ref, u_ref, w_ref, qk_ref, qg_ref, kdt_ref, et_ref,
         ob_ref, s_ref, xw_ref) = refs
        st_ref = None
    else:
        (cq_ref, ck_ref, cv_ref, cz_ref, ba_ref, wq_ref, wk_ref, wv_ref, al_ref, dt_ref, nw_ref,
         o_ref, st_ref, qs_ref, ks_ref, vs_ref, u_ref, w_ref, qk_ref, qg_ref, kdt_ref, et_ref,
         ob_ref, s_ref, xw_ref) = refs
    c_len = CHUNK
    nchunk = t // c_len
    head0 = pl.program_id(1) * nh
    pad = 16
    win = 256
    nwin = t // win

    def conv_window(r0, first, last):
        lo = 0 if first else pad
        hi = 0 if last else pad
        start = r0 - lo if isinstance(r0, int) else pl.multiple_of(r0 - lo, pad)
        for hh, (a, (src, cw_ref, dst)) in itertools.product(
                range(nh), enumerate(((cq_ref, wq_ref, qs_ref), (ck_ref, wk_ref, ks_ref),
                                      (cv_ref, wv_ref, vs_ref)))):
            hl = slice(hh * HEAD_W, (hh + 1) * HEAD_W)
            xw = src[pl.ds(start, win + lo + hi), hl].astype(F32)
            if first:
                xw = jnp.concatenate([jnp.zeros((pad, LANES), F32), xw], axis=0)
            if last:
                xw = jnp.concatenate([xw, jnp.zeros((pad, LANES), F32)], axis=0)
            cw = cw_ref[:, hl]
            xw_ref[...] = xw
            acc = None
            for j in range(CONV_K):
                off = pad - CONV_K // 2 + j
                term = xw_ref[off:off + win, :] * cw[j:j + 1, :]
                acc = term if acc is None else acc + term
            y = _silu(acc)
            if a < 2:
                y = y * lax.rsqrt(jnp.sum(y * y, axis=-1, keepdims=True) + EPS)
            if a == 0:
                y = y * (HEAD_W ** -0.5)
            dst[hh, pl.ds(r0, win), :] = y

    if nwin == 1:
        conv_window(0, True, True)
    else:
        conv_window(0, True, False)

        def conv_body(i, carry):
            conv_window(pl.multiple_of(i * win, win), False, False)
            return carry

        lax.fori_loop(1, nwin - 1, conv_body, 0)
        conv_window((nwin - 1) * win, False, True)

    ri = _row_iota((c_len, c_len))
    ci = _lane_iota((c_len, c_len))
    lane = _lane_iota((c_len, LANES))
    alog = al_ref[...]
    dtb = dt_ref[...]
    tri16 = jnp.where(ri >= ci, 1.0, 0.0).astype(BF16)
    g_lanes = jnp.logical_and(lane >= 2 * N_HEADS, lane < 4 * N_HEADS)

    def lanes3(tgt):
        return jnp.logical_or(lane == tgt, jnp.logical_or(lane == tgt + 16, lane == tgt + 32))

    def chunk_chains(c):
        t0 = pl.multiple_of(c * c_len, c_len)
        ba = ba_ref[pl.ds(t0, c_len), :]
        beta_all = _sigmoid(ba)
        g_all = jnp.where(g_lanes, -jnp.exp(alog) * _softplus(ba + dtb), 0.0)
        g_hi = g_all.astype(BF16).astype(F32)
        r1 = g_all - g_hi
        g_mid = r1.astype(BF16).astype(F32)
        g_lo = (r1 - g_mid).astype(BF16).astype(F32)
        parts = (g_hi + pltpu.roll(g_mid, 16, 1) + pltpu.roll(g_lo, 32, 1)).astype(BF16)
        pre = _dot(tri16, parts)
        chains = []
        for hh in range(nh):
            h = head0 + hh
            tgt_f = 2 * N_HEADS + h
            tgt_b = 3 * N_HEADS + h
            gc_f = jnp.sum(jnp.where(lanes3(tgt_f), pre, 0.0), axis=-1, keepdims=True)
            pre_b = jnp.sum(jnp.where(lanes3(tgt_b), pre, 0.0), axis=-1, keepdims=True)
            g_b = jnp.sum(jnp.where(lane == tgt_b, g_all, 0.0), axis=-1, keepdims=True)
            gc_b = pre_b[c_len - 1:c_len, :] - pre_b + g_b
            beta_f = jnp.sum(jnp.where(lane == h, beta_all, 0.0), axis=-1, keepdims=True)
            beta_b = jnp.sum(jnp.where(lane == N_HEADS + h, beta_all, 0.0), axis=-1,
                             keepdims=True)
            rows = jnp.where(lane == 0, gc_f, jnp.where(lane == 1, gc_b, 0.0)).T

            q = qs_ref[hh, pl.ds(t0, c_len), :]
            k = ks_ref[hh, pl.ds(t0, c_len), :]
            v = vs_ref[hh, pl.ds(t0, c_len), :]
            k16 = k.astype(BF16)
            kkqk = _dot_nt(jnp.concatenate([k16, q.astype(BF16)], axis=0), k16)
            kk = kkqk[0:c_len, :]
            qk_raw = kkqk[c_len:2 * c_len, :]
            for d, (gc, beta) in enumerate(((gc_f, beta_f), (gc_b, beta_b))):
                gr = rows[d:d + 1, :]
                incl = (ri >= ci) if d == 0 else (ci >= ri)
                strict = (ri > ci) if d == 0 else (ci > ri)
                ex = jnp.exp(jnp.where(incl, gc - gr, 0.0))
                a_mat = jnp.where(strict, ex, 0.0) * kk * beta
                egc = jnp.exp(gc)
                rhs = jnp.concatenate([v * beta, k * (beta * egc)], axis=1)
                qk_ref[hh, d, pl.ds(t0, c_len), :] = (jnp.where(incl, ex, 0.0)
                                                      * qk_raw).astype(BF16)
                qg_ref[hh, d, pl.ds(t0, c_len), :] = (q * egc).astype(BF16)
                g_tot = gc[c_len - 1:c_len, :] if d == 0 else gc[0:1, :]
                kdt_ref[hh, d, pl.ds(t0, c_len), :] = (k * jnp.exp(g_tot - gc)).T.astype(BF16)
                et_ref[hh, d, c] = jnp.broadcast_to(jnp.exp(g_tot), (8, LANES))
                chains.append((a_mat, rhs, (hh, d, t0)))
        return chains

    group = min(GDN_GROUP, nchunk)
    assert nchunk % group == 0

    def prep_body(i, carry):
        chains = []
        for j in range(group):
            chains += chunk_chains(i * group + j)
        n_mats = [-jnp.where((ri >> 1) == (ci >> 1), a_mat, 0.0) for a_mat, _, _ in chains]
        sh = 1
        while (1 << sh) < c_len:
            off = jnp.logical_and((ri >> (sh + 1)) == (ci >> (sh + 1)), (ri >> sh) != (ci >> sh))
            l_mats = [jnp.where(off, a_mat, 0.0) for a_mat, _, _ in chains]
            x_mats = [l + _dot(l.astype(BF16), n.astype(BF16)) for l, n in zip(l_mats, n_mats)]
            n_mats = [n - x - _dot(n.astype(BF16), x.astype(BF16)) for n, x in zip(n_mats, x_mats)]
            sh += 1
        for n_mat, (_, rhs, (hh, d, t0)) in zip(n_mats, chains):
            uw = rhs + _dot(n_mat.astype(BF16), rhs.astype(BF16))
            u_ref[hh, d, pl.ds(t0, c_len), :] = uw[:, 0:HEAD_W]
            w_ref[hh, d, pl.ds(t0, c_len), :] = uw[:, HEAD_W:2 * HEAD_W].astype(BF16)
        return carry

    lax.fori_loop(0, nchunk // group, prep_body, 0)

    for hh, d in itertools.product(range(nh), range(2)):
        s_ref[hh, d] = s0_ref[d, hh] if has_state else jnp.zeros((HEAD_W, HEAD_W), F32)

    def scan_body(i, carry):
        chains = [(hh, d, c, pl.multiple_of(c * c_len, c_len))
                  for hh, (d, c) in itertools.product(range(nh), ((0, i), (1, nchunk - 1 - i)))]
        st = [s_ref[hh, d] for hh, d, _, _ in chains]
        st16 = [x.astype(BF16) for x in st]
        ws = [_dot(w_ref[hh, d, pl.ds(t0, c_len), :], s16)
              for (hh, d, _, t0), s16 in zip(chains, st16)]
        qs = [_dot(qg_ref[hh, d, pl.ds(t0, c_len), :], s16)
              for (hh, d, _, t0), s16 in zip(chains, st16)]
        vn16 = [(u_ref[hh, d, pl.ds(t0, c_len), :] - x).astype(BF16)
                for (hh, d, _, t0), x in zip(chains, ws)]
        kv = [_dot(kdt_ref[hh, d, pl.ds(t0, c_len), :], v16)
              for (hh, d, _, t0), v16 in zip(chains, vn16)]
        for (hh, d, c, _), s_old, x in zip(chains, st, kv):
            s_ref[hh, d] = s_old * et_ref[hh, d, c][0:1, :] + x
        for (hh, d, _, t0), x, v16 in zip(chains, qs, vn16):
            o = x + _dot(qk_ref[hh, d, pl.ds(t0, c_len), :], v16)
            dst = o_ref if d == 0 else ob_ref
            dst[pl.ds(t0, c_len), hh * HEAD_W:(hh + 1) * HEAD_W] = o
        return carry

    lax.fori_loop(0, nchunk, scan_body, 0)

    def fin_body(i, carry):
        r0 = pl.multiple_of(i * win, win)
        for hh in range(nh):
            hl = slice(hh * HEAD_W, (hh + 1) * HEAD_W)
            o = o_ref[pl.ds(r0, win), hl] + ob_ref[pl.ds(r0, win), hl]
            o = o * lax.rsqrt(jnp.mean(o * o, axis=-1, keepdims=True) + EPS) * nw_ref[...]
            o_ref[pl.ds(r0, win), hl] = o * _silu(cz_ref[pl.ds(r0, win), hl].astype(F32))
        return carry

    lax.fori_loop(0, nwin, fin_body, 0)
    if not has_state:
        for hh, d in itertools.product(range(nh), range(2)):
            st_ref[d, hh] = s_ref[hh, d]


def _gdn_heads_per_program(t):
    per_head = t * LANES * (4 * 2 * 2 + 4 * 2 + 3 * 4 + 2 * 4 + 4 * 2 * 2 + 4)
    nh = N_HEADS
    while nh > 1 and nh * per_head > GDN_VMEM_BUDGET:
        nh //= 2
    return nh


def _gdn_call(p, p_ba, conv_w, al_row, dt_row, nw_row, state, b, t, layer):
    nh = _gdn_heads_per_program(t)
    wd = nh * HEAD_W

    def col(off):
        return lambda bi, h: (bi, off // wd + h)

    in_specs = [pl.BlockSpec((t, wd), col(OFF_CQ)),
                pl.BlockSpec((t, wd), col(OFF_CK)),
                pl.BlockSpec((t, wd), col(OFF_CV)),
                pl.BlockSpec((t, wd), col(OFF_CZ)),
                pl.BlockSpec((t, LANES), lambda bi, h: (bi, 0)),
                pl.BlockSpec((CONV_K, wd), lambda bi, h: (0, h)),
                pl.BlockSpec((CONV_K, wd), lambda bi, h: (0, N_HEADS // nh + h)),
                pl.BlockSpec((CONV_K, wd), lambda bi, h: (0, 2 * (N_HEADS // nh) + h)),
                pl.BlockSpec((1, LANES), lambda bi, h: (0, 0)),
                pl.BlockSpec((1, LANES), lambda bi, h: (0, 0)),
                pl.BlockSpec((1, LANES), lambda bi, h: (0, 0))]
    args = [p, p, p, p, p_ba, conv_w, conv_w, conv_w, al_row, dt_row, nw_row]
    scratch = [pltpu.VMEM((nh, t, LANES), F32),
               pltpu.VMEM((nh, t, LANES), F32),
               pltpu.VMEM((nh, t, LANES), F32),
               pltpu.VMEM((nh, 2, t, LANES), F32),
               pltpu.VMEM((nh, 2, t, LANES), BF16),
               pltpu.VMEM((nh, 2, t, LANES), BF16),
               pltpu.VMEM((nh, 2, t, LANES), BF16),
               pltpu.VMEM((nh, 2, t, LANES), BF16),
               pltpu.VMEM((nh, 2, t // CHUNK, 8, LANES), F32),
               pltpu.VMEM((t, wd), F32),
               pltpu.VMEM((nh, 2, HEAD_W, HEAD_W), F32),
               pltpu.VMEM((256 + 2 * 16, LANES), F32)]
    o_spec = pl.BlockSpec((t, wd), lambda bi, h: (bi, h))
    o_shape = jax.ShapeDtypeStruct((b * t, N_HEADS * HEAD_W), F32)
    if state is not None:
        in_specs.append(pl.BlockSpec((None, None, 2, nh, HEAD_W, HEAD_W),
                                     lambda bi, h: (bi, layer, 0, h, 0, 0)))
        args.append(state)
        out_specs, out_shape = o_spec, o_shape
    else:
        out_specs = [o_spec, pl.BlockSpec((None, 2, nh, HEAD_W, HEAD_W),
                                          lambda bi, h: (bi, 0, h, 0, 0))]
        out_shape = [o_shape, jax.ShapeDtypeStruct((b, 2, N_HEADS, HEAD_W, HEAD_W), F32)]
    return pl.pallas_call(
        functools.partial(_gdn_kernel, t=t, nh=nh, has_state=state is not None),
        grid=(b, N_HEADS // nh),
        in_specs=in_specs,
        out_specs=out_specs,
        out_shape=out_shape,
        scratch_shapes=scratch,
        compiler_params=_cparams(("parallel", "parallel")),
    )(*args)


def _out_kernel(ya_ref, yb_ref, yc_ref, mg_ref, x_ref, mod_ref, wb_ref, wo_ref, o_ref, *, d):
    ya = _dot(ya_ref[...].astype(BF16), wb_ref[0])
    yb = _dot(yb_ref[...].astype(BF16), wb_ref[1])
    yc = _dot(yc_ref[...].astype(BF16), wb_ref[2])
    y = (_sigmoid(mg_ref[:, 0:d].astype(F32)) * ya + _sigmoid(mg_ref[:, d:2 * d].astype(F32)) * yb
         + _sigmoid(mg_ref[:, 2 * d:3 * d].astype(F32)) * yc)
    out = _dot(y.astype(BF16), wo_ref[...])
    gate = mod_ref[0][:, 2 * d:3 * d]
    o_ref[...] = x_ref[...] + gate * out


def _out_call(ya, yb, yc, p, x2, mod, mod_row, wb, wo, mg_block, tm):
    n, d = x2.shape
    w_br = ya.shape[1]
    return pl.pallas_call(
        functools.partial(_out_kernel, d=d),
        grid=(n // tm,),
        in_specs=[pl.BlockSpec((tm, w_br), lambda i: (i, 0)),
                  pl.BlockSpec((tm, w_br), lambda i: (i, 0)),
                  pl.BlockSpec((tm, w_br), lambda i: (i, 0)),
                  pl.BlockSpec((tm, 3 * d), lambda i: (i, mg_block)),
                  pl.BlockSpec((tm, d), lambda i: (i, 0)),
                  pl.BlockSpec((1, 1, 3 * d), lambda i: (mod_row(i), 0, 0)),
                  pl.BlockSpec((3, w_br, d), lambda i: (0, 0, 0)),
                  pl.BlockSpec((d, d), lambda i: (0, 0))],
        out_specs=pl.BlockSpec((tm, d), lambda i: (i, 0)),
        out_shape=jax.ShapeDtypeStruct((n, d), F32),
        compiler_params=_cparams(("parallel",)),
    )(ya, yb, yc, p, x2, mod, wb, wo)


def _rope_tables(n_tokens, dtype):
    n_rows = n_tokens // GRID_W
    row = jnp.repeat(jnp.arange(n_rows, dtype=jnp.float32), GRID_W)
    col = jnp.tile(jnp.arange(GRID_W, dtype=jnp.float32), n_rows)
    n_freq = DK_A // 4
    inv = 1.0 / (ROPE_BASE ** (jnp.arange(n_freq, dtype=jnp.float32) / n_freq))
    ar = row[:, None] * inv
    ac = col[:, None] * inv
    ang = jnp.concatenate([ar, ar, ac, ac], axis=-1)
    cos = jnp.tile(jnp.cos(ang).astype(dtype), (1, 2))
    sin = jnp.tile(jnp.sin(ang).astype(dtype), (1, 2))
    first = (jnp.arange(LANES) % 32) < 16
    sin_a = jnp.where(first, -sin, 0.0)
    sin_b = jnp.where(first, 0.0, sin)
    return cos, sin_a, sin_b


PROJ_TM = 2048
PROJ_TN = 1536
OUT_TM = 512


def _layer(x2, mod, mod_row_for, layer, wts, rope, ctx, b, t):
    p, p_ba = _proj_call(x2, mod, mod_row_for(PROJ_TM), wts["norm_w"], wts["w_in"], PROJ_TM)
    cache = None if ctx is None else (ctx[0], ctx[1])
    prep = _prep_call(p, b, t, wts["wk_row"], rope, cache, layer)
    kall, vall, rq, rk = prep[:4]
    lam_init = 0.8 - 0.6 * math.exp(-0.3 * layer)
    ya = _attn_call(p, kall, vall, wts["wq_row"], wts["diff_lambda"], wts["subln_row"], rope,
                    b, t, lam_init)
    ret = _ret_call(p, rq, rk, wts["d256"], wts["d128"], wts["ret_norm_row"],
                    None if ctx is None else ctx[2], b, t, layer)
    gdn = _gdn_call(p, p_ba, wts["conv_w"], wts["al_row"], wts["dt_row"], wts["gdn_norm_row"],
                    None if ctx is None else ctx[3], b, t, layer)
    if ctx is None:
        yb, s_ret = ret
        yc, s_gdn = gdn
        extras = (prep[4], prep[5], s_ret, s_gdn)
    else:
        yb, yc = ret, gdn
        extras = None
    mg_block = wts["mg_off"] // (3 * x2.shape[1])
    x2 = _out_call(ya, yb, yc, p, x2, mod, mod_row_for(OUT_TM), wts["w_branch"], wts["w_out"],
                   mg_block, OUT_TM)
    return x2, extras


def kernel(x_prompt, x_sample, cache_attn_k, cache_attn_v, state_ret, state_gdn, c, c_ctx,
           norm_w, w_ada, b_ada, w_in, qk_norm_w, diff_lambda, subln_w, ret_decay, ret_norm_w,
           conv_w, gdn_a_log, gdn_dt_bias, gdn_norm_w, w_branch, w_out):
    b_ctx, t_ctx, d = x_prompt.shape
    b_lat, t_lat, _ = x_sample.shape
    depth = w_in.shape[0]
    past = cache_attn_k.shape[2]
    assert b_lat <= 4 and d % LANES == 0

    cond = jnp.zeros((8, d), F32).at[:b_lat].set(c).at[4].set(c_ctx)
    mods = _ada_call(cond, w_ada, b_ada)

    mg_off = -(-(OFF_BA + LANES) // (3 * d)) * (3 * d)
    n_cols = mg_off + 3 * d
    n_cols = -(-n_cols // PROJ_TN) * PROJ_TN
    n_ba = N_MIX + 4 * N_HEADS
    w_pad = jnp.concatenate(
        [w_in[:, :, :n_ba].astype(BF16), jnp.zeros((depth, d, mg_off - n_ba), BF16),
         w_in[:, :, n_ba:].astype(BF16), jnp.zeros((depth, d, n_cols - mg_off - 3 * d), BF16)],
        axis=2)

    rope = _rope_tables(t_lat, x_sample.dtype)
    cache_k = cache_attn_k.reshape(b_lat, depth, past, N_HEADS * 2 * DK_A)
    cache_v = cache_attn_v.reshape(b_lat, depth, past, N_HEADS * HEAD_W)

    lanes16 = jnp.zeros((depth, LANES), F32)
    al_rows = lanes16.at[:, 8:16].set(gdn_a_log.reshape(depth, 8))
    dt_rows = lanes16.at[:, 8:16].set(gdn_dt_bias.reshape(depth, 8))
    dec = ret_decay.reshape(depth, 2, 2, 2)
    dec = jnp.transpose(dec, (0, 2, 1, 3))
    d256 = jnp.repeat(dec, HEAD_W, axis=-1)
    d128 = jnp.repeat(dec, DK_B, axis=-1)

    y_p = x_prompt.reshape(b_ctx * t_ctx, d)
    y_s = x_sample.reshape(b_lat * t_lat, d)
    assert t_lat % PROJ_TM == 0 and (b_ctx * t_ctx) % PROJ_TM == 0

    def ctx_row(tm):
        return lambda i: 4

    def lat_row(tm):
        return lambda i: i // (t_lat // tm)

    ks, vs, rs, gs = [], [], [], []
    for l in range(depth):
        wts = {
            "norm_w": norm_w[l].reshape(1, d),
            "w_in": w_pad[l],
            "mg_off": mg_off,
            "wq_row": jnp.tile(qk_norm_w[l, 0], 2).reshape(1, LANES),
            "wk_row": jnp.tile(qk_norm_w[l, 1], 2).reshape(1, LANES),
            "diff_lambda": diff_lambda[l],
            "subln_row": subln_w[l].reshape(1, LANES),
            "d256": d256[l], "d128": d128[l],
            "ret_norm_row": ret_norm_w[l].reshape(1, LANES),
            "conv_w": conv_w[l],
            "al_row": al_rows[l].reshape(1, LANES),
            "dt_row": dt_rows[l].reshape(1, LANES),
            "gdn_norm_row": gdn_norm_w[l].reshape(1, LANES),
            "w_branch": w_branch[l].astype(BF16),
            "w_out": w_out[l].astype(BF16),
        }
        mod = mods[l].reshape(8, 1, 3 * d)
        y_p, (k_l, v_l, r_l, g_l) = _layer(y_p, mod, ctx_row, l, wts, None, None, b_ctx, t_ctx)
        y_s, _ = _layer(y_s, mod, lat_row, l, wts, rope,
                        (cache_k, cache_v, state_ret, state_gdn), b_lat, t_lat)
        ks.append(k_l)
        vs.append(v_l)
        rs.append(r_l)
        gs.append(g_l)
    new_k = jnp.stack(ks, axis=1).reshape(b_ctx, depth, t_ctx, N_HEADS, 2, DK_A)
    new_v = jnp.stack(vs, axis=1).reshape(b_ctx, depth, t_ctx, N_HEADS, HEAD_W)
    return (y_p.reshape(b_ctx, t_ctx, d), y_s.reshape(b_lat, t_lat, d), new_k, new_v,
            jnp.stack(rs, axis=1), jnp.stack(gs, axis=1))
```

```python
import functools
import itertools
import math

import jax
import jax.numpy as jnp
from jax import lax
from jax.experimental import pallas as pl
from jax.experimental.pallas import tpu as pltpu

F32 = jnp.float32
BF16 = jnp.bfloat16
HIGHEST = lax.Precision.HIGHEST

N_HEADS = 4
DK_A = 64
DK_B = 64
HEAD_W = 128
CONV_K = 5
GRID_W = 64
ROPE_BASE = 10000.0
EPS = 1e-6
LANES = 128
CHUNK = 128
ATTN_SUB = 256
ATTN_KEYS = 256
ATTN_TQ = 1024
ATTN_SLOTS = 4
VT_ROWS = HEAD_W + 16
LOG2E = 1.4426950408889634
RET_UNROLL = 4
GDN_GROUP = 8
GDN_VMEM_BUDGET = 40 * 1024 * 1024
VMEM_LIMIT = 56 * 1024 * 1024

OFF_AQ, OFF_AK, OFF_AV, OFF_AZ = 0, 512, 1024, 1536
OFF_BQ, OFF_BK, OFF_BV, OFF_BZ = 2048, 2304, 2560, 3072
OFF_CQ, OFF_CK, OFF_CV, OFF_CZ = 3584, 4096, 4608, 5120
OFF_BA = 5632
N_MIX = 5632


def _cparams(sem):
    return pltpu.CompilerParams(dimension_semantics=sem, vmem_limit_bytes=VMEM_LIMIT)


def _sigmoid(x):
    return 1.0 / (1.0 + jnp.exp(-x))


def _silu(x):
    return x * _sigmoid(x)


def _softplus(x):
    return jnp.maximum(x, 0.0) + jnp.log1p(jnp.exp(-jnp.abs(x)))


def _dot(a, b):
    return jnp.dot(a, b, preferred_element_type=F32)


def _dot_nt(a, b):
    return lax.dot_general(a, b, (((1,), (1,)), ((), ())), preferred_element_type=F32)


def _lane_iota(shape):
    return lax.broadcasted_iota(jnp.int32, shape, len(shape) - 1)


def _row_iota(shape):
    return lax.broadcasted_iota(jnp.int32, shape, len(shape) - 2)


def _norm_halves(x):
    lo = _lane_iota(x.shape) < 64
    x2 = x * x
    s0 = jnp.sum(jnp.where(lo, x2, 0.0), axis=-1, keepdims=True)
    s1 = jnp.sum(jnp.where(lo, 0.0, x2), axis=-1, keepdims=True)
    return x * lax.rsqrt(jnp.where(lo, s0, s1) * (1.0 / 64.0) + EPS)


def _store_values_t(vall_ref, h, v):
    vall_ref[0, h, 0:HEAD_W, :] = v.astype(F32).T.astype(BF16)
    vall_ref[0, h, HEAD_W:VT_ROWS, :] = jnp.ones((VT_ROWS - HEAD_W, v.shape[0]), BF16)


def _rope(x, cos, sin_a, sin_b):
    return (x * cos + pltpu.roll(x, LANES - 16, 1) * sin_a + pltpu.roll(x, 16, 1) * sin_b)


def _ada_kernel(cond_ref, w_ref, b_ref, o_ref):
    c = cond_ref[...]
    o_ref[0] = jnp.dot(_silu(c), w_ref[0], precision=HIGHEST,
                       preferred_element_type=F32) + b_ref[0]


def _ada_call(cond, w_ada, b_ada):
    depth, d, d3 = w_ada.shape
    tn = 1024
    return pl.pallas_call(
        _ada_kernel,
        grid=(depth, d3 // tn),
        in_specs=[pl.BlockSpec((8, d), lambda l, j: (0, 0)),
                  pl.BlockSpec((1, d, tn), lambda l, j: (l, 0, j)),
                  pl.BlockSpec((1, 1, tn), lambda l, j: (l, 0, j))],
        out_specs=pl.BlockSpec((1, 8, tn), lambda l, j: (l, 0, j)),
        out_shape=jax.ShapeDtypeStruct((depth, 8, d3), F32),
        compiler_params=_cparams(("parallel", "parallel")),
    )(cond, w_ada, b_ada.reshape(depth, 1, d3))


def _relayout_kernel(w_ref, o_ref, *, mg_off):
    n_ba = N_MIX + 4 * N_HEADS
    n_in = w_ref.shape[2]
    rows, n_out = o_ref.shape[1], o_ref.shape[2]
    o_ref[0, :, 0:N_MIX] = w_ref[0, :, 0:N_MIX].astype(BF16)
    tail = jnp.concatenate([w_ref[0, :, N_MIX:n_ba], jnp.zeros((rows, LANES - 4 * N_HEADS), F32)],
                           axis=1)
    o_ref[0, :, N_MIX:N_MIX + LANES] = tail.astype(BF16)
    o_ref[0, :, N_MIX + LANES:mg_off] = jnp.zeros((rows, mg_off - N_MIX - LANES), BF16)
    o_ref[0, :, mg_off:mg_off + n_in - n_ba] = w_ref[0, :, n_ba:n_in].astype(BF16)
    if n_out > mg_off + n_in - n_ba:
        o_ref[0, :, mg_off + n_in - n_ba:n_out] = jnp.zeros(
            (rows, n_out - mg_off - n_in + n_ba), BF16)


def _relayout_call(w_in, mg_off, n_cols):
    depth, d, n_in = w_in.shape
    tr = 256
    return pl.pallas_call(
        functools.partial(_relayout_kernel, mg_off=mg_off),
        grid=(depth, d // tr),
        in_specs=[pl.BlockSpec((1, tr, n_in), lambda l, i: (l, i, 0))],
        out_specs=pl.BlockSpec((1, tr, n_cols), lambda l, i: (l, i, 0)),
        out_shape=jax.ShapeDtypeStruct((depth, d, n_cols), BF16),
        compiler_params=_cparams(("parallel", "parallel")),
    )(w_in)


def _proj_kernel(x_ref, mod_ref, nw_ref, w_ref, o_ref, ba_ref, h_ref, *, d, ba_tile, ba_off):
    j = pl.program_id(1)

    @pl.when(j == 0)
    def _():
        x = x_ref[...]
        y = x * lax.rsqrt(jnp.mean(x * x, axis=-1, keepdims=True) + EPS) * nw_ref[...]
        mod = mod_ref[0]
        h_ref[...] = (y * (1.0 + mod[:, d:2 * d]) + mod[:, 0:d]).astype(BF16)

    acc = _dot(h_ref[...], w_ref[...])
    o_ref[...] = acc.astype(BF16)

    @pl.when(j == ba_tile)
    def _():
        ba_ref[...] = acc[:, ba_off:ba_off + LANES]


def _proj_call(x2, mod, mod_row, norm_w, w_pad, tm):
    n, d = x2.shape
    n_cols = w_pad.shape[1]
    tn = PROJ_TN
    return pl.pallas_call(
        functools.partial(_proj_kernel, d=d, ba_tile=OFF_BA // tn, ba_off=OFF_BA % tn),
        grid=(n // tm, n_cols // tn),
        in_specs=[pl.BlockSpec((tm, d), lambda i, j: (i, 0)),
                  pl.BlockSpec((1, 1, 3 * d), lambda i, j: (mod_row(i), 0, 0)),
                  pl.BlockSpec((1, d), lambda i, j: (0, 0)),
                  pl.BlockSpec((d, tn), lambda i, j: (0, j))],
        out_specs=[pl.BlockSpec((tm, tn), lambda i, j: (i, j)),
                   pl.BlockSpec((tm, LANES), lambda i, j: (i, 0))],
        out_shape=[jax.ShapeDtypeStruct((n, n_cols), BF16),
                   jax.ShapeDtypeStruct((n, LANES), F32)],
        scratch_shapes=[pltpu.VMEM((tm, d), BF16)],
        compiler_params=_cparams(("parallel", "arbitrary")),
    )(x2, mod, norm_w, w_pad)


def _prep_ctx_kernel(ak_ref, av_ref, bq_ref, bk_ref, wk_ref,
                     kall_ref, vall_ref, rq_ref, rk_ref, ka_ref, va_ref):
    wk = wk_ref[...]
    for h in range(N_HEADS):
        sl = slice(h * HEAD_W, (h + 1) * HEAD_W)
        kn = _norm_halves(ak_ref[:, sl].astype(F32)) * wk
        ka_ref[0, :, sl] = kn
        kall_ref[0, h] = kn.astype(BF16)
        v = av_ref[:, sl]
        va_ref[0, :, sl] = v.astype(F32)
        _store_values_t(vall_ref, h, v)
    rq_ref[...] = bq_ref[...].astype(F32)
    rk_ref[...] = bk_ref[...].astype(F32) * (DK_B ** -0.5)


def _prep_lat_kernel(ak_ref, av_ref, bq_ref, bk_ref, wk_ref, cos_ref, sa_ref, sb_ref,
                     ck_ref, cv_ref, kall_ref, vall_ref, rq_ref, rk_ref, *, nt):
    t = pl.program_id(1)

    @pl.when(t < nt)
    def _():
        wk = wk_ref[...]
        cos, sa, sb = cos_ref[...], sa_ref[...], sb_ref[...]
        for h in range(N_HEADS):
            sl = slice(h * HEAD_W, (h + 1) * HEAD_W)
            kn = _norm_halves(ak_ref[:, sl].astype(F32)) * wk
            kall_ref[0, h] = _rope(kn, cos, sa, sb).astype(BF16)
            _store_values_t(vall_ref, h, av_ref[:, sl])
        for e in range(2):
            sl = slice(e * LANES, (e + 1) * LANES)
            rq_ref[:, sl] = _rope(bq_ref[:, sl].astype(F32), cos, sa, sb)
            rk_ref[:, sl] = _rope(bk_ref[:, sl].astype(F32), cos, sa, sb) * (DK_B ** -0.5)

    @pl.when(t == nt)
    def _():
        for h in range(N_HEADS):
            sl = slice(h * HEAD_W, (h + 1) * HEAD_W)
            kall_ref[0, h] = ck_ref[:, sl].astype(BF16)
            _store_values_t(vall_ref, h, cv_ref[:, sl])


def _prep_call(p, b, t, wk_row, rope, cache, layer):
    tk = 256
    nt = t // tk
    n = b * t
    if rope is None:
        return pl.pallas_call(
            _prep_ctx_kernel,
            grid=(b, nt),
            in_specs=[pl.BlockSpec((tk, 512), lambda bi, ti: (bi * nt + ti, OFF_AK // 512)),
                      pl.BlockSpec((tk, 512), lambda bi, ti: (bi * nt + ti, OFF_AV // 512)),
                      pl.BlockSpec((tk, 256), lambda bi, ti: (bi * nt + ti, OFF_BQ // 256)),
                      pl.BlockSpec((tk, 256), lambda bi, ti: (bi * nt + ti, OFF_BK // 256)),
                      pl.BlockSpec((1, LANES), lambda bi, ti: (0, 0))],
            out_specs=[pl.BlockSpec((1, N_HEADS, tk, HEAD_W), lambda bi, ti: (bi, 0, ti, 0)),
                       pl.BlockSpec((1, N_HEADS, VT_ROWS, tk), lambda bi, ti: (bi, 0, 0, ti)),
                       pl.BlockSpec((tk, 256), lambda bi, ti: (bi * nt + ti, 0)),
                       pl.BlockSpec((tk, 256), lambda bi, ti: (bi * nt + ti, 0)),
                       pl.BlockSpec((1, tk, 512), lambda bi, ti: (bi, ti, 0)),
                       pl.BlockSpec((1, tk, 512), lambda bi, ti: (bi, ti, 0))],
            out_shape=[jax.ShapeDtypeStruct((b, N_HEADS, t, HEAD_W), BF16),
                       jax.ShapeDtypeStruct((b, N_HEADS, VT_ROWS, t), BF16),
                       jax.ShapeDtypeStruct((n, 256), F32),
                       jax.ShapeDtypeStruct((n, 256), F32),
                       jax.ShapeDtypeStruct((b, t, 512), F32),
                       jax.ShapeDtypeStruct((b, t, 512), F32)],
            compiler_params=_cparams(("parallel", "parallel")),
        )(p, p, p, p, wk_row)
    cos, sa, sb = rope
    cache_k, cache_v = cache
    past = cache_k.shape[2]
    assert past == tk
    s = t + past

    def tok(bi, ti):
        return bi * nt + jnp.minimum(ti, nt - 1)

    return pl.pallas_call(
        functools.partial(_prep_lat_kernel, nt=nt),
        grid=(b, nt + 1),
        in_specs=[pl.BlockSpec((tk, 512), lambda bi, ti: (tok(bi, ti), OFF_AK // 512)),
                  pl.BlockSpec((tk, 512), lambda bi, ti: (tok(bi, ti), OFF_AV // 512)),
                  pl.BlockSpec((tk, 256), lambda bi, ti: (tok(bi, ti), OFF_BQ // 256)),
                  pl.BlockSpec((tk, 256), lambda bi, ti: (tok(bi, ti), OFF_BK // 256)),
                  pl.BlockSpec((1, LANES), lambda bi, ti: (0, 0)),
                  pl.BlockSpec((tk, LANES), lambda bi, ti: (jnp.minimum(ti, nt - 1), 0)),
                  pl.BlockSpec((tk, LANES), lambda bi, ti: (jnp.minimum(ti, nt - 1), 0)),
                  pl.BlockSpec((tk, LANES), lambda bi, ti: (jnp.minimum(ti, nt - 1), 0)),
                  pl.BlockSpec((None, None, past, 512), lambda bi, ti: (bi, layer, 0, 0)),
                  pl.BlockSpec((None, None, past, 512), lambda bi, ti: (bi, layer, 0, 0))],
        out_specs=[pl.BlockSpec((1, N_HEADS, tk, HEAD_W), lambda bi, ti: (bi, 0, ti, 0)),
                   pl.BlockSpec((1, N_HEADS, VT_ROWS, tk), lambda bi, ti: (bi, 0, 0, ti)),
                   pl.BlockSpec((tk, 256), lambda bi, ti: (tok(bi, ti), 0)),
                   pl.BlockSpec((tk, 256), lambda bi, ti: (tok(bi, ti), 0))],
        out_shape=[jax.ShapeDtypeStruct((b, N_HEADS, s, HEAD_W), BF16),
                   jax.ShapeDtypeStruct((b, N_HEADS, VT_ROWS, s), BF16),
                   jax.ShapeDtypeStruct((n, 256), F32),
                   jax.ShapeDtypeStruct((n, 256), F32)],
        compiler_params=_cparams(("parallel", "arbitrary")),
    )(p, p, p, p, wk_row, cos, sa, sb, cache_k, cache_v)


def _attn_kernel(*refs, lam_init, rope):
    if rope:
        (q_ref, z_ref, k_ref, v_ref, wq_ref, dl_ref, sw_ref, cos_ref, sa_ref, sb_ref,
         o_ref, s_ref, p_ref) = refs
    else:
        q_ref, z_ref, k_ref, v_ref, wq_ref, dl_ref, sw_ref, o_ref, s_ref, p_ref = refs
    n_keys = k_ref.shape[2]
    tq = q_ref.shape[0]
    lv = dl_ref[...]
    lam = (jnp.exp(jnp.sum(lv[0:1] * lv[1:2], axis=-1, keepdims=True))
           - jnp.exp(jnp.sum(lv[2:3] * lv[3:4], axis=-1, keepdims=True)) + lam_init)
    chains = [(r0, m) for r0 in range(0, tq, ATTN_SUB) for m in range(2)]
    slot = {chain: i % ATTN_SLOTS for i, chain in enumerate(chains)}
    q_t = {}
    col_max = {}
    den = {}
    acc = {}

    def stage_scores(r0, m):
        rows = slice(r0, r0 + ATTN_SUB)
        if r0 not in q_t:
            qn = _norm_halves(q_ref[rows, :].astype(F32)) * wq_ref[...]
            if rope:
                qn = _rope(qn, cos_ref[rows, :], sa_ref[rows, :], sb_ref[rows, :])
            q_t[r0] = (qn * (DK_A ** -0.5 * LOG2E)).T
        qt = q_t[r0]
        lo = _row_iota(qt.shape) < DK_A
        qm = jnp.where(lo if m == 0 else jnp.logical_not(lo), qt, 0.0).astype(BF16)
        for c0 in range(0, n_keys, ATTN_KEYS):
            s_ref[slot[r0, m], c0:c0 + ATTN_KEYS, :] = _dot(k_ref[0, 0, c0:c0 + ATTN_KEYS, :], qm)

    def stage_softmax(r0, m):
        rows = slice(r0, r0 + ATTN_SUB)
        sl = slot[r0, m]
        mx = s_ref[sl, 0:8, :]
        for j in range(1, n_keys // 8):
            mx = jnp.maximum(mx, s_ref[sl, j * 8:(j + 1) * 8, :])
        mx = jnp.max(mx, axis=0, keepdims=True)
        for c0 in range(0, n_keys, ATTN_KEYS):
            ks = slice(c0, c0 + ATTN_KEYS)
            p_ref[sl, ks, :] = jnp.exp2(s_ref[sl, ks, :] - mx).astype(BF16)

    def stage_values(r0, m):
        rows = slice(r0, r0 + ATTN_SUB)
        pv = _dot(v_ref[0, 0], p_ref[slot[r0, m]])
        acc[r0, m] = pv[0:HEAD_W, :]
        den[r0, m] = pv[HEAD_W:HEAD_W + 1, :]
        if m == 1:
            o = (acc[r0, 0] * (1.0 / den[r0, 0]) - lam * (acc[r0, 1] * (1.0 / den[r0, 1])))
            o = o.T
            o = o * lax.rsqrt(jnp.mean(o * o, axis=-1, keepdims=True) + EPS) * sw_ref[...]
            o_ref[rows, :] = o * (1.0 - lam_init) * _silu(z_ref[rows, :].astype(F32))

    for step in range(len(chains) + 2):
        if 0 <= step - 2 < len(chains):
            stage_values(*chains[step - 2])
        if 0 <= step - 1 < len(chains):
            stage_softmax(*chains[step - 1])
        if step < len(chains):
            stage_scores(*chains[step])


def _attn_call(p, kall, vall, wq_row, dlam, subln_row, rope, b, t, lam_init):
    tq = min(ATTN_TQ, t)
    s = kall.shape[2]
    nq = t // tq
    in_specs = [pl.BlockSpec((tq, HEAD_W), lambda bi, h, i: (bi * nq + i, OFF_AQ // HEAD_W + h)),
                pl.BlockSpec((tq, HEAD_W), lambda bi, h, i: (bi * nq + i, OFF_AZ // HEAD_W + h)),
                pl.BlockSpec((1, 1, s, HEAD_W), lambda bi, h, i: (bi, h, 0, 0)),
                pl.BlockSpec((1, 1, VT_ROWS, s), lambda bi, h, i: (bi, h, 0, 0)),
                pl.BlockSpec((1, LANES), lambda bi, h, i: (0, 0)),
                pl.BlockSpec((4, DK_A), lambda bi, h, i: (0, 0)),
                pl.BlockSpec((1, LANES), lambda bi, h, i: (0, 0))]
    args = [p, p, kall, vall, wq_row, dlam, subln_row]
    if rope is not None:
        in_specs += [pl.BlockSpec((tq, LANES), lambda bi, h, i: (i, 0))] * 3
        args += list(rope)
    return pl.pallas_call(
        functools.partial(_attn_kernel, lam_init=lam_init, rope=rope is not None),
        grid=(b, N_HEADS, nq),
        in_specs=in_specs,
        out_specs=pl.BlockSpec((tq, HEAD_W), lambda bi, h, i: (bi * nq + i, h)),
        out_shape=jax.ShapeDtypeStruct((b * t, N_HEADS * HEAD_W), F32),
        scratch_shapes=[pltpu.VMEM((ATTN_SLOTS, s, ATTN_SUB), F32),
                        pltpu.VMEM((ATTN_SLOTS, s, ATTN_SUB), BF16)],
        compiler_params=_cparams(("parallel", "parallel", "arbitrary")),
    )(*args)


def _ret_kernel(*refs, nchunk, has_state):
    if has_state:
        (q_ref, k_ref, v_ref, z_ref, d256_ref, d128_ref, nw_ref, s0_ref,
         o_ref, s_ref, dm_ref, qd_ref, kd_ref, ob_ref) = refs
        st_ref = None
    else:
        (q_ref, k_ref, v_ref, z_ref, d256_ref, d128_ref, nw_ref,
         o_ref, st_ref, s_ref, dm_ref, qd_ref, kd_ref, ob_ref) = refs
    c_len = CHUNK
    lg256 = -_softplus(-d256_ref[...])
    lg128 = -_softplus(-d128_ref[...])
    ri = _row_iota((c_len, c_len)).astype(F32)
    ci = _lane_iota((c_len, c_len)).astype(F32)
    rcol256 = _row_iota((c_len, 2 * HEAD_W)).astype(F32)
    rcol128 = _row_iota((c_len, LANES)).astype(F32)
    blockmask = ((_row_iota((2 * DK_B, 2 * HEAD_W)) < DK_B)
                 == (_lane_iota((2 * DK_B, 2 * HEAD_W)) < HEAD_W))
    for d in range(2):
        rel = (ri - ci) if d == 0 else (ci - ri)
        keep = rel >= 0
        for e in range(2):
            lg = lg256[d:d + 1, e * HEAD_W:e * HEAD_W + 1]
            dm_ref[d, e] = jnp.where(keep, jnp.exp(jnp.where(keep, rel, 0.0) * lg), 0.0)
        qpow = (rcol256 + 1.0) if d == 0 else (c_len - rcol256)
        qd_ref[d] = jnp.exp(qpow * lg256[d:d + 1])
        kpow = (c_len - 1.0 - rcol128) if d == 0 else rcol128
        kd_ref[d] = jnp.exp(kpow * lg128[d:d + 1])
        if has_state:
            s_ref[d] = jnp.zeros((2 * DK_B, 2 * HEAD_W), F32)
            for e in range(2):
                s_ref[d, e * DK_B:(e + 1) * DK_B, e * HEAD_W:(e + 1) * HEAD_W] = s0_ref[d, e]
        else:
            s_ref[d] = jnp.zeros((2 * DK_B, 2 * HEAD_W), F32)
    cdec = [jnp.exp(float(c_len) * lg256[d:d + 1]) for d in range(2)]
    lane_lo = _lane_iota((c_len, LANES)) < DK_B
    nw = nw_ref[...]

    unroll = min(RET_UNROLL, nchunk)
    assert nchunk % unroll == 0

    def scan_body(i, carry):
        items = [(d, pl.multiple_of((i * unroll + u if d == 0 else nchunk - 1 - i * unroll - u)
                                    * c_len, c_len))
                 for u in range(unroll) for d in range(2)]
        q = [q_ref[pl.ds(t0, c_len), :] for _, t0 in items]
        k = [k_ref[pl.ds(t0, c_len), :] for _, t0 in items]
        vb = [v_ref[pl.ds(t0, c_len), :] for _, t0 in items]
        kb = [x.astype(BF16) for x in k]
        heads = [(n, e) for n in range(len(items)) for e in range(2)]
        qe = [jnp.where(lane_lo if e == 0 else jnp.logical_not(lane_lo), q[n], 0.0).astype(BF16)
              for n, e in heads]
        sc = [_dot_nt(qe_i, kb[n]) * dm_ref[items[n][0], e] for qe_i, (n, e) in zip(qe, heads)]
        kv = [lax.dot_general((k[n] * kd_ref[d]).astype(BF16), vb[n], (((0,), (0,)), ((), ())),
                              preferred_element_type=F32) for n, (d, _) in enumerate(items)]
        intra = [_dot(sc_i.astype(BF16), vb[n][:, e * HEAD_W:(e + 1) * HEAD_W])
                 for sc_i, (n, e) in zip(sc, heads)]
        st = [s_ref[d] for d in range(2)]
        for n, (d, t0) in enumerate(items):
            inter = _dot(q[n].astype(BF16), st[d].astype(BF16)) * qd_ref[d]
            dst = o_ref if d == 0 else ob_ref
            for e in range(2):
                sl = slice(e * HEAD_W, (e + 1) * HEAD_W)
                dst[pl.ds(t0, c_len), sl] = intra[2 * n + e] + inter[:, sl]
            st[d] = jnp.where(blockmask, st[d] * cdec[d] + kv[n], 0.0)
        for d in range(2):
            s_ref[d] = st[d]
        return carry

    lax.fori_loop(0, nchunk // unroll, scan_body, 0)

    def fin_body(i, carry):
        t0 = pl.multiple_of(i * c_len, c_len)
        for e in range(2):
            sl = slice(e * HEAD_W, (e + 1) * HEAD_W)
            o_e = o_ref[pl.ds(t0, c_len), sl] + ob_ref[pl.ds(t0, c_len), sl]
            o_e = o_e * lax.rsqrt(jnp.mean(o_e * o_e, axis=-1, keepdims=True) + EPS) * nw
            o_ref[pl.ds(t0, c_len), sl] = o_e * _silu(z_ref[pl.ds(t0, c_len), sl].astype(F32))
        return carry

    lax.fori_loop(0, nchunk, fin_body, 0)
    if not has_state:
        for d in range(2):
            for e in range(2):
                st_ref[d, e] = s_ref[d, e * DK_B:(e + 1) * DK_B, e * HEAD_W:(e + 1) * HEAD_W]


def _ret_call(p, rq, rk, d256, d128, nw_row, state, b, t, layer):
    nchunk = t // CHUNK
    hp = N_HEADS // 2
    in_specs = [pl.BlockSpec((t, LANES), lambda bi, h: (bi, h)),
                pl.BlockSpec((t, LANES), lambda bi, h: (bi, h)),
                pl.BlockSpec((t, 256), lambda bi, h: (bi, OFF_BV // 256 + h)),
                pl.BlockSpec((t, 256), lambda bi, h: (bi, OFF_BZ // 256 + h)),
                pl.BlockSpec((None, 2, 256), lambda bi, h: (h, 0, 0)),
                pl.BlockSpec((None, 2, LANES), lambda bi, h: (h, 0, 0)),
                pl.BlockSpec((1, LANES), lambda bi, h: (0, 0))]
    args = [rq, rk, p, p, d256, d128, nw_row]
    scratch = [pltpu.VMEM((2, 2 * DK_B, 2 * HEAD_W), F32),
               pltpu.VMEM((2, 2, CHUNK, CHUNK), F32),
               pltpu.VMEM((2, CHUNK, 2 * HEAD_W), F32),
               pltpu.VMEM((2, CHUNK, LANES), F32),
               pltpu.VMEM((t, 2 * HEAD_W), F32)]
    o_spec = pl.BlockSpec((t, 256), lambda bi, h: (bi, h))
    o_shape = jax.ShapeDtypeStruct((b * t, N_HEADS * HEAD_W), F32)
    if state is not None:
        in_specs.append(pl.BlockSpec((None, None, 2, 2, DK_B, HEAD_W),
                                     lambda bi, h: (bi, layer, 0, h, 0, 0)))
        args.append(state)
        out_specs, out_shape = o_spec, o_shape
    else:
        out_specs = [o_spec, pl.BlockSpec((None, 2, 2, DK_B, HEAD_W), lambda bi, h: (bi, 0, h, 0, 0))]
        out_shape = [o_shape, jax.ShapeDtypeStruct((b, 2, N_HEADS, DK_B, HEAD_W), F32)]
    return pl.pallas_call(
        functools.partial(_ret_kernel, nchunk=nchunk, has_state=state is not None),
        grid=(b, hp),
        in_specs=in_specs,
        out_specs=out_specs,
        out_shape=out_shape,
        scratch_shapes=scratch,
        compiler_params=_cparams(("parallel", "parallel")),
    )(*args)


def _gdn_kernel(*refs, t, nh, has_state):
    if has_state:
        (cq_ref, ck_ref, cv_ref, cz_ref, ba_ref, wq_ref, wk_ref, wv_ref, al_ref, dt_ref, nw_ref,
         s0_ref, o_ref, qs_ref, ks_ref, vs_ref, u_ref, w_ref, qk_ref, qg_ref, kdt_ref, et_ref,
         ob_ref, s_ref, xw_ref) = refs
        st_ref = None
    else:
        (cq_ref, ck_ref, cv_ref, cz_ref, ba_ref, wq_ref, wk_ref, wv_ref, al_ref, dt_ref, nw_ref,
         o_ref, st_ref, qs_ref, ks_ref, vs_ref, u_ref, w_ref, qk_ref, qg_ref, kdt_ref, et_ref,
         ob_ref, s_ref, xw_ref) = refs
    c_len = CHUNK
    nchunk = t // c_len
    head0 = pl.program_id(1) * nh
    pad = 16
    win = 256
    nwin = t // win

    def conv_window(r0, first, last):
        lo = 0 if first else pad
        hi = 0 if last else pad
        start = r0 - lo if isinstance(r0, int) else pl.multiple_of(r0 - lo, pad)
        for hh, (a, (src, cw_ref, dst)) in itertools.product(
                range(nh), enumerate(((cq_ref, wq_ref, qs_ref), (ck_ref, wk_ref, ks_ref),
                                      (cv_ref, wv_ref, vs_ref)))):
            hl = slice(hh * HEAD_W, (hh + 1) * HEAD_W)
            xw = src[pl.ds(start, win + lo + hi), hl].astype(F32)
            if first:
                xw = jnp.concatenate([jnp.zeros((pad, LANES), F32), xw], axis=0)
            if last:
                xw = jnp.concatenate([xw, jnp.zeros((pad, LANES), F32)], axis=0)
            cw = cw_ref[:, hl]
            xw_ref[...] = xw
            acc = None
            for j in range(CONV_K):
                off = pad - CONV_K // 2 + j
                term = xw_ref[off:off + win, :] * cw[j:j + 1, :]
                acc = term if acc is None else acc + term
            y = _silu(acc)
            if a < 2:
                y = y * lax.rsqrt(jnp.sum(y * y, axis=-1, keepdims=True) + EPS)
            if a == 0:
                y = y * (HEAD_W ** -0.5)
            dst[hh, pl.ds(r0, win), :] = y

    if nwin == 1:
        conv_window(0, True, True)
    else:
        conv_window(0, True, False)

        def conv_body(i, carry):
            conv_window(pl.multiple_of(i * win, win), False, False)
            return carry

        lax.fori_loop(1, nwin - 1, conv_body, 0)
        conv_window((nwin - 1) * win, False, True)

    ri = _row_iota((c_len, c_len))
    ci = _lane_iota((c_len, c_len))
    lane = _lane_iota((c_len, LANES))
    alog = al_ref[...]
    dtb = dt_ref[...]
    tri16 = jnp.where(ri >= ci, 1.0, 0.0).astype(BF16)
    g_lanes = jnp.logical_and(lane >= 2 * N_HEADS, lane < 4 * N_HEADS)

    def lanes3(tgt):
        return jnp.logical_or(lane == tgt, jnp.logical_or(lane == tgt + 16, lane == tgt + 32))

    def chunk_chains(c):
        t0 = pl.multiple_of(c * c_len, c_len)
        ba = ba_ref[pl.ds(t0, c_len), :]
        beta_all = _sigmoid(ba)
        g_all = jnp.where(g_lanes, -jnp.exp(alog) * _softplus(ba + dtb), 0.0)
        g_hi = g_all.astype(BF16).astype(F32)
        r1 = g_all - g_hi
        g_mid = r1.astype(BF16).astype(F32)
        g_lo = (r1 - g_mid).astype(BF16).astype(F32)
        parts = (g_hi + pltpu.roll(g_mid, 16, 1) + pltpu.roll(g_lo, 32, 1)).astype(BF16)
        pre = _dot(tri16, parts)
        chains = []
        for hh in range(nh):
            h = head0 + hh
            tgt_f = 2 * N_HEADS + h
            tgt_b = 3 * N_HEADS + h
            gc_f = jnp.sum(jnp.where(lanes3(tgt_f), pre, 0.0), axis=-1, keepdims=True)
            pre_b = jnp.sum(jnp.where(lanes3(tgt_b), pre, 0.0), axis=-1, keepdims=True)
            g_b = jnp.sum(jnp.where(lane == tgt_b, g_all, 0.0), axis=-1, keepdims=True)
            gc_b = pre_b[c_len - 1:c_len, :] - pre_b + g_b
            beta_f = jnp.sum(jnp.where(lane == h, beta_all, 0.0), axis=-1, keepdims=True)
            beta_b = jnp.sum(jnp.where(lane == N_HEADS + h, beta_all, 0.0), axis=-1,
                             keepdims=True)
            rows = jnp.where(lane == 0, gc_f, jnp.where(lane == 1, gc_b, 0.0)).T

            q = qs_ref[hh, pl.ds(t0, c_len), :]
            k = ks_ref[hh, pl.ds(t0, c_len), :]
            v = vs_ref[hh, pl.ds(t0, c_len), :]
            k16 = k.astype(BF16)
            kkqk = _dot_nt(jnp.concatenate([k16, q.astype(BF16)], axis=0), k16)
            kk = kkqk[0:c_len, :]
            qk_raw = kkqk[c_len:2 * c_len, :]
            for d, (gc, beta) in enumerate(((gc_f, beta_f), (gc_b, beta_b))):
                gr = rows[d:d + 1, :]
                incl = (ri >= ci) if d == 0 else (ci >= ri)
                strict = (ri > ci) if d == 0 else (ci > ri)
                ex = jnp.exp(jnp.where(incl, gc - gr, 0.0))
                a_mat = jnp.where(strict, ex, 0.0) * kk * beta
                egc = jnp.exp(gc)
                rhs = jnp.concatenate([v * beta, k * (beta * egc)], axis=1)
                qk_ref[hh, d, pl.ds(t0, c_len), :] = (jnp.where(incl, ex, 0.0)
                                                      * qk_raw).astype(BF16)
                qg_ref[hh, d, pl.ds(t0, c_len), :] = (q * egc).astype(BF16)
                g_tot = gc[c_len - 1:c_len, :] if d == 0 else gc[0:1, :]
                kdt_ref[hh, d, pl.ds(t0, c_len), :] = (k * jnp.exp(g_tot - gc)).T.astype(BF16)
                et_ref[hh, d, c] = jnp.broadcast_to(jnp.exp(g_tot), (8, LANES))
                chains.append((a_mat, rhs, (hh, d, t0)))
        return chains

    group = min(GDN_GROUP, nchunk)
    assert nchunk % group == 0

    def prep_body(i, carry):
        chains = []
        for j in range(group):
            chains += chunk_chains(i * group + j)
        n_mats = [-jnp.where((ri >> 1) == (ci >> 1), a_mat, 0.0) for a_mat, _, _ in chains]
        sh = 1
        while (1 << sh) < c_len:
            off = jnp.logical_and((ri >> (sh + 1)) == (ci >> (sh + 1)), (ri >> sh) != (ci >> sh))
            l_mats = [jnp.where(off, a_mat, 0.0) for a_mat, _, _ in chains]
            x_mats = [l + _dot(l.astype(BF16), n.astype(BF16)) for l, n in zip(l_mats, n_mats)]
            n_mats = [n - x - _dot(n.astype(BF16), x.astype(BF16)) for n, x in zip(n_mats, x_mats)]
            sh += 1
        for n_mat, (_, rhs, (hh, d, t0)) in zip(n_mats, chains):
            uw = rhs + _dot(n_mat.astype(BF16), rhs.astype(BF16))
            u_ref[hh, d, pl.ds(t0, c_len), :] = uw[:, 0:HEAD_W]
            w_ref[hh, d, pl.ds(t0, c_len), :] = uw[:, HEAD_W:2 * HEAD_W].astype(BF16)
        return carry

    lax.fori_loop(0, nchunk // group, prep_body, 0)

    for hh, d in itertools.product(range(nh), range(2)):
        s_ref[hh, d] = s0_ref[d, hh] if has_state else jnp.zeros((HEAD_W, HEAD_W), F32)

    def scan_body(i, carry):
        chains = [(hh, d, c, pl.multiple_of(c * c_len, c_len))
                  for hh, (d, c) in itertools.product(range(nh), ((0, i), (1, nchunk - 1 - i)))]
        st = [s_ref[hh, d] for hh, d, _, _ in chains]
        st16 = [x.astype(BF16) for x in st]
        ws = [_dot(w_ref[hh, d, pl.ds(t0, c_len), :], s16)
              for (hh, d, _, t0), s16 in zip(chains, st16)]
        qs = [_dot(qg_ref[hh, d, pl.ds(t0, c_len), :], s16)
              for (hh, d, _, t0), s16 in zip(chains, st16)]
        vn16 = [(u_ref[hh, d, pl.ds(t0, c_len), :] - x).astype(BF16)
                for (hh, d, _, t0), x in zip(chains, ws)]
        kv = [_dot(kdt_ref[hh, d, pl.ds(t0, c_len), :], v16)
              for (hh, d, _, t0), v16 in zip(chains, vn16)]
        for (hh, d, c, _), s_old, x in zip(chains, st, kv):
            s_ref[hh, d] = s_old * et_ref[hh, d, c][0:1, :] + x
        for (hh, d, _, t0), x, v16 in zip(chains, qs, vn16):
            o = x + _dot(qk_ref[hh, d, pl.ds(t0, c_len), :], v16)
            dst = o_ref if d == 0 else ob_ref
            dst[pl.ds(t0, c_len), hh * HEAD_W:(hh + 1) * HEAD_W] = o
        return carry

    lax.fori_loop(0, nchunk, scan_body, 0)

    def fin_body(i, carry):
        r0 = pl.multiple_of(i * win, win)
        for hh in range(nh):
            hl = slice(hh * HEAD_W, (hh + 1) * HEAD_W)
            o = o_ref[pl.ds(r0, win), hl] + ob_ref[pl.ds(r0, win), hl]
            o = o * lax.rsqrt(jnp.mean(o * o, axis=-1, keepdims=True) + EPS) * nw_ref[...]
            o_ref[pl.ds(r0, win), hl] = o * _silu(cz_ref[pl.ds(r0, win), hl].astype(F32))
        return carry

    lax.fori_loop(0, nwin, fin_body, 0)
    if not has_state:
        for hh, d in itertools.product(range(nh), range(2)):
            st_ref[d, hh] = s_ref[hh, d]


def _gdn_heads_per_program(t):
    per_head = t * LANES * (4 * 2 * 2 + 4 * 2 + 3 * 4 + 2 * 4 + 4 * 2 * 2 + 4)
    nh = N_HEADS
    while nh > 1 and nh * per_head > GDN_VMEM_BUDGET:
        nh //= 2
    return nh


def _gdn_call(p, p_ba, conv_w, al_row, dt_row, nw_row, state, b, t, layer):
    nh = _gdn_heads_per_program(t)
    wd = nh * HEAD_W

    def col(off):
        return lambda bi, h: (bi, off // wd + h)

    in_specs = [pl.BlockSpec((t, wd), col(OFF_CQ)),
                pl.BlockSpec((t, wd), col(OFF_CK)),
                pl.BlockSpec((t, wd), col(OFF_CV)),
                pl.BlockSpec((t, wd), col(OFF_CZ)),
                pl.BlockSpec((t, LANES), lambda bi, h: (bi, 0)),
                pl.BlockSpec((CONV_K, wd), lambda bi, h: (0, h)),
                pl.BlockSpec((CONV_K, wd), lambda bi, h: (0, N_HEADS // nh + h)),
                pl.BlockSpec((CONV_K, wd), lambda bi, h: (0, 2 * (N_HEADS // nh) + h)),
                pl.BlockSpec((1, LANES), lambda bi, h: (0, 0)),
                pl.BlockSpec((1, LANES), lambda bi, h: (0, 0)),
                pl.BlockSpec((1, LANES), lambda bi, h: (0, 0))]
    args = [p, p, p, p, p_ba, conv_w, conv_w, conv_w, al_row, dt_row, nw_row]
    scratch = [pltpu.VMEM((nh, t, LANES), F32),
               pltpu.VMEM((nh, t, LANES), F32),
               pltpu.VMEM((nh, t, LANES), F32),
               pltpu.VMEM((nh, 2, t, LANES), F32),
               pltpu.VMEM((nh, 2, t, LANES), BF16),
               pltpu.VMEM((nh, 2, t, LANES), BF16),
               pltpu.VMEM((nh, 2, t, LANES), BF16),
               pltpu.VMEM((nh, 2, t, LANES), BF16),
               pltpu.VMEM((nh, 2, t // CHUNK, 8, LANES), F32),
               pltpu.VMEM((t, wd), F32),
               pltpu.VMEM((nh, 2, HEAD_W, HEAD_W), F32),
               pltpu.VMEM((256 + 2 * 16, LANES), F32)]
    o_spec = pl.BlockSpec((t, wd), lambda bi, h: (bi, h))
    o_shape = jax.ShapeDtypeStruct((b * t, N_HEADS * HEAD_W), F32)
    if state is not None:
        in_specs.append(pl.BlockSpec((None, None, 2, nh, HEAD_W, HEAD_W),
                                     lambda bi, h: (bi, layer, 0, h, 0, 0)))
        args.append(state)
        out_specs, out_shape = o_spec, o_shape
    else:
        out_specs = [o_spec, pl.BlockSpec((None, 2, nh, HEAD_W, HEAD_W),
                                          lambda bi, h: (bi, 0, h, 0, 0))]
        out_shape = [o_shape, jax.ShapeDtypeStruct((b, 2, N_HEADS, HEAD_W, HEAD_W), F32)]
    return pl.pallas_call(
        functools.partial(_gdn_kernel, t=t, nh=nh, has_state=state is not None),
        grid=(b, N_HEADS // nh),
        in_specs=in_specs,
        out_specs=out_specs,
        out_shape=out_shape,
        scratch_shapes=scratch,
        compiler_params=_cparams(("parallel", "parallel")),
    )(*args)


def _out_kernel(ya_ref, yb_ref, yc_ref, mg_ref, x_ref, mod_ref, wb_ref, wo_ref, o_ref, *, d):
    ya = _dot(ya_ref[...].astype(BF16), wb_ref[0])
    yb = _dot(yb_ref[...].astype(BF16), wb_ref[1])
    yc = _dot(yc_ref[...].astype(BF16), wb_ref[2])
    y = (_sigmoid(mg_ref[:, 0:d].astype(F32)) * ya + _sigmoid(mg_ref[:, d:2 * d].astype(F32)) * yb
         + _sigmoid(mg_ref[:, 2 * d:3 * d].astype(F32)) * yc)
    out = _dot(y.astype(BF16), wo_ref[...])
    gate = mod_ref[0][:, 2 * d:3 * d]
    o_ref[...] = x_ref[...] + gate * out


def _out_call(ya, yb, yc, p, x2, mod, mod_row, wb, wo, mg_block, tm):
    n, d = x2.shape
    w_br = ya.shape[1]
    return pl.pallas_call(
        functools.partial(_out_kernel, d=d),
        grid=(n // tm,),
        in_specs=[pl.BlockSpec((tm, w_br), lambda i: (i, 0)),
                  pl.BlockSpec((tm, w_br), lambda i: (i, 0)),
                  pl.BlockSpec((tm, w_br), lambda i: (i, 0)),
                  pl.BlockSpec((tm, 3 * d), lambda i: (i, mg_block)),
                  pl.BlockSpec((tm, d), lambda i: (i, 0)),
                  pl.BlockSpec((1, 1, 3 * d), lambda i: (mod_row(i), 0, 0)),
                  pl.BlockSpec((3, w_br, d), lambda i: (0, 0, 0)),
                  pl.BlockSpec((d, d), lambda i: (0, 0))],
        out_specs=pl.BlockSpec((tm, d), lambda i: (i, 0)),
        out_shape=jax.ShapeDtypeStruct((n, d), F32),
        compiler_params=_cparams(("parallel",)),
    )(ya, yb, yc, p, x2, mod, wb, wo)


def _rope_tables(n_tokens, dtype):
    n_rows = n_tokens // GRID_W
    row = jnp.repeat(jnp.arange(n_rows, dtype=jnp.float32), GRID_W)
    col = jnp.tile(jnp.arange(GRID_W, dtype=jnp.float32), n_rows)
    n_freq = DK_A // 4
    inv = 1.0 / (ROPE_BASE ** (jnp.arange(n_freq, dtype=jnp.float32) / n_freq))
    ar = row[:, None] * inv
    ac = col[:, None] * inv
    ang = jnp.concatenate([ar, ar, ac, ac], axis=-1)
    cos = jnp.tile(jnp.cos(ang).astype(dtype), (1, 2))
    sin = jnp.tile(jnp.sin(ang).astype(dtype), (1, 2))
    first = (jnp.arange(LANES) % 32) < 16
    sin_a = jnp.where(first, -sin, 0.0)
    sin_b = jnp.where(first, 0.0, sin)
    return cos, sin_a, sin_b


PROJ_TM = 2048
PROJ_TN = 1536
OUT_TM = 512


def _layer(x2, mod, mod_row_for, layer, wts, rope, ctx, b, t):
    p, p_ba = _proj_call(x2, mod, mod_row_for(PROJ_TM), wts["norm_w"], wts["w_in"], PROJ_TM)
    cache = None if ctx is None else (ctx[0], ctx[1])
    prep = _prep_call(p, b, t, wts["wk_row"], rope, cache, layer)
    kall, vall, rq, rk = prep[:4]
    lam_init = 0.8 - 0.6 * math.exp(-0.3 * layer)
    ya = _attn_call(p, kall, vall, wts["wq_row"], wts["diff_lambda"], wts["subln_row"], rope,
                    b, t, lam_init)
    ret = _ret_call(p, rq, rk, wts["d256"], wts["d128"], wts["ret_norm_row"],
                    None if ctx is None else ctx[2], b, t, layer)
    gdn = _gdn_call(p, p_ba, wts["conv_w"], wts["al_row"], wts["dt_row"], wts["gdn_norm_row"],
                    None if ctx is None else ctx[3], b, t, layer)
    if ctx is None:
        yb, s_ret = ret
        yc, s_gdn = gdn
        extras = (prep[4], prep[5], s_ret, s_gdn)
    else:
        yb, yc = ret, gdn
        extras = None
    mg_block = wts["mg_off"] // (3 * x2.shape[1])
    x2 = _out_call(ya, yb, yc, p, x2, mod, mod_row_for(OUT_TM), wts["w_branch"], wts["w_out"],
                   mg_block, OUT_TM)
    return x2, extras


def kernel(x_prompt, x_sample, cache_attn_k, cache_attn_v, state_ret, state_gdn, c, c_ctx,
           norm_w, w_ada, b_ada, w_in, qk_norm_w, diff_lambda, subln_w, ret_decay, ret_norm_w,
           conv_w, gdn_a_log, gdn_dt_bias, gdn_norm_w, w_branch, w_out):
    b_ctx, t_ctx, d = x_prompt.shape
    b_lat, t_lat, _ = x_sample.shape
    depth = w_in.shape[0]
    past = cache_attn_k.shape[2]
    assert b_lat <= 4 and d % LANES == 0

    cond = jnp.zeros((8, d), F32).at[:b_lat].set(c).at[4].set(c_ctx)
    mods = _ada_call(cond, w_ada, b_ada)

    mg_off = -(-(OFF_BA + LANES) // (3 * d)) * (3 * d)
    n_cols = mg_off + 3 * d
    n_cols = -(-n_cols // PROJ_TN) * PROJ_TN
    w_pad = _relayout_call(w_in, mg_off, n_cols)

    rope = _rope_tables(t_lat, x_sample.dtype)
    cache_k = cache_attn_k.reshape(b_lat, depth, past, N_HEADS * 2 * DK_A)
    cache_v = cache_attn_v.reshape(b_lat, depth, past, N_HEADS * HEAD_W)

    lanes16 = jnp.zeros((depth, LANES), F32)
    al_rows = lanes16.at[:, 8:16].set(gdn_a_log.reshape(depth, 8))
    dt_rows = lanes16.at[:, 8:16].set(gdn_dt_bias.reshape(depth, 8))
    dec = ret_decay.reshape(depth, 2, 2, 2)
    dec = jnp.transpose(dec, (0, 2, 1, 3))
    d256 = jnp.repeat(dec, HEAD_W, axis=-1)
    d128 = jnp.repeat(dec, DK_B, axis=-1)

    y_p = x_prompt.reshape(b_ctx * t_ctx, d)
    y_s = x_sample.reshape(b_lat * t_lat, d)
    assert t_lat % PROJ_TM == 0 and (b_ctx * t_ctx) % PROJ_TM == 0

    def ctx_row(tm):
        return lambda i: 4

    def lat_row(tm):
        return lambda i: i // (t_lat // tm)

    ks, vs, rs, gs = [], [], [], []
    for l in range(depth):
        wts = {
            "norm_w": norm_w[l].reshape(1, d),
            "w_in": w_pad[l],
            "mg_off": mg_off,
            "wq_row": jnp.tile(qk_norm_w[l, 0], 2).reshape(1, LANES),
            "wk_row": jnp.tile(qk_norm_w[l, 1], 2).reshape(1, LANES),
            "diff_lambda": diff_lambda[l],
            "subln_row": subln_w[l].reshape(1, LANES),
            "d256": d256[l], "d128": d128[l],
            "ret_norm_row": ret_norm_w[l].reshape(1, LANES),
            "conv_w": conv_w[l],
            "al_row": al_rows[l].reshape(1, LANES),
            "dt_row": dt_rows[l].reshape(1, LANES),
            "gdn_norm_row": gdn_norm_w[l].reshape(1, LANES),
            "w_branch": w_branch[l].astype(BF16),
            "w_out": w_out[l].astype(BF16),
        }
        mod = mods[l].reshape(8, 1, 3 * d)
        y_p, (k_l, v_l, r_l, g_l) = _layer(y_p, mod, ctx_row, l, wts, None, None, b_ctx, t_ctx)
        y_s, _ = _layer(y_s, mod, lat_row, l, wts, rope,
                        (cache_k, cache_v, state_ret, state_gdn), b_lat, t_lat)
        ks.append(k_l)
        vs.append(v_l)
        rs.append(r_l)
        gs.append(g_l)
    new_k = jnp.stack(ks, axis=1).reshape(b_ctx, depth, t_ctx, N_HEADS, 2, DK_A)
    new_v = jnp.stack(vs, axis=1).reshape(b_ctx, depth, t_ctx, N_HEADS, HEAD_W)
    return (y_p.reshape(b_ctx, t_ctx, d), y_s.reshape(b_lat, t_lat, d), new_k, new_v,
            jnp.stack(rs, axis=1), jnp.stack(gs, axis=1))
```

```python
import functools
import itertools
import math

import jax
import jax.numpy as jnp
from jax import lax
from jax.experimental import pallas as pl
from jax.experimental.pallas import tpu as pltpu

F32 = jnp.float32
BF16 = jnp.bfloat16
HIGHEST = lax.Precision.HIGHEST

N_HEADS = 4
DK_A = 64
DK_B = 64
HEAD_W = 128
CONV_K = 5
GRID_W = 64
ROPE_BASE = 10000.0
EPS = 1e-6
LANES = 128
CHUNK = 128
ATTN_SUB = 256
ATTN_KEYS = 256
ATTN_TQ = 1024
ATTN_SLOTS = 4
VT_ROWS = HEAD_W + 16
LOG2E = 1.4426950408889634
RET_UNROLL = 4
GDN_GROUP = 8
GDN_VMEM_BUDGET = 40 * 1024 * 1024
VMEM_LIMIT = 56 * 1024 * 1024

OFF_AQ, OFF_AK, OFF_AV, OFF_AZ = 0, 512, 1024, 1536
OFF_BQ, OFF_BK, OFF_BV, OFF_BZ = 2048, 2304, 2560, 3072
OFF_CQ, OFF_CK, OFF_CV, OFF_CZ = 3584, 4096, 4608, 5120
OFF_BA = 5632
N_MIX = 5632


def _cparams(sem):
    return pltpu.CompilerParams(dimension_semantics=sem, vmem_limit_bytes=VMEM_LIMIT)


def _sigmoid(x):
    return 1.0 / (1.0 + jnp.exp(-x))


def _silu(x):
    return x * _sigmoid(x)


def _softplus(x):
    return jnp.maximum(x, 0.0) + jnp.log1p(jnp.exp(-jnp.abs(x)))


def _dot(a, b):
    return jnp.dot(a, b, preferred_element_type=F32)


def _dot_nt(a, b):
    return lax.dot_general(a, b, (((1,), (1,)), ((), ())), preferred_element_type=F32)


def _lane_iota(shape):
    return lax.broadcasted_iota(jnp.int32, shape, len(shape) - 1)


def _row_iota(shape):
    return lax.broadcasted_iota(jnp.int32, shape, len(shape) - 2)


def _norm_halves(x):
    lo = _lane_iota(x.shape) < 64
    x2 = x * x
    s0 = jnp.sum(jnp.where(lo, x2, 0.0), axis=-1, keepdims=True)
    s1 = jnp.sum(jnp.where(lo, 0.0, x2), axis=-1, keepdims=True)
    return x * lax.rsqrt(jnp.where(lo, s0, s1) * (1.0 / 64.0) + EPS)


def _store_values_t(vall_ref, h, v):
    vall_ref[0, h, 0:HEAD_W, :] = v.astype(F32).T.astype(BF16)
    vall_ref[0, h, HEAD_W:VT_ROWS, :] = jnp.ones((VT_ROWS - HEAD_W, v.shape[0]), BF16)


def _rope(x, cos, sin_a, sin_b):
    return (x * cos + pltpu.roll(x, LANES - 16, 1) * sin_a + pltpu.roll(x, 16, 1) * sin_b)


def _ada_kernel(cond_ref, w_ref, b_ref, o_ref):
    c = cond_ref[...]
    o_ref[0] = jnp.dot(_silu(c), w_ref[0], precision=HIGHEST,
                       preferred_element_type=F32) + b_ref[0]


def _ada_call(cond, w_ada, b_ada):
    depth, d, d3 = w_ada.shape
    tn = 1024
    return pl.pallas_call(
        _ada_kernel,
        grid=(depth, d3 // tn),
        in_specs=[pl.BlockSpec((8, d), lambda l, j: (0, 0)),
                  pl.BlockSpec((1, d, tn), lambda l, j: (l, 0, j)),
                  pl.BlockSpec((1, 1, tn), lambda l, j: (l, 0, j))],
        out_specs=pl.BlockSpec((1, 8, tn), lambda l, j: (l, 0, j)),
        out_shape=jax.ShapeDtypeStruct((depth, 8, d3), F32),
        compiler_params=_cparams(("parallel", "parallel")),
    )(cond, w_ada, b_ada.reshape(depth, 1, d3))


def _relayout_kernel(w_ref, o_ref, *, mg_off):
    n_ba = N_MIX + 4 * N_HEADS
    n_in = w_ref.shape[2]
    rows, n_out = o_ref.shape[1], o_ref.shape[2]
    o_ref[0, :, 0:N_MIX] = w_ref[0, :, 0:N_MIX].astype(BF16)
    tail = jnp.concatenate([w_ref[0, :, N_MIX:n_ba], jnp.zeros((rows, LANES - 4 * N_HEADS), F32)],
                           axis=1)
    o_ref[0, :, N_MIX:N_MIX + LANES] = tail.astype(BF16)
    o_ref[0, :, N_MIX + LANES:mg_off] = jnp.zeros((rows, mg_off - N_MIX - LANES), BF16)
    o_ref[0, :, mg_off:mg_off + n_in - n_ba] = w_ref[0, :, n_ba:n_in].astype(BF16)
    if n_out > mg_off + n_in - n_ba:
        o_ref[0, :, mg_off + n_in - n_ba:n_out] = jnp.zeros(
            (rows, n_out - mg_off - n_in + n_ba), BF16)


def _relayout_call(w_in, mg_off, n_cols):
    depth, d, n_in = w_in.shape
    tr = 256
    return pl.pallas_call(
        functools.partial(_relayout_kernel, mg_off=mg_off),
        grid=(depth, d // tr),
        in_specs=[pl.BlockSpec((1, tr, n_in), lambda l, i: (l, i, 0))],
        out_specs=pl.BlockSpec((1, tr, n_cols), lambda l, i: (l, i, 0)),
        out_shape=jax.ShapeDtypeStruct((depth, d, n_cols), BF16),
        compiler_params=_cparams(("parallel", "parallel")),
    )(w_in)


def _proj_kernel(x_ref, mod_ref, nw_ref, w_ref, o_ref, ba_ref, h_ref, *, d, ba_tile, ba_off):
    j = pl.program_id(1)

    @pl.when(j == 0)
    def _():
        x = x_ref[...]
        y = x * lax.rsqrt(jnp.mean(x * x, axis=-1, keepdims=True) + EPS) * nw_ref[...]
        mod = mod_ref[0]
        h_ref[...] = (y * (1.0 + mod[:, d:2 * d]) + mod[:, 0:d]).astype(BF16)

    acc = _dot(h_ref[...], w_ref[...])
    o_ref[...] = acc.astype(BF16)

    @pl.when(j == ba_tile)
    def _():
        ba_ref[...] = acc[:, ba_off:ba_off + LANES]


def _proj_call(x2, mod, mod_row, norm_w, w_pad, tm):
    n, d = x2.shape
    n_cols = w_pad.shape[1]
    tn = PROJ_TN
    return pl.pallas_call(
        functools.partial(_proj_kernel, d=d, ba_tile=OFF_BA // tn, ba_off=OFF_BA % tn),
        grid=(n // tm, n_cols // tn),
        in_specs=[pl.BlockSpec((tm, d), lambda i, j: (i, 0)),
                  pl.BlockSpec((1, 1, 3 * d), lambda i, j: (mod_row(i), 0, 0)),
                  pl.BlockSpec((1, d), lambda i, j: (0, 0)),
                  pl.BlockSpec((d, tn), lambda i, j: (0, j))],
        out_specs=[pl.BlockSpec((tm, tn), lambda i, j: (i, j)),
                   pl.BlockSpec((tm, LANES), lambda i, j: (i, 0))],
        out_shape=[jax.ShapeDtypeStruct((n, n_cols), BF16),
                   jax.ShapeDtypeStruct((n, LANES), F32)],
        scratch_shapes=[pltpu.VMEM((tm, d), BF16)],
        compiler_params=_cparams(("parallel", "arbitrary")),
    )(x2, mod, norm_w, w_pad)


def _prep_ctx_kernel(ak_ref, av_ref, bq_ref, bk_ref, wk_ref,
                     kall_ref, vall_ref, rq_ref, rk_ref, ka_ref, va_ref):
    wk = wk_ref[...]
    for h in range(N_HEADS):
        sl = slice(h * HEAD_W, (h + 1) * HEAD_W)
        kn = _norm_halves(ak_ref[:, sl].astype(F32)) * wk
        ka_ref[0, :, sl] = kn
        kall_ref[0, h] = kn.astype(BF16)
        v = av_ref[:, sl]
        va_ref[0, :, sl] = v.astype(F32)
        _store_values_t(vall_ref, h, v)
    rq_ref[...] = bq_ref[...].astype(F32)
    rk_ref[...] = bk_ref[...].astype(F32) * (DK_B ** -0.5)


def _prep_lat_kernel(ak_ref, av_ref, bq_ref, bk_ref, wk_ref, cos_ref, sa_ref, sb_ref,
                     ck_ref, cv_ref, kall_ref, vall_ref, rq_ref, rk_ref, *, nt):
    t = pl.program_id(1)

    @pl.when(t < nt)
    def _():
        wk = wk_ref[...]
        cos, sa, sb = cos_ref[...], sa_ref[...], sb_ref[...]
        for h in range(N_HEADS):
            sl = slice(h * HEAD_W, (h + 1) * HEAD_W)
            kn = _norm_halves(ak_ref[:, sl].astype(F32)) * wk
            kall_ref[0, h] = _rope(kn, cos, sa, sb).astype(BF16)
            _store_values_t(vall_ref, h, av_ref[:, sl])
        for e in range(2):
            sl = slice(e * LANES, (e + 1) * LANES)
            rq_ref[:, sl] = _rope(bq_ref[:, sl].astype(F32), cos, sa, sb)
            rk_ref[:, sl] = _rope(bk_ref[:, sl].astype(F32), cos, sa, sb) * (DK_B ** -0.5)

    @pl.when(t == nt)
    def _():
        for h in range(N_HEADS):
            sl = slice(h * HEAD_W, (h + 1) * HEAD_W)
            kall_ref[0, h] = ck_ref[:, sl].astype(BF16)
            _store_values_t(vall_ref, h, cv_ref[:, sl])


def _prep_call(p, b, t, wk_row, rope, cache, layer):
    tk = 256
    nt = t // tk
    n = b * t
    if rope is None:
        return pl.pallas_call(
            _prep_ctx_kernel,
            grid=(b, nt),
            in_specs=[pl.BlockSpec((tk, 512), lambda bi, ti: (bi * nt + ti, OFF_AK // 512)),
                      pl.BlockSpec((tk, 512), lambda bi, ti: (bi * nt + ti, OFF_AV // 512)),
                      pl.BlockSpec((tk, 256), lambda bi, ti: (bi * nt + ti, OFF_BQ // 256)),
                      pl.BlockSpec((tk, 256), lambda bi, ti: (bi * nt + ti, OFF_BK // 256)),
                      pl.BlockSpec((1, LANES), lambda bi, ti: (0, 0))],
            out_specs=[pl.BlockSpec((1, N_HEADS, tk, HEAD_W), lambda bi, ti: (bi, 0, ti, 0)),
                       pl.BlockSpec((1, N_HEADS, VT_ROWS, tk), lambda bi, ti: (bi, 0, 0, ti)),
                       pl.BlockSpec((tk, 256), lambda bi, ti: (bi * nt + ti, 0)),
                       pl.BlockSpec((tk, 256), lambda bi, ti: (bi * nt + ti, 0)),
                       pl.BlockSpec((1, tk, 512), lambda bi, ti: (bi, ti, 0)),
                       pl.BlockSpec((1, tk, 512), lambda bi, ti: (bi, ti, 0))],
            out_shape=[jax.ShapeDtypeStruct((b, N_HEADS, t, HEAD_W), BF16),
                       jax.ShapeDtypeStruct((b, N_HEADS, VT_ROWS, t), BF16),
                       jax.ShapeDtypeStruct((n, 256), F32),
                       jax.ShapeDtypeStruct((n, 256), F32),
                       jax.ShapeDtypeStruct((b, t, 512), F32),
                       jax.ShapeDtypeStruct((b, t, 512), F32)],
            compiler_params=_cparams(("parallel", "parallel")),
        )(p, p, p, p, wk_row)
    cos, sa, sb = rope
    cache_k, cache_v = cache
    past = cache_k.shape[2]
    assert past == tk
    s = t + past

    def tok(bi, ti):
        return bi * nt + jnp.minimum(ti, nt - 1)

    return pl.pallas_call(
        functools.partial(_prep_lat_kernel, nt=nt),
        grid=(b, nt + 1),
        in_specs=[pl.BlockSpec((tk, 512), lambda bi, ti: (tok(bi, ti), OFF_AK // 512)),
                  pl.BlockSpec((tk, 512), lambda bi, ti: (tok(bi, ti), OFF_AV // 512)),
                  pl.BlockSpec((tk, 256), lambda bi, ti: (tok(bi, ti), OFF_BQ // 256)),
                  pl.BlockSpec((tk, 256), lambda bi, ti: (tok(bi, ti), OFF_BK // 256)),
                  pl.BlockSpec((1, LANES), lambda bi, ti: (0, 0)),
                  pl.BlockSpec((tk, LANES), lambda bi, ti: (jnp.minimum(ti, nt - 1), 0)),
                  pl.BlockSpec((tk, LANES), lambda bi, ti: (jnp.minimum(ti, nt - 1), 0)),
                  pl.BlockSpec((tk, LANES), lambda bi, ti: (jnp.minimum(ti, nt - 1), 0)),
                  pl.BlockSpec((None, None, past, 512), lambda bi, ti: (bi, layer, 0, 0)),
                  pl.BlockSpec((None, None, past, 512), lambda bi, ti: (bi, layer, 0, 0))],
        out_specs=[pl.BlockSpec((1, N_HEADS, tk, HEAD_W), lambda bi, ti: (bi, 0, ti, 0)),
                   pl.BlockSpec((1, N_HEADS, VT_ROWS, tk), lambda bi, ti: (bi, 0, 0, ti)),
                   pl.BlockSpec((tk, 256), lambda bi, ti: (tok(bi, ti), 0)),
                   pl.BlockSpec((tk, 256), lambda bi, ti: (tok(bi, ti), 0))],
        out_shape=[jax.ShapeDtypeStruct((b, N_HEADS, s, HEAD_W), BF16),
                   jax.ShapeDtypeStruct((b, N_HEADS, VT_ROWS, s), BF16),
                   jax.ShapeDtypeStruct((n, 256), F32),
                   jax.ShapeDtypeStruct((n, 256), F32)],
        compiler_params=_cparams(("parallel", "arbitrary")),
    )(p, p, p, p, wk_row, cos, sa, sb, cache_k, cache_v)


def _attn_kernel(*refs, lam_init, rope):
    if rope:
        (q_ref, z_ref, k_ref, v_ref, wq_ref, dl_ref, sw_ref, cos_ref, sa_ref, sb_ref,
         o_ref, s_ref, p_ref) = refs
    else:
        q_ref, z_ref, k_ref, v_ref, wq_ref, dl_ref, sw_ref, o_ref, s_ref, p_ref = refs
    n_keys = k_ref.shape[2]
    tq = q_ref.shape[0]
    lv = dl_ref[...]
    lam = (jnp.exp(jnp.sum(lv[0:1] * lv[1:2], axis=-1, keepdims=True))
           - jnp.exp(jnp.sum(lv[2:3] * lv[3:4], axis=-1, keepdims=True)) + lam_init)
    chains = [(r0, m) for r0 in range(0, tq, ATTN_SUB) for m in range(2)]
    slot = {chain: i % ATTN_SLOTS for i, chain in enumerate(chains)}
    q_t = {}
    col_max = {}
    den = {}
    acc = {}

    def stage_scores(r0, m):
        rows = slice(r0, r0 + ATTN_SUB)
        if r0 not in q_t:
            qn = _norm_halves(q_ref[rows, :].astype(F32)) * wq_ref[...]
            if rope:
                qn = _rope(qn, cos_ref[rows, :], sa_ref[rows, :], sb_ref[rows, :])
            q_t[r0] = (qn * (DK_A ** -0.5 * LOG2E)).T
        qt = q_t[r0]
        lo = _row_iota(qt.shape) < DK_A
        qm = jnp.where(lo if m == 0 else jnp.logical_not(lo), qt, 0.0).astype(BF16)
        for c0 in range(0, n_keys, ATTN_KEYS):
            s_ref[slot[r0, m], c0:c0 + ATTN_KEYS, :] = _dot(k_ref[0, 0, c0:c0 + ATTN_KEYS, :], qm)

    def stage_softmax(r0, m):
        rows = slice(r0, r0 + ATTN_SUB)
        sl = slot[r0, m]
        mx = s_ref[sl, 0:8, :]
        for j in range(1, n_keys // 8):
            mx = jnp.maximum(mx, s_ref[sl, j * 8:(j + 1) * 8, :])
        mx = jnp.max(mx, axis=0, keepdims=True)
        for c0 in range(0, n_keys, ATTN_KEYS):
            ks = slice(c0, c0 + ATTN_KEYS)
            p_ref[sl, ks, :] = jnp.exp2(s_ref[sl, ks, :] - mx).astype(BF16)

    def stage_values(r0, m):
        rows = slice(r0, r0 + ATTN_SUB)
        pv = _dot(v_ref[0, 0], p_ref[slot[r0, m]])
        acc[r0, m] = pv[0:HEAD_W, :]
        den[r0, m] = pv[HEAD_W:HEAD_W + 1, :]
        if m == 1:
            o = (acc[r0, 0] * (1.0 / den[r0, 0]) - lam * (acc[r0, 1] * (1.0 / den[r0, 1])))
            o = o.T
            o = o * lax.rsqrt(jnp.mean(o * o, axis=-1, keepdims=True) + EPS) * sw_ref[...]
            o_ref[rows, :] = o * (1.0 - lam_init) * _silu(z_ref[rows, :].astype(F32))

    for step in range(len(chains) + 2):
        if 0 <= step - 2 < len(chains):
            stage_values(*chains[step - 2])
        if 0 <= step - 1 < len(chains):
            stage_softmax(*chains[step - 1])
        if step < len(chains):
            stage_scores(*chains[step])


def _attn_call(p, kall, vall, wq_row, dlam, subln_row, rope, b, t, lam_init):
    tq = min(ATTN_TQ, t)
    s = kall.shape[2]
    nq = t // tq
    in_specs = [pl.BlockSpec((tq, HEAD_W), lambda bi, h, i: (bi * nq + i, OFF_AQ // HEAD_W + h)),
                pl.BlockSpec((tq, HEAD_W), lambda bi, h, i: (bi * nq + i, OFF_AZ // HEAD_W + h)),
                pl.BlockSpec((1, 1, s, HEAD_W), lambda bi, h, i: (bi, h, 0, 0)),
                pl.BlockSpec((1, 1, VT_ROWS, s), lambda bi, h, i: (bi, h, 0, 0)),
                pl.BlockSpec((1, LANES), lambda bi, h, i: (0, 0)),
                pl.BlockSpec((4, DK_A), lambda bi, h, i: (0, 0)),
                pl.BlockSpec((1, LANES), lambda bi, h, i: (0, 0))]
    args = [p, p, kall, vall, wq_row, dlam, subln_row]
    if rope is not None:
        in_specs += [pl.BlockSpec((tq, LANES), lambda bi, h, i: (i, 0))] * 3
        args += list(rope)
    return pl.pallas_call(
        functools.partial(_attn_kernel, lam_init=lam_init, rope=rope is not None),
        grid=(b, N_HEADS, nq),
        in_specs=in_specs,
        out_specs=pl.BlockSpec((tq, HEAD_W), lambda bi, h, i: (bi * nq + i, h)),
        out_shape=jax.ShapeDtypeStruct((b * t, N_HEADS * HEAD_W), F32),
        scratch_shapes=[pltpu.VMEM((ATTN_SLOTS, s, ATTN_SUB), F32),
                        pltpu.VMEM((ATTN_SLOTS, s, ATTN_SUB), BF16)],
        compiler_params=_cparams(("parallel", "parallel", "arbitrary")),
    )(*args)


def _ret_kernel(*refs, nchunk, has_state):
    if has_state:
        (q_ref, k_ref, v_ref, z_ref, d256_ref, d128_ref, nw_ref, s0_ref,
         o_ref, s_ref, dm_ref, qd_ref, kd_ref, ob_ref) = refs
        st_ref = None
    else:
        (q_ref, k_ref, v_ref, z_ref, d256_ref, d128_ref, nw_ref,
         o_ref, st_ref, s_ref, dm_ref, qd_ref, kd_ref, ob_ref) = refs
    c_len = CHUNK
    lg256 = -_softplus(-d256_ref[...])
    lg128 = -_softplus(-d128_ref[...])
    ri = _row_iota((c_len, c_len)).astype(F32)
    ci = _lane_iota((c_len, c_len)).astype(F32)
    rcol256 = _row_iota((c_len, 2 * HEAD_W)).astype(F32)
    rcol128 = _row_iota((c_len, LANES)).astype(F32)
    blockmask = ((_row_iota((2 * DK_B, 2 * HEAD_W)) < DK_B)
                 == (_lane_iota((2 * DK_B, 2 * HEAD_W)) < HEAD_W))
    for d in range(2):
        rel = (ri - ci) if d == 0 else (ci - ri)
        keep = rel >= 0
        for e in range(2):
            lg = lg256[d:d + 1, e * HEAD_W:e * HEAD_W + 1]
            dm_ref[d, e] = jnp.where(keep, jnp.exp(jnp.where(keep, rel, 0.0) * lg), 0.0)
        qpow = (rcol256 + 1.0) if d == 0 else (c_len - rcol256)
        qd_ref[d] = jnp.exp(qpow * lg256[d:d + 1])
        kpow = (c_len - 1.0 - rcol128) if d == 0 else rcol128
        kd_ref[d] = jnp.exp(kpow * lg128[d:d + 1])
        if has_state:
            s_ref[d] = jnp.zeros((2 * DK_B, 2 * HEAD_W), F32)
            for e in range(2):
                s_ref[d, e * DK_B:(e + 1) * DK_B, e * HEAD_W:(e + 1) * HEAD_W] = s0_ref[d, e]
        else:
            s_ref[d] = jnp.zeros((2 * DK_B, 2 * HEAD_W), F32)
    cdec = [jnp.exp(float(c_len) * lg256[d:d + 1]) for d in range(2)]
    lane_lo = _lane_iota((c_len, LANES)) < DK_B
    nw = nw_ref[...]

    unroll = min(RET_UNROLL, nchunk)
    assert nchunk % unroll == 0

    def scan_body(i, carry):
        items = [(d, pl.multiple_of((i * unroll + u if d == 0 else nchunk - 1 - i * unroll - u)
                                    * c_len, c_len))
                 for u in range(unroll) for d in range(2)]
        q = [q_ref[pl.ds(t0, c_len), :] for _, t0 in items]
        k = [k_ref[pl.ds(t0, c_len), :] for _, t0 in items]
        vb = [v_ref[pl.ds(t0, c_len), :] for _, t0 in items]
        kb = [x.astype(BF16) for x in k]
        heads = [(n, e) for n in range(len(items)) for e in range(2)]
        qe = [jnp.where(lane_lo if e == 0 else jnp.logical_not(lane_lo), q[n], 0.0).astype(BF16)
              for n, e in heads]
        sc = [_dot_nt(qe_i, kb[n]) * dm_ref[items[n][0], e] for qe_i, (n, e) in zip(qe, heads)]
        kv = [lax.dot_general((k[n] * kd_ref[d]).astype(BF16), vb[n], (((0,), (0,)), ((), ())),
                              preferred_element_type=F32) for n, (d, _) in enumerate(items)]
        intra = [_dot(sc_i.astype(BF16), vb[n][:, e * HEAD_W:(e + 1) * HEAD_W])
                 for sc_i, (n, e) in zip(sc, heads)]
        st = [s_ref[d] for d in range(2)]
        for n, (d, t0) in enumerate(items):
            inter = _dot(q[n].astype(BF16), st[d].astype(BF16)) * qd_ref[d]
            dst = o_ref if d == 0 else ob_ref
            for e in range(2):
                sl = slice(e * HEAD_W, (e + 1) * HEAD_W)
                dst[pl.ds(t0, c_len), sl] = intra[2 * n + e] + inter[:, sl]
            st[d] = jnp.where(blockmask, st[d] * cdec[d] + kv[n], 0.0)
        for d in range(2):
            s_ref[d] = st[d]
        return carry

    lax.fori_loop(0, nchunk // unroll, scan_body, 0)

    def fin_body(i, carry):
        t0 = pl.multiple_of(i * c_len, c_len)
        for e in range(2):
            sl = slice(e * HEAD_W, (e + 1) * HEAD_W)
            o_e = o_ref[pl.ds(t0, c_len), sl] + ob_ref[pl.ds(t0, c_len), sl]
            o_e = o_e * lax.rsqrt(jnp.mean(o_e * o_e, axis=-1, keepdims=True) + EPS) * nw
            o_ref[pl.ds(t0, c_len), sl] = o_e * _silu(z_ref[pl.ds(t0, c_len), sl].astype(F32))
        return carry

    lax.fori_loop(0, nchunk, fin_body, 0)
    if not has_state:
        for d in range(2):
            for e in range(2):
                st_ref[d, e] = s_ref[d, e * DK_B:(e + 1) * DK_B, e * HEAD_W:(e + 1) * HEAD_W]


def _ret_call(p, rq, rk, d256, d128, nw_row, state, b, t, layer):
    nchunk = t // CHUNK
    hp = N_HEADS // 2
    in_specs = [pl.BlockSpec((t, LANES), lambda bi, h: (bi, h)),
                pl.BlockSpec((t, LANES), lambda bi, h: (bi, h)),
                pl.BlockSpec((t, 256), lambda bi, h: (bi, OFF_BV // 256 + h)),
                pl.BlockSpec((t, 256), lambda bi, h: (bi, OFF_BZ // 256 + h)),
                pl.BlockSpec((None, 2, 256), lambda bi, h: (h, 0, 0)),
                pl.BlockSpec((None, 2, LANES), lambda bi, h: (h, 0, 0)),
                pl.BlockSpec((1, LANES), lambda bi, h: (0, 0))]
    args = [rq, rk, p, p, d256, d128, nw_row]
    scratch = [pltpu.VMEM((2, 2 * DK_B, 2 * HEAD_W), F32),
               pltpu.VMEM((2, 2, CHUNK, CHUNK), F32),
               pltpu.VMEM((2, CHUNK, 2 * HEAD_W), F32),
               pltpu.VMEM((2, CHUNK, LANES), F32),
               pltpu.VMEM((t, 2 * HEAD_W), F32)]
    o_spec = pl.BlockSpec((t, 256), lambda bi, h: (bi, h))
    o_shape = jax.ShapeDtypeStruct((b * t, N_HEADS * HEAD_W), F32)
    if state is not None:
        in_specs.append(pl.BlockSpec((None, None, 2, 2, DK_B, HEAD_W),
                                     lambda bi, h: (bi, layer, 0, h, 0, 0)))
        args.append(state)
        out_specs, out_shape = o_spec, o_shape
    else:
        out_specs = [o_spec, pl.BlockSpec((None, 2, 2, DK_B, HEAD_W), lambda bi, h: (bi, 0, h, 0, 0))]
        out_shape = [o_shape, jax.ShapeDtypeStruct((b, 2, N_HEADS, DK_B, HEAD_W), F32)]
    return pl.pallas_call(
        functools.partial(_ret_kernel, nchunk=nchunk, has_state=state is not None),
        grid=(b, hp),
        in_specs=in_specs,
        out_specs=out_specs,
        out_shape=out_shape,
        scratch_shapes=scratch,
        compiler_params=_cparams(("parallel", "parallel")),
    )(*args)


def _gdn_kernel(*refs, t, nh, has_state):
    if has_state:
        (cq_ref, ck_ref, cv_ref, cz_ref, ba_ref, wq_ref, wk_ref, wv_ref, al_ref, dt_ref, nw_ref,
         s0_ref, o_ref, qs_ref, ks_ref, vs_ref, u_ref, w_ref, qk_ref, qg_ref, kdt_ref, et_ref,
         ob_ref, s_ref, xw_ref) = refs
        st_ref = None
    else:
        (cq_ref, ck_ref, cv_ref, cz_ref, ba_ref, wq_ref, wk_ref, wv_ref, al_ref, dt_ref, nw_ref,
         o_ref, st_ref, qs_ref, ks_ref, vs_ref, u_ref, w_ref, qk_ref, qg_ref, kdt_ref, et_ref,
         ob_ref, s_ref, xw_ref) = refs
    c_len = CHUNK
    nchunk = t // c_len
    head0 = pl.program_id(1) * nh
    pad = 16
    win = 256
    nwin = t // win

    def conv_window(r0, first, last):
        lo = 0 if first else pad
        hi = 0 if last else pad
        start = r0 - lo if isinstance(r0, int) else pl.multiple_of(r0 - lo, pad)
        for hh, (a, (src, cw_ref, dst)) in itertools.product(
                range(nh), enumerate(((cq_ref, wq_ref, qs_ref), (ck_ref, wk_ref, ks_ref),
                                      (cv_ref, wv_ref, vs_ref)))):
            hl = slice(hh * HEAD_W, (hh + 1) * HEAD_W)
            xw = src[pl.ds(start, win + lo + hi), hl].astype(F32)
            if first:
                xw = jnp.concatenate([jnp.zeros((pad, LANES), F32), xw], axis=0)
            if last:
                xw = jnp.concatenate([xw, jnp.zeros((pad, LANES), F32)], axis=0)
            cw = cw_ref[:, hl]
            xw_ref[...] = xw
            acc = None
            for j in range(CONV_K):
                off = pad - CONV_K // 2 + j
                term = xw_ref[off:off + win, :] * cw[j:j + 1, :]
                acc = term if acc is None else acc + term
            y = _silu(acc)
            if a < 2:
                y = y * lax.rsqrt(jnp.sum(y * y, axis=-1, keepdims=True) + EPS)
            if a == 0:
                y = y * (HEAD_W ** -0.5)
            dst[hh, pl.ds(r0, win), :] = y

    if nwin == 1:
        conv_window(0, True, True)
    else:
        conv_window(0, True, False)

        def conv_body(i, carry):
            conv_window(pl.multiple_of(i * win, win), False, False)
            return carry

        lax.fori_loop(1, nwin - 1, conv_body, 0)
        conv_window((nwin - 1) * win, False, True)

    ri = _row_iota((c_len, c_len))
    ci = _lane_iota((c_len, c_len))
    lane = _lane_iota((c_len, LANES))
    alog = al_ref[...]
    dtb = dt_ref[...]
    tri16 = jnp.where(ri >= ci, 1.0, 0.0).astype(BF16)
    g_lanes = jnp.logical_and(lane >= 2 * N_HEADS, lane < 4 * N_HEADS)

    def lanes3(tgt):
        return jnp.logical_or(lane == tgt, jnp.logical_or(lane == tgt + 16, lane == tgt + 32))

    def chunk_chains(c):
        t0 = pl.multiple_of(c * c_len, c_len)
        ba = ba_ref[pl.ds(t0, c_len), :]
        beta_all = _sigmoid(ba)
        g_all = jnp.where(g_lanes, -jnp.exp(alog) * _softplus(ba + dtb), 0.0)
        g_hi = g_all.astype(BF16).astype(F32)
        r1 = g_all - g_hi
        g_mid = r1.astype(BF16).astype(F32)
        g_lo = (r1 - g_mid).astype(BF16).astype(F32)
        parts = (g_hi + pltpu.roll(g_mid, 16, 1) + pltpu.roll(g_lo, 32, 1)).astype(BF16)
        pre = _dot(tri16, parts)
        chains = []
        for hh in range(nh):
            h = head0 + hh
            tgt_f = 2 * N_HEADS + h
            tgt_b = 3 * N_HEADS + h
            gc_f = jnp.sum(jnp.where(lanes3(tgt_f), pre, 0.0), axis=-1, keepdims=True)
            pre_b = jnp.sum(jnp.where(lanes3(tgt_b), pre, 0.0), axis=-1, keepdims=True)
            g_b = jnp.sum(jnp.where(lane == tgt_b, g_all, 0.0), axis=-1, keepdims=True)
            gc_b = pre_b[c_len - 1:c_len, :] - pre_b + g_b
            beta_f = jnp.sum(jnp.where(lane == h, beta_all, 0.0), axis=-1, keepdims=True)
            beta_b = jnp.sum(jnp.where(lane == N_HEADS + h, beta_all, 0.0), axis=-1,
                             keepdims=True)
            rows = jnp.where(lane == 0, gc_f, jnp.where(lane == 1, gc_b, 0.0)).T

            q = qs_ref[hh, pl.ds(t0, c_len), :]
            k = ks_ref[hh, pl.ds(t0, c_len), :]
            v = vs_ref[hh, pl.ds(t0, c_len), :]
            k16 = k.astype(BF16)
            kkqk = _dot_nt(jnp.concatenate([k16, q.astype(BF16)], axis=0), k16)
            kk = kkqk[0:c_len, :]
            qk_raw = kkqk[c_len:2 * c_len, :]
            for d, (gc, beta) in enumerate(((gc_f, beta_f), (gc_b, beta_b))):
                gr = rows[d:d + 1, :]
                incl = (ri >= ci) if d == 0 else (ci >= ri)
                strict = (ri > ci) if d == 0 else (ci > ri)
                ex = jnp.exp(jnp.where(incl, gc - gr, 0.0))
                a_mat = jnp.where(strict, ex, 0.0) * kk * beta
                egc = jnp.exp(gc)
                rhs = jnp.concatenate([v * beta, k * (beta * egc)], axis=1)
                qk_ref[hh, d, pl.ds(t0, c_len), :] = (jnp.where(incl, ex, 0.0)
                                                      * qk_raw).astype(BF16)
                qg_ref[hh, d, pl.ds(t0, c_len), :] = (q * egc).astype(BF16)
                g_tot = gc[c_len - 1:c_len, :] if d == 0 else gc[0:1, :]
                kdt_ref[hh, d, pl.ds(t0, c_len), :] = (k * jnp.exp(g_tot - gc)).T.astype(BF16)
                et_ref[hh, d, c] = jnp.broadcast_to(jnp.exp(g_tot), (8, LANES))
                chains.append((a_mat, rhs, (hh, d, t0)))
        return chains

    group = min(GDN_GROUP, nchunk)
    assert nchunk % group == 0

    def prep_body(i, carry):
        chains = []
        for j in range(group):
            chains += chunk_chains(i * group + j)
        n_mats = [-jnp.where((ri >> 1) == (ci >> 1), a_mat, 0.0) for a_mat, _, _ in chains]
        sh = 1
        while (1 << sh) < c_len:
            off = jnp.logical_and((ri >> (sh + 1)) == (ci >> (sh + 1)), (ri >> sh) != (ci >> sh))
            l_mats = [jnp.where(off, a_mat, 0.0) for a_mat, _, _ in chains]
            x_mats = [l + _dot(l.astype(BF16), n.astype(BF16)) for l, n in zip(l_mats, n_mats)]
            n_mats = [n - x - _dot(n.astype(BF16), x.astype(BF16)) for n, x in zip(n_mats, x_mats)]
            sh += 1
        for n_mat, (_, rhs, (hh, d, t0)) in zip(n_mats, chains):
            uw = rhs + _dot(n_mat.astype(BF16), rhs.astype(BF16))
            kq = jnp.concatenate([kdt_ref[hh, d, pl.ds(t0, c_len), :],
                                  qk_ref[hh, d, pl.ds(t0, c_len), :]], axis=0)
            m = _dot(kq, uw.astype(BF16))
            u_ref[hh, d, pl.ds(t0, c_len), :] = m[0:c_len, 0:HEAD_W]
            w_ref[hh, d, pl.ds(t0, c_len), :] = m[0:c_len, HEAD_W:2 * HEAD_W].astype(BF16)
            qg_ref[hh, d, pl.ds(t0, c_len), :] = (
                qg_ref[hh, d, pl.ds(t0, c_len), :].astype(F32)
                - m[c_len:2 * c_len, HEAD_W:2 * HEAD_W]).astype(BF16)
            dst = o_ref if d == 0 else ob_ref
            dst[pl.ds(t0, c_len), hh * HEAD_W:(hh + 1) * HEAD_W] = m[c_len:2 * c_len, 0:HEAD_W]
        return carry

    lax.fori_loop(0, nchunk // group, prep_body, 0)

    for hh, d in itertools.product(range(nh), range(2)):
        s_ref[hh, d] = s0_ref[d, hh] if has_state else jnp.zeros((HEAD_W, HEAD_W), F32)

    def scan_body(i, carry):
        chains = [(hh, d, c, pl.multiple_of(c * c_len, c_len))
                  for hh, (d, c) in itertools.product(range(nh), ((0, i), (1, nchunk - 1 - i)))]
        st = [s_ref[hh, d] for hh, d, _, _ in chains]
        st16 = [x.astype(BF16) for x in st]
        ks = [_dot(w_ref[hh, d, pl.ds(t0, c_len), :], s16)
              for (hh, d, _, t0), s16 in zip(chains, st16)]
        for (hh, d, c, t0), s_old, x in zip(chains, st, ks):
            s_ref[hh, d] = (s_old * et_ref[hh, d, c][0:1, :] - x
                            + u_ref[hh, d, pl.ds(t0, c_len), :])
        for (hh, d, _, t0), s16 in zip(chains, st16):
            dst = o_ref if d == 0 else ob_ref
            hl = slice(hh * HEAD_W, (hh + 1) * HEAD_W)
            dst[pl.ds(t0, c_len), hl] = (dst[pl.ds(t0, c_len), hl]
                                         + _dot(qg_ref[hh, d, pl.ds(t0, c_len), :], s16))
        return carry

    lax.fori_loop(0, nchunk, scan_body, 0)

    def fin_body(i, carry):
        r0 = pl.multiple_of(i * win, win)
        for hh in range(nh):
            hl = slice(hh * HEAD_W, (hh + 1) * HEAD_W)
            o = o_ref[pl.ds(r0, win), hl] + ob_ref[pl.ds(r0, win), hl]
            o = o * lax.rsqrt(jnp.mean(o * o, axis=-1, keepdims=True) + EPS) * nw_ref[...]
            o_ref[pl.ds(r0, win), hl] = o * _silu(cz_ref[pl.ds(r0, win), hl].astype(F32))
        return carry

    lax.fori_loop(0, nwin, fin_body, 0)
    if not has_state:
        for hh, d in itertools.product(range(nh), range(2)):
            st_ref[d, hh] = s_ref[hh, d]


def _gdn_heads_per_program(t):
    per_head = t * LANES * (4 * 2 * 2 + 4 * 2 + 3 * 4 + 2 * 4 + 4 * 2 * 2 + 4)
    nh = N_HEADS
    while nh > 1 and nh * per_head > GDN_VMEM_BUDGET:
        nh //= 2
    return nh


def _gdn_call(p, p_ba, conv_w, al_row, dt_row, nw_row, state, b, t, layer):
    nh = _gdn_heads_per_program(t)
    wd = nh * HEAD_W

    def col(off):
        return lambda bi, h: (bi, off // wd + h)

    in_specs = [pl.BlockSpec((t, wd), col(OFF_CQ)),
                pl.BlockSpec((t, wd), col(OFF_CK)),
                pl.BlockSpec((t, wd), col(OFF_CV)),
                pl.BlockSpec((t, wd), col(OFF_CZ)),
                pl.BlockSpec((t, LANES), lambda bi, h: (bi, 0)),
                pl.BlockSpec((CONV_K, wd), lambda bi, h: (0, h)),
                pl.BlockSpec((CONV_K, wd), lambda bi, h: (0, N_HEADS // nh + h)),
                pl.BlockSpec((CONV_K, wd), lambda bi, h: (0, 2 * (N_HEADS // nh) + h)),
                pl.BlockSpec((1, LANES), lambda bi, h: (0, 0)),
                pl.BlockSpec((1, LANES), lambda bi, h: (0, 0)),
                pl.BlockSpec((1, LANES), lambda bi, h: (0, 0))]
    args = [p, p, p, p, p_ba, conv_w, conv_w, conv_w, al_row, dt_row, nw_row]
    scratch = [pltpu.VMEM((nh, t, LANES), F32),
               pltpu.VMEM((nh, t, LANES), F32),
               pltpu.VMEM((nh, t, LANES), F32),
               pltpu.VMEM((nh, 2, t, LANES), F32),
               pltpu.VMEM((nh, 2, t, LANES), BF16),
               pltpu.VMEM((nh, 2, t, LANES), BF16),
               pltpu.VMEM((nh, 2, t, LANES), BF16),
               pltpu.VMEM((nh, 2, t, LANES), BF16),
               pltpu.VMEM((nh, 2, t // CHUNK, 8, LANES), F32),
               pltpu.VMEM((t, wd), F32),
               pltpu.VMEM((nh, 2, HEAD_W, HEAD_W), F32),
               pltpu.VMEM((256 + 2 * 16, LANES), F32)]
    o_spec = pl.BlockSpec((t, wd), lambda bi, h: (bi, h))
    o_shape = jax.ShapeDtypeStruct((b * t, N_HEADS * HEAD_W), F32)
    if state is not None:
        in_specs.append(pl.BlockSpec((None, None, 2, nh, HEAD_W, HEAD_W),
                                     lambda bi, h: (bi, layer, 0, h, 0, 0)))
        args.append(state)
        out_specs, out_shape = o_spec, o_shape
    else:
        out_specs = [o_spec, pl.BlockSpec((None, 2, nh, HEAD_W, HEAD_W),
                                          lambda bi, h: (bi, 0, h, 0, 0))]
        out_shape = [o_shape, jax.ShapeDtypeStruct((b, 2, N_HEADS, HEAD_W, HEAD_W), F32)]
    return pl.pallas_call(
        functools.partial(_gdn_kernel, t=t, nh=nh, has_state=state is not None),
        grid=(b, N_HEADS // nh),
        in_specs=in_specs,
        out_specs=out_specs,
        out_shape=out_shape,
        scratch_shapes=scratch,
        compiler_params=_cparams(("parallel", "parallel")),
    )(*args)


def _out_kernel(ya_ref, yb_ref, yc_ref, mg_ref, x_ref, mod_ref, wb_ref, wo_ref, o_ref, *, d):
    ya = _dot(ya_ref[...].astype(BF16), wb_ref[0])
    yb = _dot(yb_ref[...].astype(BF16), wb_ref[1])
    yc = _dot(yc_ref[...].astype(BF16), wb_ref[2])
    y = (_sigmoid(mg_ref[:, 0:d].astype(F32)) * ya + _sigmoid(mg_ref[:, d:2 * d].astype(F32)) * yb
         + _sigmoid(mg_ref[:, 2 * d:3 * d].astype(F32)) * yc)
    out = _dot(y.astype(BF16), wo_ref[...])
    gate = mod_ref[0][:, 2 * d:3 * d]
    o_ref[...] = x_ref[...] + gate * out


def _out_call(ya, yb, yc, p, x2, mod, mod_row, wb, wo, mg_block, tm):
    n, d = x2.shape
    w_br = ya.shape[1]
    return pl.pallas_call(
        functools.partial(_out_kernel, d=d),
        grid=(n // tm,),
        in_specs=[pl.BlockSpec((tm, w_br), lambda i: (i, 0)),
                  pl.BlockSpec((tm, w_br), lambda i: (i, 0)),
                  pl.BlockSpec((tm, w_br), lambda i: (i, 0)),
                  pl.BlockSpec((tm, 3 * d), lambda i: (i, mg_block)),
                  pl.BlockSpec((tm, d), lambda i: (i, 0)),
                  pl.BlockSpec((1, 1, 3 * d), lambda i: (mod_row(i), 0, 0)),
                  pl.BlockSpec((3, w_br, d), lambda i: (0, 0, 0)),
                  pl.BlockSpec((d, d), lambda i: (0, 0))],
        out_specs=pl.BlockSpec((tm, d), lambda i: (i, 0)),
        out_shape=jax.ShapeDtypeStruct((n, d), F32),
        compiler_params=_cparams(("parallel",)),
    )(ya, yb, yc, p, x2, mod, wb, wo)


def _rope_tables(n_tokens, dtype):
    n_rows = n_tokens // GRID_W
    row = jnp.repeat(jnp.arange(n_rows, dtype=jnp.float32), GRID_W)
    col = jnp.tile(jnp.arange(GRID_W, dtype=jnp.float32), n_rows)
    n_freq = DK_A // 4
    inv = 1.0 / (ROPE_BASE ** (jnp.arange(n_freq, dtype=jnp.float32) / n_freq))
    ar = row[:, None] * inv
    ac = col[:, None] * inv
    ang = jnp.concatenate([ar, ar, ac, ac], axis=-1)
    cos = jnp.tile(jnp.cos(ang).astype(dtype), (1, 2))
    sin = jnp.tile(jnp.sin(ang).astype(dtype), (1, 2))
    first = (jnp.arange(LANES) % 32) < 16
    sin_a = jnp.where(first, -sin, 0.0)
    sin_b = jnp.where(first, 0.0, sin)
    return cos, sin_a, sin_b


PROJ_TM = 2048
PROJ_TN = 1536
OUT_TM = 512


def _layer(x2, mod, mod_row_for, layer, wts, rope, ctx, b, t):
    p, p_ba = _proj_call(x2, mod, mod_row_for(PROJ_TM), wts["norm_w"], wts["w_in"], PROJ_TM)
    cache = None if ctx is None else (ctx[0], ctx[1])
    prep = _prep_call(p, b, t, wts["wk_row"], rope, cache, layer)
    kall, vall, rq, rk = prep[:4]
    lam_init = 0.8 - 0.6 * math.exp(-0.3 * layer)
    ya = _attn_call(p, kall, vall, wts["wq_row"], wts["diff_lambda"], wts["subln_row"], rope,
                    b, t, lam_init)
    ret = _ret_call(p, rq, rk, wts["d256"], wts["d128"], wts["ret_norm_row"],
                    None if ctx is None else ctx[2], b, t, layer)
    gdn = _gdn_call(p, p_ba, wts["conv_w"], wts["al_row"], wts["dt_row"], wts["gdn_norm_row"],
                    None if ctx is None else ctx[3], b, t, layer)
    if ctx is None:
        yb, s_ret = ret
        yc, s_gdn = gdn
        extras = (prep[4], prep[5], s_ret, s_gdn)
    else:
        yb, yc = ret, gdn
        extras = None
    mg_block = wts["mg_off"] // (3 * x2.shape[1])
    x2 = _out_call(ya, yb, yc, p, x2, mod, mod_row_for(OUT_TM), wts["w_branch"], wts["w_out"],
                   mg_block, OUT_TM)
    return x2, extras


def kernel(x_prompt, x_sample, cache_attn_k, cache_attn_v, state_ret, state_gdn, c, c_ctx,
           norm_w, w_ada, b_ada, w_in, qk_norm_w, diff_lambda, subln_w, ret_decay, ret_norm_w,
           conv_w, gdn_a_log, gdn_dt_bias, gdn_norm_w, w_branch, w_out):
    b_ctx, t_ctx, d = x_prompt.shape
    b_lat, t_lat, _ = x_sample.shape
    depth = w_in.shape[0]
    past = cache_attn_k.shape[2]
    assert b_lat <= 4 and d % LANES == 0

    cond = jnp.zeros((8, d), F32).at[:b_lat].set(c).at[4].set(c_ctx)
    mods = _ada_call(cond, w_ada, b_ada)

    mg_off = -(-(OFF_BA + LANES) // (3 * d)) * (3 * d)
    n_cols = mg_off + 3 * d
    n_cols = -(-n_cols // PROJ_TN) * PROJ_TN
    w_pad = _relayout_call(w_in, mg_off, n_cols)

    rope = _rope_tables(t_lat, x_sample.dtype)
    cache_k = cache_attn_k.reshape(b_lat, depth, past, N_HEADS * 2 * DK_A)
    cache_v = cache_attn_v.reshape(b_lat, depth, past, N_HEADS * HEAD_W)

    lanes16 = jnp.zeros((depth, LANES), F32)
    al_rows = lanes16.at[:, 8:16].set(gdn_a_log.reshape(depth, 8))
    dt_rows = lanes16.at[:, 8:16].set(gdn_dt_bias.reshape(depth, 8))
    dec = ret_decay.reshape(depth, 2, 2, 2)
    dec = jnp.transpose(dec, (0, 2, 1, 3))
    d256 = jnp.repeat(dec, HEAD_W, axis=-1)
    d128 = jnp.repeat(dec, DK_B, axis=-1)

    y_p = x_prompt.reshape(b_ctx * t_ctx, d)
    y_s = x_sample.reshape(b_lat * t_lat, d)
    assert t_lat % PROJ_TM == 0 and (b_ctx * t_ctx) % PROJ_TM == 0

    def ctx_row(tm):
        return lambda i: 4

    def lat_row(tm):
        return lambda i: i // (t_lat // tm)

    ks, vs, rs, gs = [], [], [], []
    for l in range(depth):
        wts = {
            "norm_w": norm_w[l].reshape(1, d),
            "w_in": w_pad[l],
            "mg_off": mg_off,
            "wq_row": jnp.tile(qk_norm_w[l, 0], 2).reshape(1, LANES),
            "wk_row": jnp.tile(qk_norm_w[l, 1], 2).reshape(1, LANES),
            "diff_lambda": diff_lambda[l],
            "subln_row": subln_w[l].reshape(1, LANES),
            "d256": d256[l], "d128": d128[l],
            "ret_norm_row": ret_norm_w[l].reshape(1, LANES),
            "conv_w": conv_w[l],
            "al_row": al_rows[l].reshape(1, LANES),
            "dt_row": dt_rows[l].reshape(1, LANES),
            "gdn_norm_row": gdn_norm_w[l].reshape(1, LANES),
            "w_branch": w_branch[l].astype(BF16),
            "w_out": w_out[l].astype(BF16),
        }
        mod = mods[l].reshape(8, 1, 3 * d)
        y_p, (k_l, v_l, r_l, g_l) = _layer(y_p, mod, ctx_row, l, wts, None, None, b_ctx, t_ctx)
        y_s, _ = _layer(y_s, mod, lat_row, l, wts, rope,
                        (cache_k, cache_v, state_ret, state_gdn), b_lat, t_lat)
        ks.append(k_l)
        vs.append(v_l)
        rs.append(r_l)
        gs.append(g_l)
    new_k = jnp.stack(ks, axis=1).reshape(b_ctx, depth, t_ctx, N_HEADS, 2, DK_A)
    new_v = jnp.stack(vs, axis=1).reshape(b_ctx, depth, t_ctx, N_HEADS, HEAD_W)
    return (y_p.reshape(b_ctx, t_ctx, d), y_s.reshape(b_lat, t_lat, d), new_k, new_v,
            jnp.stack(rs, axis=1), jnp.stack(gs, axis=1))
```
